```python
import math
import jax, jax.numpy as jnp
from jax import lax
import numpy as np

D_MODEL = 1024
BATCH = 16
SEQ = 2048
DEPTH = 1
DEC_BATCH = 128
DEC_SEQ = 4
PAST_LEN = 8192
PAGE_SIZE = 128

MIX_WIDTH = D_MODEL
GDN_DK = 64
GDN_DV = 64
N_GDN_HEADS = (MIX_WIDTH // 2) // GDN_DV
CONV_W = 4
GDN_CHUNK = 64
DIFF_DH = 32
DIFF_DV = 2 * DIFF_DH
N_DIFF_HEADS = (MIX_WIDTH // 2) // DIFF_DV
Q_BLOCK = 128
N_BUCKETS = 32
MAX_DISTANCE = 128
N_GROUPS = 4
EXPERTS_PER_GROUP = 8
N_EXPERTS = N_GROUPS * EXPERTS_PER_GROUP
TOP_K = 2
EXPERT_FF = D_MODEL // 2
MOE_BLOCK = 128
NORM_EPS = 1e-6
SUBLN_EPS = 1e-5

GDN_QK = N_GDN_HEADS * GDN_DK
GDN_VW = N_GDN_HEADS * GDN_DV
CONV_DIM = 2 * GDN_QK + GDN_VW
DIFF_QK = N_DIFF_HEADS * 2 * DIFF_DH
DIFF_VW = N_DIFF_HEADS * DIFF_DV
IN_SPLITS = (CONV_DIM, GDN_VW, N_GDN_HEADS, N_GDN_HEADS, DIFF_QK, DIFF_QK, DIFF_VW)
IN_COLS = sum(IN_SPLITS)
F32 = jnp.float32

kernel_name = 'hymba_gdn_diffattn_hier_moe_step'


def rmsnorm(x, w, eps=NORM_EPS):
    xf = x.astype(F32)
    y = xf * lax.rsqrt(jnp.mean(xf * xf, axis=-1, keepdims=True) + eps)
    return (y * w.astype(F32)).astype(x.dtype)


def l2norm(x):
    return x * lax.rsqrt(jnp.sum(x * x, axis=-1, keepdims=True) + 1e-6)


def split_proj(z):
    cuts = [int(c) for c in np.cumsum(IN_SPLITS)[:-1]]
    return jnp.split(z, cuts, axis=-1)


def causal_conv_silu(x, buf, w):
    T = x.shape[1]
    xp = jnp.concatenate([buf.astype(x.dtype), x], axis=1)
    y = sum(xp[:, i:i + T] * w[i] for i in range(CONV_W))
    return jax.nn.silu(y), xp[:, -(CONV_W - 1):]


def gdn_chunked(q, k, v, g, beta, s0):
    bsz, T, H, DK = q.shape
    DV = v.shape[-1]
    C = math.gcd(T, GDN_CHUNK)
    n = T // C

    def chunks(a):
        a = jnp.moveaxis(a, 2, 1)
        a = a.reshape(bsz, H, n, C, *a.shape[3:])
        return jnp.moveaxis(a, 2, 0)

    incl = jnp.tril(jnp.ones((C, C), bool))
    strict = jnp.tril(jnp.ones((C, C), bool), -1)
    eye = jnp.eye(C, dtype=F32)

    def step(S, xs):
        qc, kc, vc, gc, bc = xs
        gcum = jnp.cumsum(gc, axis=-1)
        diff = gcum[..., :, None] - gcum[..., None, :]
        decay = jnp.where(incl, jnp.exp(jnp.where(incl, diff, 0.0)), 0.0)
        kb = kc * bc[..., None]
        L = jnp.where(strict, jnp.einsum('bhid,bhjd->bhij', kb, kc) * decay, 0.0)
        rhs = jnp.concatenate([vc * bc[..., None], kb * jnp.exp(gcum)[..., None]], axis=-1)
        sol = lax.linalg.triangular_solve(eye + L, rhs, left_side=True, lower=True, unit_diagonal=True)
        u, w = sol[..., :DV], sol[..., DV:]
        v_new = u - jnp.einsum('bhck,bhkv->bhcv', w, S)
        qk = jnp.einsum('bhik,bhjk->bhij', qc, kc) * decay
        o = (jnp.einsum('bhck,bhkv->bhcv', qc * jnp.exp(gcum)[..., None], S)
             + jnp.einsum('bhij,bhjv->bhiv', qk, v_new))
        g_last = gcum[..., -1:]
        S = (S * jnp.exp(g_last)[..., None]
             + jnp.einsum('bhck,bhcv->bhkv', kc * jnp.exp(g_last - gcum)[..., None], v_new))
        return S, o

    s_fin, o = lax.scan(step, s0, (chunks(q), chunks(k), chunks(v), chunks(g), chunks(beta)))
    o = jnp.moveaxis(o, 0, 2).reshape(bsz, H, T, DV)
    return jnp.moveaxis(o, 1, 2), s_fin


def gdn_mixer(z_qkv, z_gate, z_a, z_b, conv_buf, s0, conv_w, a_log, dt_bias, norm_w):
    bsz, T, _ = z_qkv.shape
    qkv, new_buf = causal_conv_silu(z_qkv, conv_buf, conv_w)
    qkv = qkv.astype(F32)
    q = qkv[..., :GDN_QK].reshape(bsz, T, N_GDN_HEADS, GDN_DK)
    k = qkv[..., GDN_QK:2 * GDN_QK].reshape(bsz, T, N_GDN_HEADS, GDN_DK)
    v = qkv[..., 2 * GDN_QK:].reshape(bsz, T, N_GDN_HEADS, GDN_DV)
    q = l2norm(q) * (GDN_DK ** -0.5)
    k = l2norm(k)
    g = -jnp.exp(a_log.astype(F32)) * jax.nn.softplus(z_a.astype(F32) + dt_bias.astype(F32))
    beta = jax.nn.sigmoid(z_b.astype(F32))
    o, s_new = gdn_chunked(q, k, v, g, beta, s0.astype(F32))
    gate = z_gate.astype(F32).reshape(bsz, T, N_GDN_HEADS, GDN_DV)
    o = rmsnorm(o, norm_w) * jax.nn.silu(gate)
    return o.reshape(bsz, T, GDN_VW).astype(z_qkv.dtype), new_buf, s_new


def t5_bucket(rel):
    n = jnp.maximum(rel, 0)
    max_exact = N_BUCKETS // 2
    large = max_exact + (jnp.log(jnp.maximum(n, 1).astype(F32) / max_exact)
                         / math.log(MAX_DISTANCE / max_exact) * (N_BUCKETS - max_exact)).astype(jnp.int32)
    large = jnp.minimum(large, N_BUCKETS - 1)
    return jnp.where(n < max_exact, n, large)


def diff_attend(q, k, v, q_pos, k_pos, lam, rel_bias):
    rel = q_pos[:, None] - k_pos[None, :]
    bias = jnp.transpose(rel_bias[t5_bucket(rel)].astype(F32), (2, 0, 1))[None]
    mask = (rel >= 0)[None, None]
    scale = DIFF_DH ** -0.5

    def probs(qa, ka):
        s = jnp.einsum('bqhd,bkhd->bhqk', qa, ka, preferred_element_type=F32) * scale + bias
        return jax.nn.softmax(jnp.where(mask, s, -jnp.inf), axis=-1)

    a = probs(q[..., :DIFF_DH], k[..., :DIFF_DH]) - lam * probs(q[..., DIFF_DH:], k[..., DIFF_DH:])
    return jnp.einsum('bhqk,bkhd->bqhd', a, v.astype(F32))


def diff_prompt(q, k, v, lam, rel_bias):
    bsz, seq = q.shape[:2]
    nqb = seq // Q_BLOCK
    qb = jnp.moveaxis(q.reshape(bsz, nqb, Q_BLOCK, N_DIFF_HEADS, 2 * DIFF_DH), 1, 0)
    k_pos = jnp.arange(seq, dtype=jnp.int32)

    def one_block(args):
        qi, i = args
        q_pos = i * Q_BLOCK + jnp.arange(Q_BLOCK, dtype=jnp.int32)
        return diff_attend(qi, k, v, q_pos, k_pos, lam, rel_bias)

    o = lax.map(one_block, (qb, jnp.arange(nqb, dtype=jnp.int32)))
    return jnp.moveaxis(o, 0, 1).reshape(bsz, seq, N_DIFF_HEADS, DIFF_DV)


def diff_sample(q, k, v, cache_k, cache_v, layer, page_table, lam, rel_bias):
    T = q.shape[1]
    past = page_table.shape[1] * PAGE_SIZE
    k_pos = jnp.arange(past + T, dtype=jnp.int32)
    q_pos = past + jnp.arange(T, dtype=jnp.int32)

    def one_seq(args):
        pages, qi, ki, vi = args
        k_all = jnp.concatenate([cache_k[layer, pages].reshape(past, N_DIFF_HEADS, 2 * DIFF_DH),
                                 ki.astype(cache_k.dtype)], axis=0)
        v_all = jnp.concatenate([cache_v[layer, pages].reshape(past, N_DIFF_HEADS, DIFF_DV),
                                 vi.astype(cache_v.dtype)], axis=0)
        return diff_attend(qi[None], k_all[None], v_all[None], q_pos, k_pos, lam, rel_bias)[0]

    return lax.map(one_seq, (page_table, q, k, v))


def moe_dispatch(h, e_idx, gate_w, w_gate, w_up, w_down):
    N, D = h.shape
    E = w_gate.shape[0]
    B = MOE_BLOCK
    M = N * TOP_K
    n_blocks = (M + E * (B - 1) + B - 1) // B
    R = n_blocks * B
    flat_e = e_idx.reshape(M)
    flat_tok = jnp.repeat(jnp.arange(N, dtype=jnp.int32), TOP_K)
    flat_w = gate_w.reshape(M)
    order = jnp.argsort(flat_e)
    sorted_e = flat_e[order]
    counts = jnp.zeros((E,), jnp.int32).at[flat_e].add(1)
    starts = jnp.cumsum(counts) - counts
    padded = ((counts + B - 1) // B) * B
    pad_ends = jnp.cumsum(padded)
    pad_starts = pad_ends - padded
    dest = pad_starts[sorted_e] + (jnp.arange(M, dtype=jnp.int32) - starts[sorted_e])
    row_tok = jnp.full((R,), N, jnp.int32).at[dest].set(flat_tok[order])
    row_w = jnp.zeros((R,), h.dtype).at[dest].set(flat_w[order])
    block_e = jnp.minimum(jnp.searchsorted(pad_ends, jnp.arange(n_blocks, dtype=jnp.int32) * B, side='right'),
                          E - 1).astype(jnp.int32)
    h_pad = jnp.concatenate([h, jnp.zeros((1, D), h.dtype)], axis=0)

    def run_block(args):
        tok, wt, e = args
        xb = h_pad[tok]
        y = (jax.nn.silu(xb @ w_gate[e]) * (xb @ w_up[e])) @ w_down[e]
        return y * wt[:, None]

    out = lax.map(run_block, (row_tok.reshape(n_blocks, B), row_w.reshape(n_blocks, B), block_e))
    return jax.ops.segment_sum(out.reshape(R, D), row_tok, num_segments=N + 1)[:N]


def hier_moe(h, router_group, router_expert, w_gate, w_up, w_down):
    bsz, T, D = h.shape
    hf = h.reshape(bsz * T, D)
    N = hf.shape[0]
    rows = jnp.arange(N)
    gl = jnp.einsum('nd,dg->ng', hf, router_group, preferred_element_type=F32)
    pg = jax.nn.softmax(gl, axis=-1)
    g_idx = jnp.argmax(gl, axis=-1).astype(jnp.int32)
    g_w = pg[rows, g_idx][:, None]
    el = jnp.einsum('nd,dge->nge', hf, router_expert, preferred_element_type=F32)[rows, g_idx]
    top_p, top_j = lax.top_k(jax.nn.softmax(el, axis=-1), TOP_K)
    w = g_w * top_p / jnp.sum(top_p, axis=-1, keepdims=True)
    e_idx = g_idx[:, None] * EXPERTS_PER_GROUP + top_j.astype(jnp.int32)
    y = moe_dispatch(hf, e_idx, w.astype(h.dtype), w_gate, w_up, w_down)
    return y.reshape(bsz, T, D)


def setup_inputs(seed: int = 0) -> dict:
    key = jax.random.key(seed)
    ks = jax.random.split(key, 32)
    n_pages = PAST_LEN // PAGE_SIZE
    n_used = DEC_BATCH * n_pages
    n_pool = n_used + n_used // 4

    def nrm(k, shape, scale=1.0):
        return jax.random.normal(k, shape, F32) * scale

    def gain(k, shape):
        return 1.0 + 0.02 * jax.random.normal(k, shape, F32)

    page_table = jax.random.permutation(ks[0], n_pool)[:n_used].reshape(DEC_BATCH, n_pages).astype(jnp.int32)
    dt = jnp.exp(jax.random.uniform(ks[1], (DEPTH, N_GDN_HEADS), F32, math.log(1e-3), math.log(1e-1)))
    return {
        'x_prompt': nrm(ks[2], (BATCH, SEQ, D_MODEL)),
        'x_sample': nrm(ks[3], (DEC_BATCH, DEC_SEQ, D_MODEL)),
        'cache_k': nrm(ks[4], (DEPTH, n_pool, PAGE_SIZE, N_DIFF_HEADS, 2 * DIFF_DH)),
        'cache_v': nrm(ks[5], (DEPTH, n_pool, PAGE_SIZE, N_DIFF_HEADS, DIFF_DV)),
        'state_conv': nrm(ks[6], (DEPTH, DEC_BATCH, CONV_W - 1, CONV_DIM)),
        'state_gdn': nrm(ks[7], (DEPTH, DEC_BATCH, N_GDN_HEADS, GDN_DK, GDN_DV), GDN_DK ** -0.5),
        'page_table': page_table,
        'norm_mix': gain(ks[8], (DEPTH, D_MODEL)),
        'w_in': nrm(ks[9], (DEPTH, D_MODEL, IN_COLS), D_MODEL ** -0.5),
        'conv_w': nrm(ks[10], (DEPTH, CONV_W, CONV_DIM), CONV_W ** -0.5),
        'gdn_a_log': jnp.log(jax.random.uniform(ks[11], (DEPTH, N_GDN_HEADS), F32, 1.0, 16.0)),
        'gdn_dt_bias': dt + jnp.log(-jnp.expm1(-dt)),
        'gdn_norm': gain(ks[12], (DEPTH, GDN_DV)),
        'lambda_q1': nrm(ks[13], (DEPTH, DIFF_DH), 0.1),
        'lambda_k1': nrm(ks[14], (DEPTH, DIFF_DH), 0.1),
        'lambda_q2': nrm(ks[15], (DEPTH, DIFF_DH), 0.1),
        'lambda_k2': nrm(ks[16], (DEPTH, DIFF_DH), 0.1),
        'diff_subln': gain(ks[17], (DEPTH, DIFF_DV)),
        'rel_bias': nrm(ks[18], (N_BUCKETS, N_DIFF_HEADS), 0.2),
        'w_out': nrm(ks[19], (DEPTH, MIX_WIDTH, D_MODEL), MIX_WIDTH ** -0.5),
        'norm_ffn': gain(ks[20], (DEPTH, D_MODEL)),
        'router_group': nrm(ks[21], (DEPTH, D_MODEL, N_GROUPS), D_MODEL ** -0.5),
        'router_expert': nrm(ks[22], (DEPTH, D_MODEL, N_GROUPS, EXPERTS_PER_GROUP), D_MODEL ** -0.5),
        'w_gate': nrm(ks[23], (DEPTH, N_EXPERTS, D_MODEL, EXPERT_FF), D_MODEL ** -0.5),
        'w_up': nrm(ks[24], (DEPTH, N_EXPERTS, D_MODEL, EXPERT_FF), D_MODEL ** -0.5),
        'w_down': nrm(ks[25], (DEPTH, N_EXPERTS, EXPERT_FF, D_MODEL), EXPERT_FF ** -0.5),
        'norm_final': gain(ks[26], (D_MODEL,)),
    }


def reference(x_prompt, x_sample, cache_k, cache_v, state_conv, state_gdn, page_table,
              norm_mix, w_in, conv_w, gdn_a_log, gdn_dt_bias, gdn_norm,
              lambda_q1, lambda_k1, lambda_q2, lambda_k2, diff_subln, rel_bias, w_out,
              norm_ffn, router_group, router_expert, w_gate, w_up, w_down, norm_final):
    bsz, seq, _ = x_prompt.shape
    dbsz, dseq, _ = x_sample.shape
    xp, xs = x_prompt, x_sample
    kp_l, vp_l, ks_l, vs_l, cp_l, cs_l, sp_l, ss_l = [], [], [], [], [], [], [], []
    for l in range(DEPTH):
        lam_init = 0.8 - 0.6 * math.exp(-0.3 * l)
        lam = (jnp.exp(jnp.sum(lambda_q1[l].astype(F32) * lambda_k1[l].astype(F32)))
               - jnp.exp(jnp.sum(lambda_q2[l].astype(F32) * lambda_k2[l].astype(F32))) + lam_init)
        gdn_params = (conv_w[l], gdn_a_log[l], gdn_dt_bias[l], gdn_norm[l])

        zp = split_proj(rmsnorm(xp, norm_mix[l]) @ w_in[l])
        gdn_p, conv_p, s_p = gdn_mixer(zp[0], zp[1], zp[2], zp[3],
                                       jnp.zeros((bsz, CONV_W - 1, CONV_DIM), xp.dtype),
                                       jnp.zeros((bsz, N_GDN_HEADS, GDN_DK, GDN_DV), F32), *gdn_params)
        q_p = zp[4].reshape(bsz, seq, N_DIFF_HEADS, 2 * DIFF_DH)
        k_p = zp[5].reshape(bsz, seq, N_DIFF_HEADS, 2 * DIFF_DH)
        v_p = zp[6].reshape(bsz, seq, N_DIFF_HEADS, DIFF_DV)
        diff_p = rmsnorm(diff_prompt(q_p, k_p, v_p, lam, rel_bias), diff_subln[l], SUBLN_EPS) * (1.0 - lam_init)
        mix_p = jnp.concatenate([gdn_p, diff_p.reshape(bsz, seq, DIFF_VW).astype(xp.dtype)], axis=-1)
        xp = xp + mix_p @ w_out[l]
        xp = xp + hier_moe(rmsnorm(xp, norm_ffn[l]), router_group[l], router_expert[l],
                           w_gate[l], w_up[l], w_down[l])

        zs = split_proj(rmsnorm(xs, norm_mix[l]) @ w_in[l])
        gdn_s, conv_s, s_s = gdn_mixer(zs[0], zs[1], zs[2], zs[3], state_conv[l], state_gdn[l], *gdn_params)
        q_s = zs[4].reshape(dbsz, dseq, N_DIFF_HEADS, 2 * DIFF_DH)
        k_s = zs[5].reshape(dbsz, dseq, N_DIFF_HEADS, 2 * DIFF_DH)
        v_s = zs[6].reshape(dbsz, dseq, N_DIFF_HEADS, DIFF_DV)
        diff_s = diff_sample(q_s, k_s, v_s, cache_k, cache_v, l, page_table, lam, rel_bias)
        diff_s = rmsnorm(diff_s, diff_subln[l], SUBLN_EPS) * (1.0 - lam_init)
        mix_s = jnp.concatenate([gdn_s, diff_s.reshape(dbsz, dseq, DIFF_VW).astype(xs.dtype)], axis=-1)
        xs = xs + mix_s @ w_out[l]
        xs = xs + hier_moe(rmsnorm(xs, norm_ffn[l]), router_group[l], router_expert[l],
                           w_gate[l], w_up[l], w_down[l])

        kp_l.append(k_p.astype(cache_k.dtype))
        vp_l.append(v_p.astype(cache_v.dtype))
        ks_l.append(k_s.astype(cache_k.dtype))
        vs_l.append(v_s.astype(cache_v.dtype))
        cp_l.append(conv_p.astype(state_conv.dtype))
        cs_l.append(conv_s.astype(state_conv.dtype))
        sp_l.append(s_p.astype(state_gdn.dtype))
        ss_l.append(s_s.astype(state_gdn.dtype))

    y_prompt = rmsnorm(xp, norm_final)
    y_sample = rmsnorm(xs, norm_final)
    new_k_prompt = jnp.stack(kp_l)
    new_v_prompt = jnp.stack(vp_l)
    new_k_sample = jnp.stack(ks_l)
    new_v_sample = jnp.stack(vs_l)
    new_conv_prompt = jnp.stack(cp_l)
    new_conv_sample = jnp.stack(cs_l)
    new_gdn_prompt = jnp.stack(sp_l)
    new_gdn_sample = jnp.stack(ss_l)
    return (y_prompt, y_sample, new_k_prompt, new_v_prompt, new_k_sample, new_v_sample,
            new_conv_prompt, new_conv_sample, new_gdn_prompt, new_gdn_sample)
```

```python
import functools
import math

import jax
import jax.numpy as jnp
from jax import lax
from jax.experimental import pallas as pl
from jax.experimental.pallas import tpu as pltpu

F32 = jnp.float32
BF16 = jnp.bfloat16

D_MODEL = 1024
N_HEADS = 8
HEAD_DIM = 64
MAP_DIM = 32
MIX_HALF = N_HEADS * HEAD_DIM
CONV_W = 4
CONV_DIM = 3 * MIX_HALF
GDN_CHUNK = 64
HEADS_PER_GROUP = 4
GROUP_W = HEADS_PER_GROUP * HEAD_DIM
N_GROUPS_GDN = N_HEADS // HEADS_PER_GROUP
PAGE = 128
N_BUCKETS = 32
MAX_DISTANCE = 128
N_ROUTE_GROUPS = 4
EXPERTS_PER_GROUP = 8
N_EXPERTS = N_ROUTE_GROUPS * EXPERTS_PER_GROUP
EXPERT_FF = D_MODEL // 2
MOE_BLOCK = 128
NORM_EPS = 1e-6
SUBLN_EPS = 1e-5
LAMBDA_INIT = 0.8 - 0.6 * math.exp(-0.3 * 0)

LANES = 128
SUBLANES = 8
VMEM_LIMIT = 48 * 1024 * 1024

_C_QKV = (0, CONV_DIM)
_C_GATE = (_C_QKV[1], _C_QKV[1] + MIX_HALF)
_C_AB = (_C_GATE[1], _C_GATE[1] + LANES)
_C_DQ = (_C_AB[1], _C_AB[1] + MIX_HALF)
_C_DK = (_C_DQ[1], _C_DQ[1] + MIX_HALF)
_C_DV = (_C_DK[1], _C_DK[1] + MIX_HALF)
IN_COLS_PACKED = _C_DV[1]


def _params(sem, vmem=VMEM_LIMIT):
    return pltpu.CompilerParams(dimension_semantics=sem, vmem_limit_bytes=vmem)


def _dot(a, b):
    return jnp.dot(a, b, preferred_element_type=F32)


def _dot_nt(a, b):
    return lax.dot_general(a, b, (((1,), (1,)), ((), ())), preferred_element_type=F32)


def _dot_tn(a, b):
    return lax.dot_general(a, b, (((0,), (0,)), ((), ())), preferred_element_type=F32)


def _split2(x):
    x1 = x.astype(BF16)
    x2 = (x - x1.astype(F32)).astype(BF16)
    return x1, x2


def _split3(x):
    x1 = x.astype(BF16)
    r = x - x1.astype(F32)
    x2 = r.astype(BF16)
    x3 = (r - x2.astype(F32)).astype(BF16)
    return x1, x2, x3


def _dot_x01(x, m01):
    x1, x2, x3 = _split3(x)
    return _dot(x1, m01) + (_dot(x2, m01) + _dot(x3, m01))


def _dot_01x(m01, x):
    x1, x2, x3 = _split3(x)
    return _dot(m01, x1) + (_dot(m01, x2) + _dot(m01, x3))


def _iota(shape, dim):
    return lax.broadcasted_iota(jnp.int32, shape, dim)


def _inproj_body(x_ref, nw_ref, w_ref, zqkv_ref, zgate_ref, zab_ref, kf_ref, vf_ref,
                 qh_ref, kh_ref, vh_ref):
    x = x_ref[...]
    h = x * lax.rsqrt(jnp.mean(x * x, axis=-1, keepdims=True) + NORM_EPS) * nw_ref[...]
    hb = h.astype(BF16)

    def proj(cols):
        return _dot(hb, w_ref[:, cols[0]:cols[1]])

    zqkv_ref[...] = proj(_C_QKV)
    zgate_ref[...] = proj(_C_GATE)
    zab_ref[...] = proj(_C_AB)
    zq = proj(_C_DQ)
    zk = proj(_C_DK)
    zv = proj(_C_DV)
    kf_ref[...] = zk
    vf_ref[...] = zv
    qscale = MAP_DIM ** -0.5
    for h_i in range(N_HEADS):
        sl = slice(HEAD_DIM * h_i, HEAD_DIM * (h_i + 1))
        qh_ref[h_i] = (zq[:, sl] * qscale).astype(BF16)
        kh_ref[h_i] = zk[:, sl].astype(BF16)
        vh_ref[h_i] = zv[:, sl].astype(BF16)


def _inproj(x2d, norm_w, w_packed, tm):
    n = x2d.shape[0]
    assert n % tm == 0
    row = lambda i: (i, 0)
    fixed = lambda i: (0, 0)
    hm = lambda i: (0, i, 0)
    out_shape = (
        jax.ShapeDtypeStruct((n, CONV_DIM), F32),
        jax.ShapeDtypeStruct((n, MIX_HALF), F32),
        jax.ShapeDtypeStruct((n, LANES), F32),
        jax.ShapeDtypeStruct((n, MIX_HALF), F32),
        jax.ShapeDtypeStruct((n, MIX_HALF), F32),
        jax.ShapeDtypeStruct((N_HEADS, n, HEAD_DIM), BF16),
        jax.ShapeDtypeStruct((N_HEADS, n, HEAD_DIM), BF16),
        jax.ShapeDtypeStruct((N_HEADS, n, HEAD_DIM), BF16),
    )
    out_specs = (
        pl.BlockSpec((tm, CONV_DIM), row),
        pl.BlockSpec((tm, MIX_HALF), row),
        pl.BlockSpec((tm, LANES), row),
        pl.BlockSpec((tm, MIX_HALF), row),
        pl.BlockSpec((tm, MIX_HALF), row),
        pl.BlockSpec((N_HEADS, tm, HEAD_DIM), hm),
        pl.BlockSpec((N_HEADS, tm, HEAD_DIM), hm),
        pl.BlockSpec((N_HEADS, tm, HEAD_DIM), hm),
    )
    return pl.pallas_call(
        _inproj_body,
        grid=(n // tm,),
        in_specs=[pl.BlockSpec((tm, D_MODEL), row),
                  pl.BlockSpec((1, D_MODEL), fixed),
                  pl.BlockSpec((D_MODEL, IN_COLS_PACKED), fixed)],
        out_specs=out_specs,
        out_shape=out_shape,
        compiler_params=_params(("parallel",)),
        name="inproj",
    )(x2d, norm_w, w_packed)


def _gdn_body(zq_ref, zk_ref, zv_ref, zg_ref, zab_ref, cwq_ref, cwk_ref, cwv_ref,
              alog_ref, dtb_ref, nw_ref, ciq_ref, cik_ref, civ_ref, s0_ref,
              o_ref, sfin_ref, s_scr, prev_scr, *, t_valid, zero_init):
    C = GDN_CHUNK
    W = GROUP_W
    g = pl.program_id(1)
    c = pl.program_id(2)
    n_chunks = pl.num_programs(2)

    rw = _iota((W, W), 0)
    cw = _iota((W, W), 1)
    blk = (rw // HEAD_DIM) == (cw // HEAD_DIM)
    ones_blk = jnp.where(blk, 1.0, 0.0).astype(BF16)
    ri = _iota((C, W), 0)
    cj = _iota((C, W), 1) % C
    incl = ri >= cj
    strict = ri > cj
    eye_all = ri == cj

    @pl.when(c == 0)
    def _init():
        prev_scr[0] = ciq_ref[...]
        prev_scr[1] = cik_ref[...]
        prev_scr[2] = civ_ref[...]
        if zero_init:
            s_scr[...] = jnp.zeros((W, W), F32)
        else:
            s0 = s0_ref[...].reshape(W, HEAD_DIM)
            et = (_iota((HEAD_DIM, W), 0) == (_iota((HEAD_DIM, W), 1) % HEAD_DIM))
            s_scr[...] = jnp.where(blk, _dot_x01(s0, jnp.where(et, 1.0, 0.0).astype(BF16)), 0.0)

    def conv(z_ref, cw_ref, idx):
        x = z_ref[...]
        xc = jnp.concatenate([prev_scr[idx], x], axis=0)
        w = cw_ref[...]
        y = (xc * w[3:4] + pltpu.roll(xc, 1, 0) * w[2:3]
             + pltpu.roll(xc, 2, 0) * w[1:2] + pltpu.roll(xc, 3, 0) * w[0:1])
        prev_scr[idx] = xc[C:C + SUBLANES]
        return jax.nn.silu(y[SUBLANES:SUBLANES + C])

    def blocksum(x):
        x1, x2 = _split2(x)
        return _dot(x1, ones_blk) + _dot(x2, ones_blk)

    def bd(x16):
        return jnp.where(blk, jnp.concatenate([x16] * HEADS_PER_GROUP, axis=0), jnp.zeros((), BF16))

    def bd_pair(x):
        x1, x2 = _split2(x)
        return bd(x1), bd(x2)

    def mm_hi(a, b_pair):
        a1, a2 = _split2(a)
        return _dot(a1, b_pair[0]) + (_dot(a1, b_pair[1]) + _dot(a2, b_pair[0]))

    q = conv(zq_ref, cwq_ref, 0)
    k = conv(zk_ref, cwk_ref, 1)
    v = conv(zv_ref, cwv_ref, 2)
    q = q * lax.rsqrt(blocksum(q * q) + 1e-6) * (HEAD_DIM ** -0.5)
    k = k * lax.rsqrt(blocksum(k * k) + 1e-6)

    zab = zab_ref[...]
    valid = (_iota((C, LANES), 0) + c * C) < t_valid
    gfull = -jnp.exp(alog_ref[...]) * jax.nn.softplus(zab + dtb_ref[...])
    gfull = jnp.where(valid, gfull, 0.0)
    bfull = jnp.where(valid, jax.nn.sigmoid(zab), 0.0)
    tri = jnp.where(_iota((C, C), 0) >= _iota((C, C), 1), 1.0, 0.0).astype(BF16)
    gcum = _dot_01x(tri, gfull)
    lane_r = _iota((LANES, W), 0)
    head_c = _iota((LANES, W), 1) // HEAD_DIM + g * HEADS_PER_GROUP
    e_g = jnp.where(lane_r == head_c, 1.0, 0.0).astype(BF16)
    e_b = jnp.where(lane_r == head_c + N_HEADS, 1.0, 0.0).astype(BF16)
    gexp = _dot_x01(gcum, e_g)
    bexp = _dot_x01(bfull, e_b)
    grow = jnp.sum(jnp.where(eye_all, gexp, 0.0), axis=0, keepdims=True)
    decay = jnp.where(incl, jnp.exp(jnp.where(incl, gexp - grow, 0.0)), 0.0)

    kb = k * bexp
    bdk = bd(k.astype(BF16))
    kkqk = _dot_nt(jnp.concatenate([kb, q], axis=0).astype(BF16), bdk)
    lmat = jnp.where(strict, kkqk[:C] * decay, 0.0)
    qkd = kkqk[C:] * decay

    tinv = jnp.where(eye_all, 1.0, 0.0) - lmat
    p = lmat
    bp = bd_pair(p)
    for _ in range(int(math.log2(C)) - 1):
        p = mm_hi(p, bp)
        bp = bd_pair(p)
        tinv = tinv + mm_hi(tinv, bp)

    eg = jnp.exp(gexp)
    u = mm_hi(tinv, bd_pair(v * bexp))
    wm = mm_hi(tinv, bd_pair(kb * eg))
    s = s_scr[...]
    s16 = s.astype(BF16)
    v_new = u - _dot(wm.astype(BF16), s16)
    o = _dot((q * eg).astype(BF16), s16) + _dot(qkd.astype(BF16), bd(v_new.astype(BF16)))
    glast = gexp[C - 1:C, :]
    kd = k * jnp.exp(glast - gexp)
    upd = _dot_tn(kd.astype(BF16), v_new.astype(BF16))
    s_new = s * jnp.exp(glast) + jnp.where(blk, upd, 0.0)
    s_scr[...] = s_new

    ms = blocksum(o * o) * (1.0 / HEAD_DIM)
    on = o * lax.rsqrt(ms + NORM_EPS) * nw_ref[...]
    o_ref[...] = (on * jax.nn.silu(zg_ref[...])).astype(o_ref.dtype)

    @pl.when(c == n_chunks - 1)
    def _fin():
        ett = (_iota((W, HEAD_DIM), 0) % HEAD_DIM) == _iota((W, HEAD_DIM), 1)
        sout = _dot_x01(s_new, jnp.where(ett, 1.0, 0.0).astype(BF16))
        sfin_ref[...] = sout.reshape(HEADS_PER_GROUP, HEAD_DIM, HEAD_DIM)


def _gdn(zqkv, zgate, zab, conv_w, alog_row, dtb_row, norm_row, conv_init, s0, t_valid):
    bsz, t, _ = zqkv.shape
    C = GDN_CHUNK
    assert t % C == 0
    n_chunks = t // C
    zero_init = s0 is None
    if zero_init:
        s0 = jnp.zeros((1, N_HEADS, HEAD_DIM, HEAD_DIM), F32)
        s0_map = lambda b, g, c: (0, g, 0, 0)
    else:
        s0_map = lambda b, g, c: (b, g, 0, 0)
    ng = N_GROUPS_GDN

    def zspec(off):
        return pl.BlockSpec((None, C, GROUP_W), lambda b, g, c, off=off: (b, c, off + g))

    def cwspec(off):
        return pl.BlockSpec((CONV_W, GROUP_W), lambda b, g, c, off=off: (0, off + g))

    def cispec(off):
        return pl.BlockSpec((None, SUBLANES, GROUP_W), lambda b, g, c, off=off: (b, 0, off + g))

    row128 = pl.BlockSpec((1, LANES), lambda b, g, c: (0, 0))
    in_specs = [
        zspec(0), zspec(ng), zspec(2 * ng),
        pl.BlockSpec((None, C, GROUP_W), lambda b, g, c: (b, c, g)),
        pl.BlockSpec((None, C, LANES), lambda b, g, c: (b, c, 0)),
        cwspec(0), cwspec(ng), cwspec(2 * ng),
        row128, row128,
        pl.BlockSpec((1, GROUP_W), lambda b, g, c: (0, 0)),
        cispec(0), cispec(ng), cispec(2 * ng),
        pl.BlockSpec((None, HEADS_PER_GROUP, HEAD_DIM, HEAD_DIM), s0_map),
    ]
    out_specs = (
        pl.BlockSpec((None, C, GROUP_W), lambda b, g, c: (b, c, g)),
        pl.BlockSpec((None, HEADS_PER_GROUP, HEAD_DIM, HEAD_DIM), lambda b, g, c: (b, g, 0, 0)),
    )
    out_shape = (
        jax.ShapeDtypeStruct((bsz, t, MIX_HALF), BF16),
        jax.ShapeDtypeStruct((bsz, N_HEADS, HEAD_DIM, HEAD_DIM), F32),
    )
    body = functools.partial(_gdn_body, t_valid=t_valid, zero_init=zero_init)
    return pl.pallas_call(
        body,
        grid=(bsz, ng, n_chunks),
        in_specs=in_specs,
        out_specs=out_specs,
        out_shape=out_shape,
        scratch_shapes=[pltpu.VMEM((GROUP_W, GROUP_W), F32),
                        pltpu.VMEM((3, SUBLANES, GROUP_W), F32)],
        compiler_params=_params(("parallel", "parallel", "arbitrary")),
        name="gdn",
    )(zqkv, zqkv, zqkv, zgate, zab, conv_w, conv_w, conv_w, alog_row, dtb_row, norm_row,
      conv_init, conv_init, conv_init, s0)


def _t5_bucket(rel):
    n = jnp.maximum(rel, 0)
    max_exact = N_BUCKETS // 2
    large = max_exact + (jnp.log(jnp.maximum(n, 1).astype(F32) / max_exact)
                         / math.log(MAX_DISTANCE / max_exact) * (N_BUCKETS - max_exact)).astype(jnp.int32)
    large = jnp.minimum(large, N_BUCKETS - 1)
    return jnp.where(n < max_exact, n, large)


def _rel_bias_shifted(rel_bias, rel):
    far = rel_bias[N_BUCKETS - 1].astype(F32)
    return rel_bias[_t5_bucket(rel)].astype(F32) - far


def _lambda(lam_ref):
    lam = lam_ref[...]
    a = jnp.exp(jnp.sum(lam[0:1] * lam[1:2], axis=-1, keepdims=True))
    b = jnp.exp(jnp.sum(lam[2:3] * lam[3:4], axis=-1, keepdims=True))
    return a - b + LAMBDA_INIT


def _subln(o, w_row):
    ms = jnp.mean(o * o, axis=-1, keepdims=True)
    return o * lax.rsqrt(ms + SUBLN_EPS) * w_row * (1.0 - LAMBDA_INIT)


def _softmax_update(s, m_prev, l_prev, acc_prev, v16):
    m_new = jnp.maximum(m_prev, jnp.max(s, axis=-1, keepdims=True))
    alpha = jnp.exp(m_prev - m_new)
    p = jnp.exp(s - m_new)
    l_new = alpha * l_prev + jnp.sum(p, axis=-1, keepdims=True)
    acc_new = alpha * acc_prev + _dot(p.astype(BF16), v16)
    return m_new, l_new, acc_new


def _pattn_body(q_ref, k_ref, v_ref, bias_ref, lam_ref, sw_ref, o_ref, o_scr, *, tq):
    qi = pl.program_id(1)
    h = pl.program_id(2)
    tk = tq
    q = q_ref[...]
    lane = _iota((tq, HEAD_DIM), 1)
    zero = jnp.zeros((), BF16)
    qs = jnp.concatenate([jnp.where(lane < MAP_DIM, q, zero), jnp.where(lane >= MAP_DIM, q, zero)], axis=0)

    def tile(j, carry, bias):
        start = pl.multiple_of(j * tk, tk)
        kt = k_ref[pl.ds(start, tk), :]
        vt = v_ref[pl.ds(start, tk), :]
        s = _dot_nt(qs, kt)
        if bias is not None:
            s = s + jnp.concatenate([bias, bias], axis=0)
        return _softmax_update(s, *carry, vt)

    m0 = jnp.full((2 * tq, 1), -jnp.inf, F32)
    l0 = jnp.zeros((2 * tq, 1), F32)
    a0 = jnp.zeros((2 * tq, HEAD_DIM), F32)
    carry = tile(qi, (m0, l0, a0), bias_ref[0])
    prev = jnp.maximum(qi - 1, 0)
    prev_bias = bias_ref[1] + jnp.where(qi > 0, 0.0, -jnp.inf)
    carry = tile(prev, carry, prev_bias)
    carry = lax.fori_loop(0, prev, lambda j, cr: tile(j, cr, None), carry)
    m, l, acc = carry
    o = acc / l
    lam = _lambda(lam_ref)
    od = o[:tq] - lam * o[tq:]
    o_scr[h] = _subln(od, sw_ref[...])

    @pl.when(h == N_HEADS - 1)
    def _store():
        o_ref[...] = jnp.concatenate([o_scr[i] for i in range(N_HEADS)], axis=-1).astype(o_ref.dtype)


def _prompt_attention(qh, kh, vh, bias_tab, lam_vecs, subln_row, tq):
    _, bsz, t, _ = qh.shape
    assert t % tq == 0
    nq = t // tq
    body = functools.partial(_pattn_body, tq=tq)
    return pl.pallas_call(
        body,
        grid=(bsz, nq, N_HEADS),
        in_specs=[
            pl.BlockSpec((None, None, tq, HEAD_DIM), lambda b, i, h: (h, b, i, 0)),
            pl.BlockSpec((None, None, t, HEAD_DIM), lambda b, i, h: (h, b, 0, 0)),
            pl.BlockSpec((None, None, t, HEAD_DIM), lambda b, i, h: (h, b, 0, 0)),
            pl.BlockSpec((None, 2, tq, tq), lambda b, i, h: (h, 0, 0, 0)),
            pl.BlockSpec((4, MAP_DIM), lambda b, i, h: (0, 0)),
            pl.BlockSpec((1, HEAD_DIM), lambda b, i, h: (0, 0)),
        ],
        out_specs=pl.BlockSpec((None, tq, MIX_HALF), lambda b, i, h: (b, i, 0)),
        out_shape=jax.ShapeDtypeStruct((bsz, t, MIX_HALF), BF16),
        scratch_shapes=[pltpu.VMEM((N_HEADS, tq, HEAD_DIM), F32)],
        compiler_params=_params(("parallel", "parallel", "arbitrary")),
        name="prompt_attn",
    )(qh, kh, vh, bias_tab, lam_vecs, subln_row)


def _prompt_bias_tables(rel_bias, tq):
    i = jnp.arange(tq, dtype=jnp.int32)[:, None]
    j = jnp.arange(tq, dtype=jnp.int32)[None, :]
    rel0 = i - j
    rel1 = tq + i - j
    d0 = jnp.where((rel0 >= 0)[..., None], _rel_bias_shifted(rel_bias, rel0), -jnp.inf)
    d1 = _rel_bias_shifted(rel_bias, rel1)
    return jnp.transpose(jnp.stack([d0, d1]), (3, 0, 1, 2))


def _sattn_body(pt_ref, q_ref, kn_ref, vn_ref, madd_ref, bnear_ref, bnew_ref, lam_ref, sw_ref,
                ck_hbm, cv_hbm, o_ref, kbuf, vbuf, sem, m_scr, l_scr, acc_scr, *, ch, n_chunks, n_seq):
    b = pl.program_id(0)
    c = pl.program_id(1)
    step = b * n_chunks + c
    slot = step % 2
    rows = PAGE * N_HEADS
    nr = 2 * N_HEADS * (q_ref.shape[0] // N_HEADS)

    def copies(bb, cc, sl):
        out = []
        for p in range(ch):
            pg = pt_ref[bb, cc * ch + p]
            out.append(pltpu.make_async_copy(ck_hbm.at[0, pg], kbuf.at[sl, p], sem.at[0, sl]))
            out.append(pltpu.make_async_copy(cv_hbm.at[0, pg], vbuf.at[sl, p], sem.at[1, sl]))
        return out

    @pl.when(step == 0)
    def _prime():
        for cp in copies(0, 0, 0):
            cp.start()

    @pl.when(step + 1 < n_seq * n_chunks)
    def _prefetch():
        nxt = step + 1
        for cp in copies(nxt // n_chunks, nxt % n_chunks, 1 - slot):
            cp.start()

    for cp in copies(b, c, slot):
        cp.wait()

    @pl.when(c == 0)
    def _init():
        m_scr[...] = jnp.full(m_scr.shape, -jnp.inf, F32)
        l_scr[...] = jnp.zeros(l_scr.shape, F32)
        acc_scr[...] = jnp.zeros(acc_scr.shape, F32)

    q = q_ref[...]
    lane = _iota(q.shape, 1)
    zero = jnp.zeros((), BF16)
    qs = jnp.concatenate([jnp.where(lane < MAP_DIM, q, zero), jnp.where(lane >= MAP_DIM, q, zero)], axis=0)

    is_last = c == n_chunks - 1
    parts = []
    for p in range(ch):
        kf = kbuf[slot, p].reshape(rows, HEAD_DIM).astype(BF16)
        sp = _dot_nt(qs, kf) + madd_ref[...]
        if p == ch - 1:
            sp = sp + jnp.where(is_last, bnear_ref[...], 0.0)
        parts.append(sp)
    s = jnp.concatenate(parts, axis=-1)

    m_prev = m_scr[:, 0:1]
    l_prev = l_scr[:, 0:1]
    m_new = jnp.maximum(m_prev, jnp.max(s, axis=-1, keepdims=True))
    alpha = jnp.exp(m_prev - m_new)
    pr = jnp.exp(s - m_new)
    l_new = alpha * l_prev + jnp.sum(pr, axis=-1, keepdims=True)
    p16 = pr.astype(BF16)
    acc = alpha * acc_scr[...]
    for p in range(ch):
        vf = vbuf[slot, p].reshape(rows, HEAD_DIM).astype(BF16)
        acc = acc + _dot(p16[:, p * rows:(p + 1) * rows], vf)
    m_scr[...] = jnp.broadcast_to(m_new, m_scr.shape)
    l_scr[...] = jnp.broadcast_to(l_new, l_scr.shape)
    acc_scr[...] = acc

    @pl.when(is_last)
    def _fin():
        sn = _dot_nt(qs, kn_ref[...].astype(BF16)) + bnew_ref[...]
        m2, l2, acc2 = _softmax_update(sn, m_new, l_new, acc, vn_ref[...].astype(BF16))
        o = acc2 / l2
        half = nr // 2
        od = o[:half] - _lambda(lam_ref) * o[half:]
        o_ref[...] = _subln(od, sw_ref[...])


def _sample_attention(page_table, q_rows, k_new, v_new, cache_k, cache_v, madd, bnear, bnew,
                      lam_vecs, subln_row, ch):
    n_seq, n_pages = page_table.shape
    assert n_pages % ch == 0
    n_chunks = n_pages // ch
    nq_rows = q_rows.shape[1]
    nr = 2 * nq_rows
    rows = PAGE * N_HEADS
    body = functools.partial(_sattn_body, ch=ch, n_chunks=n_chunks, n_seq=n_seq)
    grid_spec = pltpu.PrefetchScalarGridSpec(
        num_scalar_prefetch=1,
        grid=(n_seq, n_chunks),
        in_specs=[
            pl.BlockSpec((None, nq_rows, HEAD_DIM), lambda b, c, pt: (b, 0, 0)),
            pl.BlockSpec((None, k_new.shape[1], HEAD_DIM), lambda b, c, pt: (b, 0, 0)),
            pl.BlockSpec((None, v_new.shape[1], HEAD_DIM), lambda b, c, pt: (b, 0, 0)),
            pl.BlockSpec((nr, rows), lambda b, c, pt: (0, 0)),
            pl.BlockSpec((nr, rows), lambda b, c, pt: (0, 0)),
            pl.BlockSpec((nr, k_new.shape[1]), lambda b, c, pt: (0, 0)),
            pl.BlockSpec((4, MAP_DIM), lambda b, c, pt: (0, 0)),
            pl.BlockSpec((1, HEAD_DIM), lambda b, c, pt: (0, 0)),
            pl.BlockSpec(memory_space=pl.ANY),
            pl.BlockSpec(memory_space=pl.ANY),
        ],
        out_specs=pl.BlockSpec((None, nq_rows, HEAD_DIM), lambda b, c, pt: (b, 0, 0)),
        scratch_shapes=[
            pltpu.VMEM((2, ch, PAGE, N_HEADS, HEAD_DIM), F32),
            pltpu.VMEM((2, ch, PAGE, N_HEADS, HEAD_DIM), F32),
            pltpu.SemaphoreType.DMA((2, 2)),
            pltpu.VMEM((nr, LANES), F32),
            pltpu.VMEM((nr, LANES), F32),
            pltpu.VMEM((nr, HEAD_DIM), F32),
        ],
    )
    return pl.pallas_call(
        body,
        grid_spec=grid_spec,
        out_shape=jax.ShapeDtypeStruct((n_seq, nq_rows, HEAD_DIM), F32),
        compiler_params=_params(("arbitrary", "arbitrary")),
        name="sample_attn",
    )(page_table, q_rows, k_new, v_new, madd, bnear, bnew, lam_vecs, subln_row, cache_k, cache_v)


def _sample_bias_tables(rel_bias, t_new, t_pad):
    H = N_HEADS
    nr = 2 * H * t_new
    row = jnp.arange(nr, dtype=jnp.int32)
    row_h = (row % (H * t_new)) // t_new
    row_t = row % t_new
    col = jnp.arange(PAGE * H, dtype=jnp.int32)
    madd = jnp.where(row_h[:, None] == (col % H)[None, :], 0.0, -jnp.inf).astype(F32)
    rel_near = PAGE + row_t[:, None] - (col // H)[None, :]
    near_all = _rel_bias_shifted(rel_bias, rel_near)
    bnear = jnp.take_along_axis(near_all, jnp.broadcast_to(row_h[:, None, None], near_all.shape[:2] + (1,)), axis=2)[..., 0]
    coln = jnp.arange(t_pad * H, dtype=jnp.int32)
    tn = coln // H
    reln = row_t[:, None] - tn[None, :]
    new_all = _rel_bias_shifted(rel_bias, reln)
    bnew = jnp.take_along_axis(new_all, jnp.broadcast_to(row_h[:, None, None], new_all.shape[:2] + (1,)), axis=2)[..., 0]
    ok = (reln >= 0) & (tn[None, :] < t_new) & (row_h[:, None] == (coln % H)[None, :])
    bnew = jnp.where(ok, bnew, -jnp.inf)
    return madd, bnear.astype(F32), bnew.astype(F32)


def _outproj_body(x_ref, gdn_ref, diff_ref, wo_ref, nw_ref, wr_ref, x2_ref, hp_ref, route_ref):
    x2 = (x_ref[...] + _dot(gdn_ref[...], wo_ref[:MIX_HALF, :])
          + _dot(diff_ref[...], wo_ref[MIX_HALF:, :]))
    x2_ref[...] = x2
    h = x2 * lax.rsqrt(jnp.mean(x2 * x2, axis=-1, keepdims=True) + NORM_EPS) * nw_ref[...]
    h1, h2 = _split2(h)
    logits = _dot(h1, wr_ref[0]) + (_dot(h1, wr_ref[1]) + _dot(h2, wr_ref[0]))
    tm = logits.shape[0]
    lane = _iota((tm, LANES), 1)
    neg = -jnp.inf
    big = jnp.int32(10_000)
    gl = jnp.where(lane < N_ROUTE_GROUPS, logits, neg)
    gm = jnp.max(gl, axis=-1, keepdims=True)
    g_idx = jnp.min(jnp.where(gl == gm, lane, big), axis=-1, keepdims=True)
    g_w = 1.0 / jnp.sum(jnp.exp(gl - gm), axis=-1, keepdims=True)
    lo = N_ROUTE_GROUPS + EXPERTS_PER_GROUP * g_idx
    el = jnp.where((lane >= lo) & (lane < lo + EXPERTS_PER_GROUP), logits, neg)
    em = jnp.max(el, axis=-1, keepdims=True)
    i1 = jnp.min(jnp.where(el == em, lane, big), axis=-1, keepdims=True)
    el2 = jnp.where(lane == i1, neg, el)
    em2 = jnp.max(el2, axis=-1, keepdims=True)
    i2 = jnp.min(jnp.where(el2 == em2, lane, big), axis=-1, keepdims=True)
    e2 = jnp.exp(em2 - em)
    w1 = g_w / (1.0 + e2)
    w2 = g_w * e2 / (1.0 + e2)
    route = jnp.where(lane == 0, (i1 - N_ROUTE_GROUPS).astype(F32),
                      jnp.where(lane == 1, (i2 - N_ROUTE_GROUPS).astype(F32),
                                jnp.where(lane == 2, w1, jnp.where(lane == 3, w2, 0.0))))
    route_ref[...] = route
    hb = h.astype(BF16).astype(F32)
    lo_bits = pltpu.bitcast(hb[:, :D_MODEL // 2], jnp.uint32) >> 16
    hi_bits = pltpu.bitcast(hb[:, D_MODEL // 2:], jnp.uint32) & jnp.uint32(0xFFFF0000)
    hp_ref[...] = lo_bits | hi_bits


def _outproj(x2d, gdn2d, diff2d, w_out16, norm_w, wr_pair, tm):
    n = x2d.shape[0]
    assert n % tm == 0
    row = lambda i: (i, 0)
    fixed = lambda i: (0, 0)
    return pl.pallas_call(
        _outproj_body,
        grid=(n // tm,),
        in_specs=[
            pl.BlockSpec((tm, D_MODEL), row),
            pl.BlockSpec((tm, MIX_HALF), row),
            pl.BlockSpec((tm, MIX_HALF), row),
            pl.BlockSpec((D_MODEL, D_MODEL), fixed),
            pl.BlockSpec((1, D_MODEL), fixed),
            pl.BlockSpec((2, D_MODEL, LANES), lambda i: (0, 0, 0)),
        ],
        out_specs=(pl.BlockSpec((tm, D_MODEL), row),
                   pl.BlockSpec((tm, D_MODEL // 2), row),
                   pl.BlockSpec((tm, LANES), row)),
        out_shape=(jax.ShapeDtypeStruct((n, D_MODEL), F32),
                   jax.ShapeDtypeStruct((n, D_MODEL // 2), jnp.uint32),
                   jax.ShapeDtypeStruct((n, LANES), F32)),
        compiler_params=_params(("parallel",)),
        name="outproj_router",
    )(x2d, gdn2d, diff2d, w_out16, norm_w, wr_pair)


def _moe_plan(e_idx, w, tile, max_blocks):
    n = e_idx.shape[0]
    n_tiles = n // tile
    m = 2 * tile
    e = e_idx.reshape(n_tiles, m)
    wt = w.reshape(n_tiles, m)
    counts = jnp.sum((e[..., None] == jnp.arange(N_EXPERTS, dtype=jnp.int32)).astype(jnp.int32), axis=1)
    order = jnp.argsort(e, axis=1, stable=True).astype(jnp.int32)
    starts = jnp.cumsum(counts, axis=1) - counts
    nb_e = (counts + MOE_BLOCK - 1) // MOE_BLOCK
    pad_ends = jnp.cumsum(nb_e, axis=1)
    pad_starts = pad_ends - nb_e
    nblk = pad_ends[:, -1]
    blk = jnp.arange(max_blocks, dtype=jnp.int32)
    block_e = jax.vmap(lambda pe: jnp.searchsorted(pe, blk, side="right"))(pad_ends).astype(jnp.int32)
    block_e = jnp.minimum(block_e, N_EXPERTS - 1)
    last_e = jnp.take_along_axis(block_e, jnp.maximum(nblk - 1, 0)[:, None], axis=1)
    live = blk[None, :] < nblk[:, None]
    block_e = jnp.where(live, block_e, last_e)
    j0 = (blk[None, :] - jnp.take_along_axis(pad_starts, block_e, axis=1)) * MOE_BLOCK
    r = j0[..., None] + jnp.arange(MOE_BLOCK, dtype=jnp.int32)
    cnt_b = jnp.take_along_axis(counts, block_e, axis=1)
    valid = (r < cnt_b[..., None]) & live[..., None]
    src = jnp.take_along_axis(starts, block_e, axis=1)[..., None] + r
    src = jnp.clip(src, 0, m - 1).reshape(n_tiles, -1)
    a = jnp.take_along_axis(order, src, axis=1).reshape(n_tiles, max_blocks, MOE_BLOCK)
    row_tok = jnp.where(valid, a // 2, 0).astype(jnp.int32)
    row_w = jnp.where(valid, jnp.take_along_axis(wt, a.reshape(n_tiles, -1), axis=1).reshape(a.shape), 0.0)
    nvalid = jnp.sum(valid.astype(jnp.int32), axis=-1)
    return (block_e.reshape(-1), nblk.astype(jnp.int32), nvalid.reshape(-1),
            row_tok.reshape(n_tiles * max_blocks, 1, MOE_BLOCK),
            row_w.astype(F32).reshape(n_tiles * max_blocks, 1, MOE_BLOCK))


def _moe_body(be_ref, nblk_ref, nvalid_ref, hp_ref, tok_ref, w_ref, wg_ref, wu_ref, wd_ref,
              y_ref, xg_scr, yb_scr, *, max_blocks):
    t = pl.program_id(0)
    b = pl.program_id(1)

    @pl.when(b == 0)
    def _zero():
        y_ref[...] = jnp.zeros(y_ref.shape, F32)

    @pl.when(b < nblk_ref[t])
    def _block():
        def gather(i, carry):
            tok = tok_ref[0, i]
            xg_scr[pl.ds(i, 1), :] = hp_ref[pl.ds(tok, 1), :]
            return carry

        lax.fori_loop(0, MOE_BLOCK, gather, 0, unroll=8)
        xp = xg_scr[...]
        half = D_MODEL // 2
        x_lo = pltpu.bitcast(xp << 16, F32).astype(BF16)
        x_hi = pltpu.bitcast(xp & jnp.uint32(0xFFFF0000), F32).astype(BF16)
        gt = _dot(x_lo, wg_ref[:half, :]) + _dot(x_hi, wg_ref[half:, :])
        up = _dot(x_lo, wu_ref[:half, :]) + _dot(x_hi, wu_ref[half:, :])
        act = (jax.nn.silu(gt) * up).astype(BF16)
        yb_scr[...] = _dot(act, wd_ref[...])

        def scatter(i, carry):
            tok = tok_ref[0, i]
            wt = w_ref[0, i]
            y_ref[pl.ds(tok, 1), :] = y_ref[pl.ds(tok, 1), :] + wt * yb_scr[pl.ds(i, 1), :]
            return carry

        lax.fori_loop(0, nvalid_ref[t * max_blocks + b], scatter, 0)


def _moe(hp, plan, wg16, wu16, wd16, tile, max_blocks):
    n = hp.shape[0]
    n_tiles = n // tile
    block_e, nblk, nvalid, row_tok, row_w = plan
    body = functools.partial(_moe_body, max_blocks=max_blocks)
    half = D_MODEL // 2
    wmap = lambda t, b, be, nb, nv: (be[t * max_blocks + b], 0, 0)
    smap = lambda t, b, be, nb, nv: (t * max_blocks + b, 0, 0)
    grid_spec = pltpu.PrefetchScalarGridSpec(
        num_scalar_prefetch=3,
        grid=(n_tiles, max_blocks),
        in_specs=[
            pl.BlockSpec((tile, half), lambda t, b, be, nb, nv: (t, 0)),
            pl.BlockSpec((None, 1, MOE_BLOCK), smap, memory_space=pltpu.SMEM),
            pl.BlockSpec((None, 1, MOE_BLOCK), smap, memory_space=pltpu.SMEM),
            pl.BlockSpec((None, D_MODEL, EXPERT_FF), wmap),
            pl.BlockSpec((None, D_MODEL, EXPERT_FF), wmap),
            pl.BlockSpec((None, EXPERT_FF, D_MODEL), wmap),
        ],
        out_specs=pl.BlockSpec((tile, D_MODEL), lambda t, b, be, nb, nv: (t, 0)),
        scratch_shapes=[pltpu.VMEM((MOE_BLOCK, half), jnp.uint32),
                        pltpu.VMEM((MOE_BLOCK, D_MODEL), F32)],
    )
    return pl.pallas_call(
        body,
        grid_spec=grid_spec,
        out_shape=jax.ShapeDtypeStruct((n, D_MODEL), F32),
        compiler_params=_params(("parallel", "arbitrary")),
        name="moe",
    )(block_e, nblk, nvalid, hp, row_tok, row_w, wg16, wu16, wd16)


def _final_body(x_ref, y_ref, nw_ref, o_ref):
    x = x_ref[...] + y_ref[...]
    o_ref[...] = x * lax.rsqrt(jnp.mean(x * x, axis=-1, keepdims=True) + NORM_EPS) * nw_ref[...]


def _final_norm(x2, y, norm_w, tm):
    n = x2.shape[0]
    row = lambda i: (i, 0)
    return pl.pallas_call(
        _final_body,
        grid=(n // tm,),
        in_specs=[pl.BlockSpec((tm, D_MODEL), row), pl.BlockSpec((tm, D_MODEL), row),
                  pl.BlockSpec((1, D_MODEL), lambda i: (0, 0))],
        out_specs=pl.BlockSpec((tm, D_MODEL), row),
        out_shape=jax.ShapeDtypeStruct((n, D_MODEL), F32),
        compiler_params=_params(("parallel",)),
        name="final_norm",
    )(x2, y, norm_w)


def _pick_tile(n, pref):
    t = min(n, pref)
    while n % t:
        t //= 2
    return t


def _ffn(x2d, gdn2d, diff2d, w_out16, norm_ffn_row, wr_pair, wg16, wu16, wd16, norm_final_row):
    n = x2d.shape[0]
    tm = _pick_tile(n, 512)
    x2, hp, route = _outproj(x2d, gdn2d, diff2d, w_out16, norm_ffn_row, wr_pair, tm)
    e_idx = route[:, 0:2].astype(jnp.int32)
    w = route[:, 2:4]
    tile = _pick_tile(n, 2048)
    max_blocks = 2 * tile // MOE_BLOCK + N_EXPERTS
    plan = _moe_plan(e_idx, w, tile, max_blocks)
    y = _moe(hp, plan, wg16, wu16, wd16, tile, max_blocks)
    return _final_norm(x2, y, norm_final_row, tm)


def kernel(x_prompt, x_sample, cache_k, cache_v, state_conv, state_gdn, page_table, norm_mix, w_in, conv_w, gdn_a_log, gdn_dt_bias, gdn_norm, lambda_q1, lambda_k1, lambda_q2, lambda_k2, diff_subln, rel_bias, w_out, norm_ffn, router_group, router_expert, w_gate, w_up, w_down, norm_final):
    bsz, seq, _ = x_prompt.shape
    dbsz, dseq, _ = x_sample.shape
    H = N_HEADS
    l = 0

    wl = w_in[l]
    n_ab = 2 * H
    w_packed = jnp.concatenate(
        [wl[:, :_C_GATE[1]], wl[:, _C_GATE[1]:_C_GATE[1] + n_ab],
         jnp.zeros((D_MODEL, LANES - n_ab), wl.dtype), wl[:, _C_GATE[1] + n_ab:]], axis=1).astype(BF16)
    norm_mix_row = norm_mix[l].reshape(1, D_MODEL)
    pad_h = jnp.zeros((LANES - H,), F32)
    alog_row = jnp.concatenate([gdn_a_log[l].astype(F32), pad_h]).reshape(1, LANES)
    dtb_row = jnp.concatenate([gdn_dt_bias[l].astype(F32), pad_h]).reshape(1, LANES)
    gdn_norm_row = jnp.tile(gdn_norm[l].astype(F32), HEADS_PER_GROUP).reshape(1, GROUP_W)
    lam_vecs = jnp.stack([lambda_q1[l], lambda_k1[l], lambda_q2[l], lambda_k2[l]]).astype(F32)
    subln_row = diff_subln[l].astype(F32).reshape(1, HEAD_DIM)
    w_out16 = w_out[l].astype(BF16)
    norm_ffn_row = norm_ffn[l].reshape(1, D_MODEL)
    wr = jnp.concatenate([router_group[l], router_expert[l].reshape(D_MODEL, N_EXPERTS),
                          jnp.zeros((D_MODEL, LANES - N_ROUTE_GROUPS - N_EXPERTS), F32)], axis=1)
    wr1 = wr.astype(BF16)
    wr_pair = jnp.stack([wr1, (wr - wr1.astype(F32)).astype(BF16)])
    wg16 = w_gate[l].astype(BF16)
    wu16 = w_up[l].astype(BF16)
    wd16 = w_down[l].astype(BF16)
    norm_final_row = norm_final.reshape(1, D_MODEL)

    n_p = bsz * seq
    xp2d = x_prompt.reshape(n_p, D_MODEL)
    zqkv, zgate, zab, kf, vf, qh, kh, vh = _inproj(xp2d, norm_mix_row, w_packed, _pick_tile(n_p, 256))
    conv_zero = jnp.zeros((bsz, SUBLANES, CONV_DIM), F32)
    gdn_p, s_p = _gdn(zqkv.reshape(bsz, seq, CONV_DIM), zgate.reshape(bsz, seq, MIX_HALF),
                      zab.reshape(bsz, seq, LANES), conv_w[l], alog_row, dtb_row, gdn_norm_row,
                      conv_zero, None, seq)
    tq = 256
    bias_tab = _prompt_bias_tables(rel_bias, tq)
    hm = lambda a: a.reshape(H, bsz, seq, HEAD_DIM)
    diff_p = _prompt_attention(hm(qh), hm(kh), hm(vh), bias_tab, lam_vecs, subln_row, tq)
    y_prompt = _ffn(xp2d, gdn_p.reshape(n_p, MIX_HALF), diff_p.reshape(n_p, MIX_HALF), w_out16,
                    norm_ffn_row, wr_pair, wg16, wu16, wd16, norm_final_row).reshape(bsz, seq, D_MODEL)
    new_k_prompt = kf.reshape(1, bsz, seq, H, HEAD_DIM)
    new_v_prompt = vf.reshape(1, bsz, seq, H, HEAD_DIM)
    new_conv_prompt = zqkv.reshape(bsz, seq, CONV_DIM)[:, seq - (CONV_W - 1):, :][None]
    new_gdn_prompt = s_p[None]

    n_s = dbsz * dseq
    xs2d = x_sample.reshape(n_s, D_MODEL)
    zqkv_s, zgate_s, zab_s, kf_s, vf_s, qh_s, _, _ = _inproj(xs2d, norm_mix_row, w_packed, _pick_tile(n_s, 256))
    C = GDN_CHUNK
    padt = lambda a: jnp.pad(a.reshape(dbsz, dseq, -1), ((0, 0), (0, C - dseq), (0, 0)))
    conv_init = jnp.pad(state_conv[l].astype(F32), ((0, 0), (SUBLANES - (CONV_W - 1), 0), (0, 0)))
    gdn_s, s_s = _gdn(padt(zqkv_s), padt(zgate_s), padt(zab_s), conv_w[l], alog_row, dtb_row,
                      gdn_norm_row, conv_init, state_gdn[l].astype(F32), dseq)
    gdn_s = gdn_s[:, :dseq, :].reshape(n_s, MIX_HALF)
    t_pad = SUBLANES
    q_rows = jnp.transpose(qh_s.reshape(H, dbsz, dseq, HEAD_DIM), (1, 0, 2, 3)).reshape(dbsz, H * dseq, HEAD_DIM)
    newpad = lambda a: jnp.pad(a.reshape(dbsz, dseq, H, HEAD_DIM),
                               ((0, 0), (0, t_pad - dseq), (0, 0), (0, 0))).reshape(dbsz, t_pad * H, HEAD_DIM)
    madd, bnear, bnew = _sample_bias_tables(rel_bias, dseq, t_pad)
    n_pages = page_table.shape[1]
    ch = 8 if n_pages % 8 == 0 else 1
    diff_s = _sample_attention(page_table, q_rows, newpad(kf_s), newpad(vf_s), cache_k, cache_v,
                               madd, bnear, bnew, lam_vecs, subln_row, ch)
    diff_s = jnp.transpose(diff_s.reshape(dbsz, H, dseq, HEAD_DIM), (0, 2, 1, 3)).reshape(n_s, MIX_HALF)
    y_sample = _ffn(xs2d, gdn_s, diff_s.astype(BF16), w_out16, norm_ffn_row, wr_pair, wg16, wu16, wd16,
                    norm_final_row).reshape(dbsz, dseq, D_MODEL)
    new_k_sample = kf_s.reshape(1, dbsz, dseq, H, HEAD_DIM)
    new_v_sample = vf_s.reshape(1, dbsz, dseq, H, HEAD_DIM)
    zq3 = zqkv_s.reshape(dbsz, dseq, CONV_DIM)
    new_conv_sample = jnp.concatenate([state_conv[l].astype(F32), zq3], axis=1)[:, -(CONV_W - 1):, :][None]
    new_gdn_sample = s_s[None]

    return (y_prompt, y_sample, new_k_prompt, new_v_prompt, new_k_sample, new_v_sample,
            new_conv_prompt, new_conv_sample, new_gdn_prompt, new_gdn_sample)
```

```python
import functools
import math

import jax
import jax.numpy as jnp
from jax import lax
from jax.experimental import pallas as pl
from jax.experimental.pallas import tpu as pltpu

F32 = jnp.float32
BF16 = jnp.bfloat16

D_MODEL = 1024
N_HEADS = 8
HEAD_DIM = 64
MAP_DIM = 32
MIX_HALF = N_HEADS * HEAD_DIM
CONV_W = 4
CONV_DIM = 3 * MIX_HALF
GDN_CHUNK = 64
HEADS_PER_GROUP = 4
GROUP_W = HEADS_PER_GROUP * HEAD_DIM
N_GROUPS_GDN = N_HEADS // HEADS_PER_GROUP
PAGE = 128
N_BUCKETS = 32
MAX_DISTANCE = 128
N_ROUTE_GROUPS = 4
EXPERTS_PER_GROUP = 8
N_EXPERTS = N_ROUTE_GROUPS * EXPERTS_PER_GROUP
EXPERT_FF = D_MODEL // 2
MOE_BLOCK = 128
NORM_EPS = 1e-6
SUBLN_EPS = 1e-5
LAMBDA_INIT = 0.8 - 0.6 * math.exp(-0.3 * 0)
LOG2E = math.log2(math.e)

LANES = 128
SUBLANES = 8
BF16_ROWS = 16
VMEM_LIMIT = 48 * 1024 * 1024

_C_QKV = (0, CONV_DIM)
_C_GATE = (_C_QKV[1], _C_QKV[1] + MIX_HALF)
_C_AB = (_C_GATE[1], _C_GATE[1] + LANES)
_C_DQ = (_C_AB[1], _C_AB[1] + MIX_HALF)
_C_DK = (_C_DQ[1], _C_DQ[1] + MIX_HALF)
_C_DV = (_C_DK[1], _C_DK[1] + MIX_HALF)
IN_COLS_PACKED = _C_DV[1]


def _params(sem, vmem=VMEM_LIMIT):
    return pltpu.CompilerParams(dimension_semantics=sem, vmem_limit_bytes=vmem)


def _dot(a, b):
    return jnp.dot(a, b, preferred_element_type=F32)


def _dot_nt(a, b):
    return lax.dot_general(a, b, (((1,), (1,)), ((), ())), preferred_element_type=F32)


def _dot_tn(a, b):
    return lax.dot_general(a, b, (((0,), (0,)), ((), ())), preferred_element_type=F32)


def _split2(x):
    x1 = x.astype(BF16)
    x2 = (x - x1.astype(F32)).astype(BF16)
    return x1, x2


def _split3(x):
    x1 = x.astype(BF16)
    r = x - x1.astype(F32)
    x2 = r.astype(BF16)
    x3 = (r - x2.astype(F32)).astype(BF16)
    return x1, x2, x3


def _dot_x01(x, m01):
    x1, x2, x3 = _split3(x)
    return _dot(x1, m01) + (_dot(x2, m01) + _dot(x3, m01))


def _dot_01x(m01, x):
    x1, x2, x3 = _split3(x)
    return _dot(m01, x1) + (_dot(m01, x2) + _dot(m01, x3))


def _iota(shape, dim):
    return lax.broadcasted_iota(jnp.int32, shape, dim)


def _one_hot_f(cond):
    return jnp.where(cond, 1.0, 0.0).astype(BF16)


def _inproj_body(x_ref, nw_ref, w_ref, zqkv_ref, zgate_ref, zab_ref, kf_ref, vf_ref,
                 qf_ref, qh_ref, kh_ref, vh_ref):
    x = x_ref[...]
    h = x * lax.rsqrt(jnp.mean(x * x, axis=-1, keepdims=True) + NORM_EPS) * nw_ref[...]
    hb = h.astype(BF16)

    def proj(cols):
        return _dot(hb, w_ref[:, cols[0]:cols[1]])

    zqkv_ref[...] = proj(_C_QKV)
    zgate_ref[...] = proj(_C_GATE)
    zab_ref[...] = proj(_C_AB)
    zq = proj(_C_DQ) * (MAP_DIM ** -0.5)
    zk = proj(_C_DK)
    zv = proj(_C_DV)
    kf_ref[...] = zk
    vf_ref[...] = zv
    qf_ref[...] = zq.astype(BF16)
    tm = x.shape[0]
    zq2 = zq * LOG2E
    zeros = jnp.zeros((tm, HEAD_DIM), BF16)
    ones = jnp.ones((tm, HEAD_DIM), BF16)
    two_ones = jnp.where(_iota((tm, HEAD_DIM), 1) < 2, 1.0, 0.0).astype(BF16)
    for h_i in range(N_HEADS):
        sl = slice(HEAD_DIM * h_i, HEAD_DIM * (h_i + 1))
        qh_ref[h_i] = jnp.concatenate([zq2[:, sl].astype(BF16), zeros], axis=-1)
        kh_ref[h_i] = jnp.concatenate([zk[:, sl].astype(BF16), two_ones], axis=-1)
        vh_ref[h_i] = jnp.concatenate([zv[:, sl].astype(BF16), ones], axis=-1)


def _inproj(x2d, norm_w, w_packed, tm):
    n = x2d.shape[0]
    assert n % tm == 0
    row = lambda i: (i, 0)
    fixed = lambda i: (0, 0)
    hm = lambda i: (0, i, 0)
    out_shape = (
        jax.ShapeDtypeStruct((n, CONV_DIM), F32),
        jax.ShapeDtypeStruct((n, MIX_HALF), F32),
        jax.ShapeDtypeStruct((n, LANES), F32),
        jax.ShapeDtypeStruct((n, MIX_HALF), F32),
        jax.ShapeDtypeStruct((n, MIX_HALF), F32),
        jax.ShapeDtypeStruct((n, MIX_HALF), BF16),
        jax.ShapeDtypeStruct((N_HEADS, n, LANES), BF16),
        jax.ShapeDtypeStruct((N_HEADS, n, LANES), BF16),
        jax.ShapeDtypeStruct((N_HEADS, n, LANES), BF16),
    )
    out_specs = (
        pl.BlockSpec((tm, CONV_DIM), row),
        pl.BlockSpec((tm, MIX_HALF), row),
        pl.BlockSpec((tm, LANES), row),
        pl.BlockSpec((tm, MIX_HALF), row),
        pl.BlockSpec((tm, MIX_HALF), row),
        pl.BlockSpec((tm, MIX_HALF), row),
        pl.BlockSpec((N_HEADS, tm, LANES), hm),
        pl.BlockSpec((N_HEADS, tm, LANES), hm),
        pl.BlockSpec((N_HEADS, tm, LANES), hm),
    )
    return pl.pallas_call(
        _inproj_body,
        grid=(n // tm,),
        in_specs=[pl.BlockSpec((tm, D_MODEL), row),
                  pl.BlockSpec((1, D_MODEL), fixed),
                  pl.BlockSpec((D_MODEL, IN_COLS_PACKED), fixed)],
        out_specs=out_specs,
        out_shape=out_shape,
        compiler_params=_params(("parallel",)),
        name="inproj",
    )(x2d, norm_w, w_packed)


def _gdn_body(z_ref, zg_ref, zab_ref, cw_ref, alog_ref, dtb_ref, nw_ref, ci_ref, s0_ref,
              o_ref, sfin_ref, s_scr, prev_scr, *, t_valid, zero_init, nb):
    C = GDN_CHUNK
    W = GROUP_W
    c = pl.program_id(1)
    n_chunks = pl.num_programs(1)

    rw = _iota((W, W), 0)
    cw = _iota((W, W), 1)
    blk = (rw // HEAD_DIM) == (cw // HEAD_DIM)
    ones_blk = _one_hot_f(blk)
    ri = _iota((C, W), 0)
    cj = _iota((C, W), 1) % C
    incl = ri >= cj
    strict = ri > cj
    eye_all = ri == cj
    eye_f = jnp.where(eye_all, 1.0, 0.0)
    zero16 = jnp.zeros((), BF16)

    @pl.when(c == 0)
    def _init():
        prev_scr[...] = ci_ref[...]
        for sg in range(nb * N_GROUPS_GDN):
            bb, g = divmod(sg, N_GROUPS_GDN)
            if zero_init:
                s_scr[sg] = jnp.zeros((W, W), F32)
            else:
                s0 = s0_ref[bb, g * HEADS_PER_GROUP:(g + 1) * HEADS_PER_GROUP].reshape(W, HEAD_DIM)
                et = _one_hot_f(_iota((HEAD_DIM, W), 0) == (_iota((HEAD_DIM, W), 1) % HEAD_DIM))
                s_scr[sg] = jnp.where(blk, _dot_x01(s0, et), 0.0)

    tri = _one_hot_f(_iota((C, C), 0) >= _iota((C, C), 1))

    def blocksum(v):
        v1, v2 = _split2(v)
        return _dot(v1, ones_blk) + _dot(v2, ones_blk)

    def bd(x16):
        return jnp.where(blk, jnp.concatenate([x16] * HEADS_PER_GROUP, axis=0), zero16)

    seqs = range(nb)
    chains = [(bb, g) for bb in seqs for g in range(N_GROUPS_GDN)]

    def each(fn, *lists):
        return [fn(*args) for args in zip(*lists)]

    w = cw_ref[...]
    xc = [jnp.concatenate([prev_scr[bb], z_ref[bb]], axis=0) for bb in seqs]
    y = each(lambda a: (a * w[3:4] + pltpu.roll(a, 1, 0) * w[2:3]
                        + pltpu.roll(a, 2, 0) * w[1:2] + pltpu.roll(a, 3, 0) * w[0:1]), xc)
    for bb in seqs:
        prev_scr[bb] = xc[bb][C:C + SUBLANES]
    qkv = each(lambda a: jax.nn.silu(a[SUBLANES:SUBLANES + C]), y)

    valid = (_iota((C, LANES), 0) + c * C) < t_valid
    zab = [zab_ref[bb] for bb in seqs]
    neg_a = -jnp.exp(alog_ref[...])
    gfull = each(lambda a: jnp.where(valid, neg_a * jax.nn.softplus(a + dtb_ref[...]), 0.0), zab)
    bfull = each(lambda a: jnp.where(valid, jax.nn.sigmoid(a), 0.0), zab)
    gcum_s = each(lambda a: _dot_01x(tri, a), gfull)

    lane_r = _iota((LANES, W), 0)
    head_c = _iota((LANES, W), 1) // HEAD_DIM
    e_g = [_one_hot_f(lane_r == head_c + g * HEADS_PER_GROUP) for g in range(N_GROUPS_GDN)]
    e_b = [_one_hot_f(lane_r == head_c + g * HEADS_PER_GROUP + N_HEADS) for g in range(N_GROUPS_GDN)]

    def cols(part, g):
        return slice(part * MIX_HALF + g * W, part * MIX_HALF + (g + 1) * W)

    q = [qkv[bb][:, cols(0, g)] for bb, g in chains]
    k = [qkv[bb][:, cols(1, g)] for bb, g in chains]
    v = [qkv[bb][:, cols(2, g)] for bb, g in chains]
    q = each(lambda a: a * lax.rsqrt(blocksum(a * a) + 1e-6) * (HEAD_DIM ** -0.5), q)
    k = each(lambda a: a * lax.rsqrt(blocksum(a * a) + 1e-6), k)
    gexp = [_dot_x01(gcum_s[bb], e_g[g]) for bb, g in chains]
    bexp = [_dot_x01(bfull[bb], e_b[g]) for bb, g in chains]
    grow = each(lambda a: jnp.sum(jnp.where(eye_all, a, 0.0), axis=0, keepdims=True), gexp)
    decay = each(lambda a, r: jnp.where(incl, jnp.exp(jnp.where(incl, a - r, 0.0)), 0.0), gexp, grow)

    kb = each(lambda a, b: a * b, k, bexp)
    kkqk = each(lambda a, b, kk: _dot_nt(jnp.concatenate([a, b], axis=0).astype(BF16), bd(kk.astype(BF16))),
                kb, q, k)
    lmat = each(lambda a, d: jnp.where(strict, a[:C] * d, 0.0), kkqk, decay)
    qkd = each(lambda a, d: a[C:] * d, kkqk, decay)

    tinv = each(lambda a: eye_f - a, lmat)
    p16 = each(lambda a: a.astype(BF16), lmat)
    bp = each(bd, p16)
    for _ in range(int(math.log2(C)) - 1):
        p16 = each(lambda a, b: _dot(a, b).astype(BF16), p16, bp)
        bp = each(bd, p16)
        tinv = each(lambda t, b: t + _dot(t.astype(BF16), b), tinv, bp)

    eg = each(jnp.exp, gexp)
    vb = each(lambda a, b: a * b, v, bexp)
    kbg = each(lambda a, b: a * b, kb, eg)
    corr = each(lambda t, a, b: _dot((t - eye_f).astype(BF16),
                                     jnp.concatenate([bd(a.astype(BF16)), bd(b.astype(BF16))], axis=1)),
                tinv, vb, kbg)
    u = each(lambda a, cr: a + cr[:, :W], vb, corr)
    wm = each(lambda a, cr: a + cr[:, W:], kbg, corr)
    s = [s_scr[bb * N_GROUPS_GDN + g] for bb, g in chains]
    s16 = each(lambda a: a.astype(BF16), s)
    v_new = each(lambda a, b, st: a - _dot(b.astype(BF16), st), u, wm, s16)
    o = each(lambda a, e, st, d, vn: _dot((a * e).astype(BF16), st) + _dot(d.astype(BF16), bd(vn.astype(BF16))),
             q, eg, s16, qkd, v_new)
    glast = each(lambda a: a[C - 1:C, :], gexp)
    kd = each(lambda a, gl, ge: a * jnp.exp(gl - ge), k, glast, gexp)
    upd = each(lambda a, vn: _dot_tn(a.astype(BF16), vn.astype(BF16)), kd, v_new)
    s_new = each(lambda st, gl, up: st * jnp.exp(gl) + jnp.where(blk, up, 0.0), s, glast, upd)
    for (bb, g), st in zip(chains, s_new):
        s_scr[bb * N_GROUPS_GDN + g] = st

    on = each(lambda a: a * lax.rsqrt(blocksum(a * a) * (1.0 / HEAD_DIM) + NORM_EPS) * nw_ref[...], o)
    for (bb, g), a in zip(chains, on):
        o_ref[bb, :, cols(0, g)] = (a * jax.nn.silu(zg_ref[bb, :, cols(0, g)])).astype(o_ref.dtype)

    @pl.when(c == n_chunks - 1)
    def _fin():
        ett = _one_hot_f((_iota((W, HEAD_DIM), 0) % HEAD_DIM) == _iota((W, HEAD_DIM), 1))
        for sg in range(nb * N_GROUPS_GDN):
            bb, g = divmod(sg, N_GROUPS_GDN)
            sout = _dot_x01(s_scr[sg], ett)
            sfin_ref[bb, g * HEADS_PER_GROUP:(g + 1) * HEADS_PER_GROUP] = sout.reshape(
                HEADS_PER_GROUP, HEAD_DIM, HEAD_DIM)


def _gdn(zqkv, zgate, zab, conv_w, alog_row, dtb_row, norm_row, conv_init, s0, t_valid):
    bsz, t, _ = zqkv.shape
    C = GDN_CHUNK
    assert t % C == 0
    n_chunks = t // C
    nb = 2 if bsz % 2 == 0 else 1
    zero_init = s0 is None
    if zero_init:
        s0 = jnp.zeros((nb, N_HEADS, HEAD_DIM, HEAD_DIM), F32)
        s0_map = lambda b, c: (0, 0, 0, 0)
    else:
        s0_map = lambda b, c: (b, 0, 0, 0)
    chunk = lambda b, c: (b, c, 0)
    fixed = lambda b, c: (0, 0)
    in_specs = [
        pl.BlockSpec((nb, C, CONV_DIM), chunk),
        pl.BlockSpec((nb, C, MIX_HALF), chunk),
        pl.BlockSpec((nb, C, LANES), chunk),
        pl.BlockSpec((CONV_W, CONV_DIM), fixed),
        pl.BlockSpec((1, LANES), fixed),
        pl.BlockSpec((1, LANES), fixed),
        pl.BlockSpec((1, GROUP_W), fixed),
        pl.BlockSpec((nb, SUBLANES, CONV_DIM), lambda b, c: (b, 0, 0)),
        pl.BlockSpec((nb, N_HEADS, HEAD_DIM, HEAD_DIM), s0_map),
    ]
    out_specs = (
        pl.BlockSpec((nb, C, MIX_HALF), chunk),
        pl.BlockSpec((nb, N_HEADS, HEAD_DIM, HEAD_DIM), lambda b, c: (b, 0, 0, 0)),
    )
    out_shape = (
        jax.ShapeDtypeStruct((bsz, t, MIX_HALF), BF16),
        jax.ShapeDtypeStruct((bsz, N_HEADS, HEAD_DIM, HEAD_DIM), F32),
    )
    body = functools.partial(_gdn_body, t_valid=t_valid, zero_init=zero_init, nb=nb)
    return pl.pallas_call(
        body,
        grid=(bsz // nb, n_chunks),
        in_specs=in_specs,
        out_specs=out_specs,
        out_shape=out_shape,
        scratch_shapes=[pltpu.VMEM((nb * N_GROUPS_GDN, GROUP_W, GROUP_W), F32),
                        pltpu.VMEM((nb, SUBLANES, CONV_DIM), F32)],
        compiler_params=_params(("parallel", "arbitrary")),
        name="gdn",
    )(zqkv, zgate, zab, conv_w, alog_row, dtb_row, norm_row, conv_init, s0)


def _t5_bucket(rel):
    n = jnp.maximum(rel, 0)
    max_exact = N_BUCKETS // 2
    large = max_exact + (jnp.log(jnp.maximum(n, 1).astype(F32) / max_exact)
                         / math.log(MAX_DISTANCE / max_exact) * (N_BUCKETS - max_exact)).astype(jnp.int32)
    large = jnp.minimum(large, N_BUCKETS - 1)
    return jnp.where(n < max_exact, n, large)


def _rel_bias_shifted(rel_bias, rel):
    rb = rel_bias.astype(F32)
    sel = _t5_bucket(rel)[..., None] == jnp.arange(N_BUCKETS, dtype=jnp.int32)
    picked = jnp.sum(jnp.where(sel[..., None], rb, 0.0), axis=-2)
    return picked - rb[N_BUCKETS - 1]


def _lambda(lam_ref):
    lam = lam_ref[...]
    a = jnp.exp(jnp.sum(lam[0:1] * lam[1:2], axis=-1, keepdims=True))
    b = jnp.exp(jnp.sum(lam[2:3] * lam[3:4], axis=-1, keepdims=True))
    return a - b + LAMBDA_INIT


def _subln(o, w_row):
    ms = jnp.mean(o * o, axis=-1, keepdims=True)
    return o * lax.rsqrt(ms + SUBLN_EPS) * w_row * (1.0 - LAMBDA_INIT)


def _loop_pairs(n, body):
    def two(i, carry):
        body(2 * i)
        body(2 * i + 1)
        return carry

    def one(j, carry):
        body(j)
        return carry

    lax.fori_loop(0, n // 2, two, 0)
    lax.fori_loop((n // 2) * 2, n, one, 0)


def _pattn_body(q_ref, k_ref, v_ref, bias_ref, lam_ref, sw_ref, o_ref, o_scr, mv_scr, acc_scr, *, tq):
    qi = pl.program_id(1)
    h = pl.program_id(2)
    tk = tq
    prev = jnp.maximum(qi - 1, 0)
    prev_off = jnp.where(qi > 0, 0.0, -jnp.inf)
    zero = jnp.zeros((), BF16)

    def fold(s):
        out = s[:, :LANES]
        for cidx in range(1, tk // LANES):
            out = jnp.maximum(out, s[:, cidx * LANES:(cidx + 1) * LANES])
        return out

    def k_tile(j):
        return k_ref[pl.ds(pl.multiple_of(j * tk, tk), tk), :]

    q = q_ref[...]
    lane = _iota((tq, LANES), 1)
    qs = jnp.concatenate([jnp.where(lane < MAP_DIM, q, zero), jnp.where(lane >= MAP_DIM, q, zero)], axis=0)
    b0 = bias_ref[0]
    b0 = jnp.concatenate([b0, b0], axis=0)
    b1 = bias_ref[1] + prev_off
    b1 = jnp.concatenate([b1, b1], axis=0)

    mv_scr[...] = jnp.maximum(fold(_dot_nt(qs, k_tile(qi)) + b0), fold(_dot_nt(qs, k_tile(prev)) + b1))

    def max_tile(j):
        mv_scr[...] = jnp.maximum(mv_scr[...], fold(_dot_nt(qs, k_tile(j))))

    _loop_pairs(prev, max_tile)
    m = jnp.max(mv_scr[...], axis=-1, keepdims=True)
    m_hi = m.astype(BF16)
    m_lo = (m - m_hi.astype(F32)).astype(BF16)
    lane2 = _iota((2 * tq, LANES), 1)
    q2 = jnp.where(lane2 == HEAD_DIM, -m_hi, jnp.where(lane2 == HEAD_DIM + 1, -m_lo, qs))

    def pv(j, bias):
        s = _dot_nt(q2, k_tile(j))
        if bias is not None:
            s = s + bias
        return _dot(jnp.exp2(s).astype(BF16), v_ref[pl.ds(pl.multiple_of(j * tk, tk), tk), :])

    acc_scr[...] = pv(qi, b0) + pv(prev, b1)

    def acc_tile(j):
        acc_scr[...] += pv(j, None)

    _loop_pairs(prev, acc_tile)
    acc = acc_scr[...]
    o = acc[:, :HEAD_DIM] / acc[:, HEAD_DIM:HEAD_DIM + 1]
    od = o[:tq] - _lambda(lam_ref) * o[tq:]
    o_scr[h] = _subln(od, sw_ref[...])

    @pl.when(h == N_HEADS - 1)
    def _store():
        o_ref[...] = jnp.concatenate([o_scr[i] for i in range(N_HEADS)], axis=-1).astype(o_ref.dtype)


def _prompt_attention(qh, kh, vh, bias_tab, lam_vecs, subln_row, tq):
    _, bsz, t, _ = qh.shape
    assert t % tq == 0
    nq = t // tq
    body = functools.partial(_pattn_body, tq=tq)
    return pl.pallas_call(
        body,
        grid=(bsz, nq, N_HEADS),
        in_specs=[
            pl.BlockSpec((None, None, tq, LANES), lambda b, i, h: (h, b, i, 0)),
            pl.BlockSpec((None, None, t, LANES), lambda b, i, h: (h, b, 0, 0)),
            pl.BlockSpec((None, None, t, LANES), lambda b, i, h: (h, b, 0, 0)),
            pl.BlockSpec((None, 2, tq, tq), lambda b, i, h: (h, 0, 0, 0)),
            pl.BlockSpec((4, MAP_DIM), lambda b, i, h: (0, 0)),
            pl.BlockSpec((1, HEAD_DIM), lambda b, i, h: (0, 0)),
        ],
        out_specs=pl.BlockSpec((None, tq, MIX_HALF), lambda b, i, h: (b, i, 0)),
        out_shape=jax.ShapeDtypeStruct((bsz, t, MIX_HALF), BF16),
        scratch_shapes=[pltpu.VMEM((N_HEADS, tq, HEAD_DIM), F32),
                        pltpu.VMEM((2 * tq, LANES), F32),
                        pltpu.VMEM((2 * tq, LANES), F32)],
        compiler_params=_params(("parallel", "parallel", "arbitrary")),
        name="prompt_attn",
    )(qh, kh, vh, bias_tab, lam_vecs, subln_row)


def _prompt_bias_tables(rel_bias, tq):
    r = jnp.arange(-tq + 1, tq, dtype=jnp.int32)
    f0 = jnp.where((r >= 0)[:, None], _rel_bias_shifted(rel_bias, r), -jnp.inf)
    f1 = _rel_bias_shifted(rel_bias, r + tq)

    def toeplitz(f):
        period = 2 * tq
        g = jnp.concatenate([f[tq - 1::-1], f[:1], f[:tq - 1:-1]], axis=0)
        x = jnp.tile(g, (tq, 1))[:tq * (period - 1)].reshape(tq, period - 1, f.shape[-1])
        return x[:, :tq]

    return jnp.transpose(jnp.stack([toeplitz(f0), toeplitz(f1)]), (3, 0, 1, 2)) * LOG2E


def _sattn_body(pt_ref, q_ref, kn_ref, vn_ref, bnear_ref, bnew_ref, lam_ref, sw_ref,
                ck_hbm, cv_hbm, o_ref, kbuf, vbuf, sem, m_scr, l_scr, acc_scr, *, ch, n_chunks, n_seq):
    b = pl.program_id(0)
    c = pl.program_id(1)
    step = b * n_chunks + c
    slot = step % 2
    nr = q_ref.shape[0]
    t_new = nr // (2 * N_HEADS)

    def copies(bb, cc, sl):
        out = []
        for p in range(ch):
            pg = pt_ref[bb, cc * ch + p]
            out.append(pltpu.make_async_copy(ck_hbm.at[0, pg], kbuf.at[sl, p], sem.at[0, sl]))
            out.append(pltpu.make_async_copy(cv_hbm.at[0, pg], vbuf.at[sl, p], sem.at[1, sl]))
        return out

    @pl.when(step == 0)
    def _prime():
        for cp in copies(0, 0, 0):
            cp.start()

    @pl.when(step + 1 < n_seq * n_chunks)
    def _prefetch():
        nxt = step + 1
        for cp in copies(nxt // n_chunks, nxt % n_chunks, 1 - slot):
            cp.start()

    for cp in copies(b, c, slot):
        cp.wait()

    @pl.when(c == 0)
    def _init():
        m_scr[...] = jnp.full(m_scr.shape, -jnp.inf, F32)
        l_scr[...] = jnp.zeros(l_scr.shape, F32)
        acc_scr[...] = jnp.zeros(acc_scr.shape, F32)

    row = _iota((nr, MIX_HALF), 0)
    col = _iota((nr, MIX_HALF), 1)
    own = ((col // HEAD_DIM) == ((row % (nr // 2)) // t_new)) & (((col % HEAD_DIM) // MAP_DIM) == (row // (nr // 2)))
    own_head = (col // HEAD_DIM) == ((row % (nr // 2)) // t_new)
    qbd = jnp.where(own, q_ref[...], jnp.zeros((), BF16))

    is_last = c == n_chunks - 1
    rows_hd = N_HEADS * HEAD_DIM
    kt = jnp.concatenate([kbuf[slot, p].reshape(rows_hd, PAGE) for p in range(ch)], axis=-1).astype(BF16)
    s = _dot(qbd, kt)
    near = jnp.where(is_last, bnear_ref[...], 0.0)
    if ch > 1:
        s = jnp.concatenate([s[:, :(ch - 1) * PAGE], s[:, (ch - 1) * PAGE:] + near], axis=-1)
    else:
        s = s + near

    m_prev = m_scr[:, 0:1]
    l_prev = l_scr[:, 0:1]
    m_new = jnp.maximum(m_prev, jnp.max(s, axis=-1, keepdims=True))
    alpha = jnp.exp(m_prev - m_new)
    pr = jnp.exp(s - m_new)
    l_new = alpha * l_prev + jnp.sum(pr, axis=-1, keepdims=True)
    vt = jnp.concatenate([vbuf[slot, p].reshape(rows_hd, PAGE) for p in range(ch)], axis=-1).astype(BF16)
    acc = alpha * acc_scr[...] + _dot_nt(pr.astype(BF16), vt)
    m_scr[...] = jnp.broadcast_to(m_new, m_scr.shape)
    l_scr[...] = jnp.broadcast_to(l_new, l_scr.shape)
    acc_scr[...] = acc

    @pl.when(is_last)
    def _fin():
        sn = _dot_nt(qbd, kn_ref[...].astype(BF16)) + bnew_ref[...]
        m2 = jnp.maximum(m_new, jnp.max(sn, axis=-1, keepdims=True))
        a2 = jnp.exp(m_new - m2)
        p2 = jnp.exp(sn - m2)
        l2 = a2 * l_new + jnp.sum(p2, axis=-1, keepdims=True)
        acc2 = a2 * acc + _dot(p2.astype(BF16), vn_ref[...].astype(BF16))
        o_full = jnp.where(own_head, acc2 / l2, 0.0)
        fold = _one_hot_f((_iota((MIX_HALF, HEAD_DIM), 0) % HEAD_DIM) == _iota((MIX_HALF, HEAD_DIM), 1))
        o = _dot_x01(o_full, fold)
        half = nr // 2
        od = o[:half] - _lambda(lam_ref) * o[half:]
        o_ref[...] = _subln(od, sw_ref[...])


def _sample_attention(page_table, q_rep, k_new, v_new, cache_kt, cache_vt, bnear, bnew,
                      lam_vecs, subln_row, ch):
    n_seq, n_pages = page_table.shape
    assert n_pages % ch == 0
    n_chunks = n_pages // ch
    nr = q_rep.shape[1]
    t_pad = k_new.shape[1]
    body = functools.partial(_sattn_body, ch=ch, n_chunks=n_chunks, n_seq=n_seq)
    grid_spec = pltpu.PrefetchScalarGridSpec(
        num_scalar_prefetch=1,
        grid=(n_seq, n_chunks),
        in_specs=[
            pl.BlockSpec((None, nr, MIX_HALF), lambda b, c, pt: (b, 0, 0)),
            pl.BlockSpec((None, t_pad, MIX_HALF), lambda b, c, pt: (b, 0, 0)),
            pl.BlockSpec((None, t_pad, MIX_HALF), lambda b, c, pt: (b, 0, 0)),
            pl.BlockSpec((nr, PAGE), lambda b, c, pt: (0, 0)),
            pl.BlockSpec((nr, t_pad), lambda b, c, pt: (0, 0)),
            pl.BlockSpec((4, MAP_DIM), lambda b, c, pt: (0, 0)),
            pl.BlockSpec((1, HEAD_DIM), lambda b, c, pt: (0, 0)),
            pl.BlockSpec(memory_space=pl.ANY),
            pl.BlockSpec(memory_space=pl.ANY),
        ],
        out_specs=pl.BlockSpec((None, nr // 2, HEAD_DIM), lambda b, c, pt: (b, 0, 0)),
        scratch_shapes=[
            pltpu.VMEM((2, ch, N_HEADS, HEAD_DIM, PAGE), F32),
            pltpu.VMEM((2, ch, N_HEADS, HEAD_DIM, PAGE), F32),
            pltpu.SemaphoreType.DMA((2, 2)),
            pltpu.VMEM((nr, LANES), F32),
            pltpu.VMEM((nr, LANES), F32),
            pltpu.VMEM((nr, MIX_HALF), F32),
        ],
    )
    return pl.pallas_call(
        body,
        grid_spec=grid_spec,
        out_shape=jax.ShapeDtypeStruct((n_seq, nr // 2, HEAD_DIM), F32),
        compiler_params=_params(("arbitrary", "arbitrary")),
        name="sample_attn",
    )(page_table, q_rep, k_new, v_new, bnear, bnew, lam_vecs, subln_row, cache_kt, cache_vt)


def _sample_bias_tables(rel_bias, t_new, t_pad):
    H = N_HEADS
    nr = 2 * H * t_new
    row = jnp.arange(nr, dtype=jnp.int32)
    row_h = (row % (H * t_new)) // t_new
    row_t = row % t_new
    pick = (row_h[:, None] == jnp.arange(H, dtype=jnp.int32)[None, :])

    def table(rel):
        allh = _rel_bias_shifted(rel_bias, rel)
        return jnp.sum(jnp.where(pick[:, None, :], allh, 0.0), axis=-1)

    pos = jnp.arange(PAGE, dtype=jnp.int32)
    bnear = table(PAGE + row_t[:, None] - pos[None, :])
    tn = jnp.arange(t_pad, dtype=jnp.int32)
    reln = row_t[:, None] - tn[None, :]
    ok = (reln >= 0) & (tn[None, :] < t_new)
    bnew = jnp.where(ok, table(reln), -jnp.inf)
    return bnear.astype(F32), bnew.astype(F32)


def _outproj_body(x_ref, gdn_ref, diff_ref, wo_ref, nw_ref, wr_ref, x2_ref, hp_ref, route_ref):
    x2 = (x_ref[...] + _dot(gdn_ref[...], wo_ref[:MIX_HALF, :])
          + _dot(diff_ref[...], wo_ref[MIX_HALF:, :]))
    x2_ref[...] = x2
    h = x2 * lax.rsqrt(jnp.mean(x2 * x2, axis=-1, keepdims=True) + NORM_EPS) * nw_ref[...]
    h1, h2 = _split2(h)
    logits = _dot(h1, wr_ref[0]) + (_dot(h1, wr_ref[1]) + _dot(h2, wr_ref[0]))
    tm = logits.shape[0]
    lane = _iota((tm, LANES), 1)
    neg = -jnp.inf
    big = jnp.int32(10_000)
    gl = jnp.where(lane < N_ROUTE_GROUPS, logits, neg)
    gm = jnp.max(gl, axis=-1, keepdims=True)
    g_idx = jnp.min(jnp.where(gl == gm, lane, big), axis=-1, keepdims=True)
    g_w = 1.0 / jnp.sum(jnp.exp(gl - gm), axis=-1, keepdims=True)
    lo = N_ROUTE_GROUPS + EXPERTS_PER_GROUP * g_idx
    el = jnp.where((lane >= lo) & (lane < lo + EXPERTS_PER_GROUP), logits, neg)
    em = jnp.max(el, axis=-1, keepdims=True)
    i1 = jnp.min(jnp.where(el == em, lane, big), axis=-1, keepdims=True)
    el2 = jnp.where(lane == i1, neg, el)
    em2 = jnp.max(el2, axis=-1, keepdims=True)
    i2 = jnp.min(jnp.where(el2 == em2, lane, big), axis=-1, keepdims=True)
    e2 = jnp.exp(em2 - em)
    w1 = g_w / (1.0 + e2)
    w2 = g_w * e2 / (1.0 + e2)
    route = jnp.where(lane == 0, (i1 - N_ROUTE_GROUPS).astype(F32),
                      jnp.where(lane == 1, (i2 - N_ROUTE_GROUPS).astype(F32),
                                jnp.where(lane == 2, w1, jnp.where(lane == 3, w2, 0.0))))
    route_ref[...] = route
    hb = h.astype(BF16).astype(F32)
    lo_bits = pltpu.bitcast(hb[:, :D_MODEL // 2], jnp.uint32) >> 16
    hi_bits = pltpu.bitcast(hb[:, D_MODEL // 2:], jnp.uint32) & jnp.uint32(0xFFFF0000)
    hp_ref[...] = lo_bits | hi_bits


def _outproj(x2d, gdn2d, diff2d, w_out16, norm_w, wr_pair, tm):
    n = x2d.shape[0]
    assert n % tm == 0
    row = lambda i: (i, 0)
    fixed = lambda i: (0, 0)
    return pl.pallas_call(
        _outproj_body,
        grid=(n // tm,),
        in_specs=[
            pl.BlockSpec((tm, D_MODEL), row),
            pl.BlockSpec((tm, MIX_HALF), row),
            pl.BlockSpec((tm, MIX_HALF), row),
            pl.BlockSpec((D_MODEL, D_MODEL), fixed),
            pl.BlockSpec((1, D_MODEL), fixed),
            pl.BlockSpec((2, D_MODEL, LANES), lambda i: (0, 0, 0)),
        ],
        out_specs=(pl.BlockSpec((tm, D_MODEL), row),
                   pl.BlockSpec((tm, D_MODEL // 2), row),
                   pl.BlockSpec((tm, LANES), row)),
        out_shape=(jax.ShapeDtypeStruct((n, D_MODEL), F32),
                   jax.ShapeDtypeStruct((n, D_MODEL // 2), jnp.uint32),
                   jax.ShapeDtypeStruct((n, LANES), F32)),
        compiler_params=_params(("parallel",)),
        name="outproj_router",
    )(x2d, gdn2d, diff2d, w_out16, norm_w, wr_pair)


def _moe_plan(e_idx, w, tile, max_blocks):
    n = e_idx.shape[0]
    n_tiles = n // tile
    m = 2 * tile
    experts = jnp.arange(N_EXPERTS, dtype=jnp.int32)
    e = e_idx.reshape(n_tiles, m)
    onehot = (e[..., None] == experts).astype(jnp.int32)
    csum = jnp.cumsum(onehot, axis=1)
    counts = csum[:, -1, :]
    nb_e = (counts + MOE_BLOCK - 1) // MOE_BLOCK
    pad_ends = jnp.cumsum(nb_e, axis=1)
    pad_starts = pad_ends - nb_e
    nblk = pad_ends[:, -1]
    dest = jnp.sum(onehot * (pad_starts[:, None, :] * MOE_BLOCK + csum - 1), axis=-1)
    blk = jnp.arange(max_blocks, dtype=jnp.int32)
    block_e = jnp.sum((pad_ends[:, None, :] <= blk[None, :, None]).astype(jnp.int32), axis=-1)
    block_e = jnp.minimum(block_e, N_EXPERTS - 1)
    oh_b = (block_e[..., None] == experts).astype(jnp.int32)
    cnt_b = jnp.sum(oh_b * counts[:, None, :], axis=-1)
    ps_b = jnp.sum(oh_b * pad_starts[:, None, :], axis=-1)
    live = blk[None, :] < nblk[:, None]
    nvalid = jnp.where(live, jnp.clip(cnt_b - (blk[None, :] - ps_b) * MOE_BLOCK, 0, MOE_BLOCK), 0)
    last_e = jnp.max(jnp.where(counts > 0, experts[None, :], 0), axis=-1, keepdims=True)
    block_e = jnp.where(live, block_e, last_e)
    return (block_e.reshape(-1).astype(jnp.int32), nblk.astype(jnp.int32), nvalid.reshape(-1).astype(jnp.int32),
            dest.astype(jnp.int32).reshape(n_tiles, 1, m), w.astype(F32).reshape(n_tiles, 1, m))


_SCATTER_BATCH = 4
_INVERT_BATCH = 8


def _moe_body(be_ref, nblk_ref, nvalid_ref, hp_ref, dest_ref, wa_ref, wg_ref, wu_ref, wd_ref,
              y_ref, tok_s, w_s, xg_scr, yb_scr, *, max_blocks):
    t = pl.program_id(0)
    b = pl.program_id(1)
    m = dest_ref.shape[1]

    @pl.when(b == 0)
    def _start_tile():
        y_ref[...] = jnp.zeros(y_ref.shape, F32)

        def invert(ab, carry):
            a0 = ab * _INVERT_BATCH
            dests = [dest_ref[0, a0 + u] for u in range(_INVERT_BATCH)]
            wts = [wa_ref[0, a0 + u] for u in range(_INVERT_BATCH)]
            for u in range(_INVERT_BATCH):
                tok_s[dests[u]] = (a0 + u) // 2
                w_s[dests[u]] = wts[u]
            return carry

        lax.fori_loop(0, m // _INVERT_BATCH, invert, 0)

    @pl.when(b < nblk_ref[t])
    def _block():
        base = b * MOE_BLOCK
        nv = nvalid_ref[t * max_blocks + b]

        def gather(i, carry):
            tok = tok_s[base + jnp.minimum(i, nv - 1)]
            xg_scr[pl.ds(i, 1), :] = hp_ref[pl.ds(tok, 1), :]
            return carry

        lax.fori_loop(0, MOE_BLOCK, gather, 0, unroll=8)
        xp = xg_scr[...]
        half = D_MODEL // 2
        x_lo = pltpu.bitcast(xp << 16, F32).astype(BF16)
        x_hi = pltpu.bitcast(xp & jnp.uint32(0xFFFF0000), F32).astype(BF16)
        gt = _dot(x_lo, wg_ref[:half, :]) + _dot(x_hi, wg_ref[half:, :])
        up = _dot(x_lo, wu_ref[:half, :]) + _dot(x_hi, wu_ref[half:, :])
        act = (jax.nn.silu(gt) * up).astype(BF16)
        yb_scr[...] = _dot(act, wd_ref[...])

        def scatter_batch(ib, carry):
            i0 = ib * _SCATTER_BATCH
            toks = [tok_s[base + i0 + u] for u in range(_SCATTER_BATCH)]
            vals = [y_ref[pl.ds(toks[u], 1), :] + w_s[base + i0 + u] * yb_scr[pl.ds(i0 + u, 1), :]
                    for u in range(_SCATTER_BATCH)]
            for u in range(_SCATTER_BATCH):
                y_ref[pl.ds(toks[u], 1), :] = vals[u]
            return carry

        nfull = nv // _SCATTER_BATCH
        lax.fori_loop(0, nfull, scatter_batch, 0)

        def scatter_one(i, carry):
            tok = tok_s[base + i]
            y_ref[pl.ds(tok, 1), :] = y_ref[pl.ds(tok, 1), :] + w_s[base + i] * yb_scr[pl.ds(i, 1), :]
            return carry

        lax.fori_loop(nfull * _SCATTER_BATCH, nv, scatter_one, 0)


def _moe(hp, plan, wg16, wu16, wd16, tile, max_blocks):
    n = hp.shape[0]
    n_tiles = n // tile
    block_e, nblk, nvalid, dest, wa = plan
    m = 2 * tile
    body = functools.partial(_moe_body, max_blocks=max_blocks)
    half = D_MODEL // 2
    wmap = lambda t, b, be, nb, nv: (be[t * max_blocks + b], 0, 0)
    tmap = lambda t, b, be, nb, nv: (t, 0, 0)
    grid_spec = pltpu.PrefetchScalarGridSpec(
        num_scalar_prefetch=3,
        grid=(n_tiles, max_blocks),
        in_specs=[
            pl.BlockSpec((tile, half), lambda t, b, be, nb, nv: (t, 0)),
            pl.BlockSpec((None, 1, m), tmap, memory_space=pltpu.SMEM),
            pl.BlockSpec((None, 1, m), tmap, memory_space=pltpu.SMEM),
            pl.BlockSpec((None, D_MODEL, EXPERT_FF), wmap),
            pl.BlockSpec((None, D_MODEL, EXPERT_FF), wmap),
            pl.BlockSpec((None, EXPERT_FF, D_MODEL), wmap),
        ],
        out_specs=pl.BlockSpec((tile, D_MODEL), lambda t, b, be, nb, nv: (t, 0)),
        scratch_shapes=[pltpu.SMEM((max_blocks * MOE_BLOCK,), jnp.int32),
                        pltpu.SMEM((max_blocks * MOE_BLOCK,), F32),
                        pltpu.VMEM((MOE_BLOCK, half), jnp.uint32),
                        pltpu.VMEM((MOE_BLOCK, D_MODEL), F32)],
    )
    return pl.pallas_call(
        body,
        grid_spec=grid_spec,
        out_shape=jax.ShapeDtypeStruct((n, D_MODEL), F32),
        compiler_params=_params(("arbitrary", "arbitrary")),
        name="moe",
    )(block_e, nblk, nvalid, hp, dest, wa, wg16, wu16, wd16)


def _final_body(x_ref, y_ref, nw_ref, o_ref):
    x = x_ref[...] + y_ref[...]
    o_ref[...] = x * lax.rsqrt(jnp.mean(x * x, axis=-1, keepdims=True) + NORM_EPS) * nw_ref[...]


def _final_norm(x2, y, norm_w, tm):
    n = x2.shape[0]
    row = lambda i: (i, 0)
    return pl.pallas_call(
        _final_body,
        grid=(n // tm,),
        in_specs=[pl.BlockSpec((tm, D_MODEL), row), pl.BlockSpec((tm, D_MODEL), row),
                  pl.BlockSpec((1, D_MODEL), lambda i: (0, 0))],
        out_specs=pl.BlockSpec((tm, D_MODEL), row),
        out_shape=jax.ShapeDtypeStruct((n, D_MODEL), F32),
        compiler_params=_params(("parallel",)),
        name="final_norm",
    )(x2, y, norm_w)


def _pick_tile(n, pref):
    t = min(n, pref)
    while n % t:
        t //= 2
    return t


def _ffn(x2d, gdn2d, diff2d, w_out16, norm_ffn_row, wr_pair, wg16, wu16, wd16, norm_final_row):
    n = x2d.shape[0]
    tm = _pick_tile(n, 512)
    x2, hp, route = _outproj(x2d, gdn2d, diff2d, w_out16, norm_ffn_row, wr_pair, tm)
    e_idx = route[:, 0:2].astype(jnp.int32)
    w = route[:, 2:4]
    tile = _pick_tile(n, 2048)
    max_blocks = 2 * tile // MOE_BLOCK + N_EXPERTS
    plan = _moe_plan(e_idx, w, tile, max_blocks)
    y = _moe(hp, plan, wg16, wu16, wd16, tile, max_blocks)
    return _final_norm(x2, y, norm_final_row, tm)


def kernel(x_prompt, x_sample, cache_k, cache_v, state_conv, state_gdn, page_table, norm_mix, w_in, conv_w, gdn_a_log, gdn_dt_bias, gdn_norm, lambda_q1, lambda_k1, lambda_q2, lambda_k2, diff_subln, rel_bias, w_out, norm_ffn, router_group, router_expert, w_gate, w_up, w_down, norm_final):
    bsz, seq, _ = x_prompt.shape
    dbsz, dseq, _ = x_sample.shape
    H = N_HEADS
    l = 0

    wl = w_in[l]
    n_ab = 2 * H
    w_packed = jnp.concatenate(
        [wl[:, :_C_GATE[1]], wl[:, _C_GATE[1]:_C_GATE[1] + n_ab],
         jnp.zeros((D_MODEL, LANES - n_ab), wl.dtype), wl[:, _C_GATE[1] + n_ab:]], axis=1).astype(BF16)
    norm_mix_row = norm_mix[l].reshape(1, D_MODEL)
    pad_h = jnp.zeros((LANES - H,), F32)
    alog_row = jnp.concatenate([gdn_a_log[l].astype(F32), pad_h]).reshape(1, LANES)
    dtb_row = jnp.concatenate([gdn_dt_bias[l].astype(F32), pad_h]).reshape(1, LANES)
    gdn_norm_row = jnp.tile(gdn_norm[l].astype(F32), HEADS_PER_GROUP).reshape(1, GROUP_W)
    lam_vecs = jnp.stack([lambda_q1[l], lambda_k1[l], lambda_q2[l], lambda_k2[l]]).astype(F32)
    subln_row = diff_subln[l].astype(F32).reshape(1, HEAD_DIM)
    w_out16 = w_out[l].astype(BF16)
    norm_ffn_row = norm_ffn[l].reshape(1, D_MODEL)
    wr = jnp.concatenate([router_group[l], router_expert[l].reshape(D_MODEL, N_EXPERTS),
                          jnp.zeros((D_MODEL, LANES - N_ROUTE_GROUPS - N_EXPERTS), F32)], axis=1)
    wr1 = wr.astype(BF16)
    wr_pair = jnp.stack([wr1, (wr - wr1.astype(F32)).astype(BF16)])
    wg16 = w_gate[l].astype(BF16)
    wu16 = w_up[l].astype(BF16)
    wd16 = w_down[l].astype(BF16)
    norm_final_row = norm_final.reshape(1, D_MODEL)

    n_p = bsz * seq
    xp2d = x_prompt.reshape(n_p, D_MODEL)
    zqkv, zgate, zab, kf, vf, _, qh, kh, vh = _inproj(xp2d, norm_mix_row, w_packed, _pick_tile(n_p, 256))
    conv_zero = jnp.zeros((bsz, SUBLANES, CONV_DIM), F32)
    gdn_p, s_p = _gdn(zqkv.reshape(bsz, seq, CONV_DIM), zgate.reshape(bsz, seq, MIX_HALF),
                      zab.reshape(bsz, seq, LANES), conv_w[l], alog_row, dtb_row, gdn_norm_row,
                      conv_zero, None, seq)
    tq = 256
    bias_tab = _prompt_bias_tables(rel_bias, tq)
    hm = lambda a: a.reshape(H, bsz, seq, a.shape[-1])
    diff_p = _prompt_attention(hm(qh), hm(kh), hm(vh), bias_tab, lam_vecs, subln_row, tq)
    y_prompt = _ffn(xp2d, gdn_p.reshape(n_p, MIX_HALF), diff_p.reshape(n_p, MIX_HALF), w_out16,
                    norm_ffn_row, wr_pair, wg16, wu16, wd16, norm_final_row).reshape(bsz, seq, D_MODEL)
    new_k_prompt = kf.reshape(1, bsz, seq, H, HEAD_DIM)
    new_v_prompt = vf.reshape(1, bsz, seq, H, HEAD_DIM)
    new_conv_prompt = zqkv.reshape(bsz, seq, CONV_DIM)[:, seq - (CONV_W - 1):, :][None]
    new_gdn_prompt = s_p[None]

    n_s = dbsz * dseq
    xs2d = x_sample.reshape(n_s, D_MODEL)
    zqkv_s, zgate_s, zab_s, kf_s, vf_s, qf_s, _, _, _ = _inproj(xs2d, norm_mix_row, w_packed, _pick_tile(n_s, 256))
    C = GDN_CHUNK
    padt = lambda a: jnp.pad(a.reshape(dbsz, dseq, -1), ((0, 0), (0, C - dseq), (0, 0)))
    conv_init = jnp.pad(state_conv[l].astype(F32), ((0, 0), (SUBLANES - (CONV_W - 1), 0), (0, 0)))
    gdn_s, s_s = _gdn(padt(zqkv_s), padt(zgate_s), padt(zab_s), conv_w[l], alog_row, dtb_row,
                      gdn_norm_row, conv_init, state_gdn[l].astype(F32), dseq)
    gdn_s = gdn_s[:, :dseq, :].reshape(n_s, MIX_HALF)
    t_pad = BF16_ROWS
    q_rep = jnp.broadcast_to(qf_s.reshape(dbsz, 1, 1, dseq, MIX_HALF),
                             (dbsz, 2, H, dseq, MIX_HALF)).reshape(dbsz, 2 * H * dseq, MIX_HALF)
    newpad = lambda a: jnp.pad(a.reshape(dbsz, dseq, MIX_HALF), ((0, 0), (0, t_pad - dseq), (0, 0)))
    bnear, bnew = _sample_bias_tables(rel_bias, dseq, t_pad)
    cache_kt = jnp.transpose(cache_k, (0, 1, 3, 4, 2))
    cache_vt = jnp.transpose(cache_v, (0, 1, 3, 4, 2))
    n_pages = page_table.shape[1]
    ch = 16 if n_pages % 16 == 0 else 1
    diff_s = _sample_attention(page_table, q_rep, newpad(kf_s), newpad(vf_s), cache_kt, cache_vt,
                               bnear, bnew, lam_vecs, subln_row, ch)
    diff_s = jnp.transpose(diff_s.reshape(dbsz, H, dseq, HEAD_DIM), (0, 2, 1, 3)).reshape(n_s, MIX_HALF)
    y_sample = _ffn(xs2d, gdn_s, diff_s.astype(BF16), w_out16, norm_ffn_row, wr_pair, wg16, wu16, wd16,
                    norm_final_row).reshape(dbsz, dseq, D_MODEL)
    new_k_sample = kf_s.reshape(1, dbsz, dseq, H, HEAD_DIM)
    new_v_sample = vf_s.reshape(1, dbsz, dseq, H, HEAD_DIM)
    zq3 = zqkv_s.reshape(dbsz, dseq, CONV_DIM)
    new_conv_sample = jnp.concatenate([state_conv[l].astype(F32), zq3], axis=1)[:, -(CONV_W - 1):, :][None]
    new_gdn_sample = s_s[None]

    return (y_prompt, y_sample, new_k_prompt, new_v_prompt, new_k_sample, new_v_sample,
            new_conv_prompt, new_conv_sample, new_gdn_prompt, new_gdn_sample)
```

```python
import functools
import math

import jax
import jax.numpy as jnp
from jax import lax
from jax.experimental import pallas as pl
from jax.experimental.pallas import tpu as pltpu

F32 = jnp.float32
BF16 = jnp.bfloat16

D_MODEL = 1024
N_HEADS = 8
HEAD_DIM = 64
MAP_DIM = 32
MIX_HALF = N_HEADS * HEAD_DIM
CONV_W = 4
CONV_DIM = 3 * MIX_HALF
GDN_CHUNK = 64
HEADS_PER_GROUP = 4
GROUP_W = HEADS_PER_GROUP * HEAD_DIM
N_GROUPS_GDN = N_HEADS // HEADS_PER_GROUP
PAGE = 128
N_BUCKETS = 32
MAX_DISTANCE = 128
N_ROUTE_GROUPS = 4
EXPERTS_PER_GROUP = 8
N_EXPERTS = N_ROUTE_GROUPS * EXPERTS_PER_GROUP
EXPERT_FF = D_MODEL // 2
MOE_BLOCK = 128
NORM_EPS = 1e-6
SUBLN_EPS = 1e-5
LAMBDA_INIT = 0.8 - 0.6 * math.exp(-0.3 * 0)
LOG2E = math.log2(math.e)

LANES = 128
SUBLANES = 8
BF16_ROWS = 16
VMEM_LIMIT = 48 * 1024 * 1024

_C_QKV = (0, CONV_DIM)
_C_GATE = (_C_QKV[1], _C_QKV[1] + MIX_HALF)
_C_AB = (_C_GATE[1], _C_GATE[1] + LANES)
_C_DQ = (_C_AB[1], _C_AB[1] + MIX_HALF)
_C_DK = (_C_DQ[1], _C_DQ[1] + MIX_HALF)
_C_DV = (_C_DK[1], _C_DK[1] + MIX_HALF)
IN_COLS_PACKED = _C_DV[1]


def _params(sem, vmem=VMEM_LIMIT):
    return pltpu.CompilerParams(dimension_semantics=sem, vmem_limit_bytes=vmem)


def _dot(a, b):
    return jnp.dot(a, b, preferred_element_type=F32)


def _dot_nt(a, b):
    return lax.dot_general(a, b, (((1,), (1,)), ((), ())), preferred_element_type=F32)


def _dot_tn(a, b):
    return lax.dot_general(a, b, (((0,), (0,)), ((), ())), preferred_element_type=F32)


def _split2(x):
    x1 = x.astype(BF16)
    x2 = (x - x1.astype(F32)).astype(BF16)
    return x1, x2


def _split3(x):
    x1 = x.astype(BF16)
    r = x - x1.astype(F32)
    x2 = r.astype(BF16)
    x3 = (r - x2.astype(F32)).astype(BF16)
    return x1, x2, x3


def _dot_x01(x, m01):
    x1, x2, x3 = _split3(x)
    return _dot(x1, m01) + (_dot(x2, m01) + _dot(x3, m01))


def _dot_01x(m01, x):
    x1, x2, x3 = _split3(x)
    return _dot(m01, x1) + (_dot(m01, x2) + _dot(m01, x3))


def _iota(shape, dim):
    return lax.broadcasted_iota(jnp.int32, shape, dim)


def _one_hot_f(cond):
    return jnp.where(cond, 1.0, 0.0).astype(BF16)


def _inproj_body(x_ref, nw_ref, w_ref, zqkv_ref, zgate_ref, zab_ref, kf_ref, vf_ref,
                 qf_ref, qh_ref, kh_ref, vh_ref):
    x = x_ref[...]
    h = x * lax.rsqrt(jnp.mean(x * x, axis=-1, keepdims=True) + NORM_EPS) * nw_ref[...]
    hb = h.astype(BF16)

    def proj(cols):
        return _dot(hb, w_ref[:, cols[0]:cols[1]])

    zqkv_ref[...] = proj(_C_QKV)
    zgate_ref[...] = proj(_C_GATE)
    zab_ref[...] = proj(_C_AB)
    zq = proj(_C_DQ) * (MAP_DIM ** -0.5)
    zk = proj(_C_DK)
    zv = proj(_C_DV)
    kf_ref[...] = zk
    vf_ref[...] = zv
    qf_ref[...] = zq.astype(BF16)
    tm = x.shape[0]
    zq2 = zq * LOG2E
    zeros = jnp.zeros((tm, HEAD_DIM), BF16)
    ones = jnp.ones((tm, HEAD_DIM), BF16)
    two_ones = jnp.where(_iota((tm, HEAD_DIM), 1) < 2, 1.0, 0.0).astype(BF16)
    for h_i in range(N_HEADS):
        sl = slice(HEAD_DIM * h_i, HEAD_DIM * (h_i + 1))
        qh_ref[h_i] = jnp.concatenate([zq2[:, sl].astype(BF16), zeros], axis=-1)
        kh_ref[h_i] = jnp.concatenate([zk[:, sl].astype(BF16), two_ones], axis=-1)
        vh_ref[h_i] = jnp.concatenate([zv[:, sl].astype(BF16), ones], axis=-1)


def _inproj(x2d, norm_w, w_packed, tm):
    n = x2d.shape[0]
    assert n % tm == 0
    row = lambda i: (i, 0)
    fixed = lambda i: (0, 0)
    hm = lambda i: (0, i, 0)
    out_shape = (
        jax.ShapeDtypeStruct((n, CONV_DIM), F32),
        jax.ShapeDtypeStruct((n, MIX_HALF), F32),
        jax.ShapeDtypeStruct((n, LANES), F32),
        jax.ShapeDtypeStruct((n, MIX_HALF), F32),
        jax.ShapeDtypeStruct((n, MIX_HALF), F32),
        jax.ShapeDtypeStruct((n, MIX_HALF), BF16),
        jax.ShapeDtypeStruct((N_HEADS, n, LANES), BF16),
        jax.ShapeDtypeStruct((N_HEADS, n, LANES), BF16),
        jax.ShapeDtypeStruct((N_HEADS, n, LANES), BF16),
    )
    out_specs = (
        pl.BlockSpec((tm, CONV_DIM), row),
        pl.BlockSpec((tm, MIX_HALF), row),
        pl.BlockSpec((tm, LANES), row),
        pl.BlockSpec((tm, MIX_HALF), row),
        pl.BlockSpec((tm, MIX_HALF), row),
        pl.BlockSpec((tm, MIX_HALF), row),
        pl.BlockSpec((N_HEADS, tm, LANES), hm),
        pl.BlockSpec((N_HEADS, tm, LANES), hm),
        pl.BlockSpec((N_HEADS, tm, LANES), hm),
    )
    return pl.pallas_call(
        _inproj_body,
        grid=(n // tm,),
        in_specs=[pl.BlockSpec((tm, D_MODEL), row),
                  pl.BlockSpec((1, D_MODEL), fixed),
                  pl.BlockSpec((D_MODEL, IN_COLS_PACKED), fixed)],
        out_specs=out_specs,
        out_shape=out_shape,
        compiler_params=_params(("parallel",)),
        name="inproj",
    )(x2d, norm_w, w_packed)


def _gdn_body(z_ref, zg_ref, zab_ref, cw_ref, alog_ref, dtb_ref, nw_ref, ci_ref, s0_ref,
              o_ref, sfin_ref, s_scr, prev_scr, *, t_valid, zero_init, nb):
    C = GDN_CHUNK
    W = GROUP_W
    c = pl.program_id(1)
    n_chunks = pl.num_programs(1)

    rw = _iota((W, W), 0)
    cw = _iota((W, W), 1)
    blk = (rw // HEAD_DIM) == (cw // HEAD_DIM)
    ones_blk = _one_hot_f(blk)
    ri = _iota((C, W), 0)
    cj = _iota((C, W), 1) % C
    incl = ri >= cj
    strict = ri > cj
    eye_all = ri == cj
    eye_f = jnp.where(eye_all, 1.0, 0.0)
    zero16 = jnp.zeros((), BF16)

    @pl.when(c == 0)
    def _init():
        prev_scr[...] = ci_ref[...]
        for sg in range(nb * N_GROUPS_GDN):
            bb, g = divmod(sg, N_GROUPS_GDN)
            if zero_init:
                s_scr[sg] = jnp.zeros((W, W), F32)
            else:
                s0 = s0_ref[bb, g * HEADS_PER_GROUP:(g + 1) * HEADS_PER_GROUP].reshape(W, HEAD_DIM)
                et = _one_hot_f(_iota((HEAD_DIM, W), 0) == (_iota((HEAD_DIM, W), 1) % HEAD_DIM))
                s_scr[sg] = jnp.where(blk, _dot_x01(s0, et), 0.0)

    tri = _one_hot_f(_iota((C, C), 0) >= _iota((C, C), 1))

    def blocksum(v):
        v1, v2 = _split2(v)
        return _dot(v1, ones_blk) + _dot(v2, ones_blk)

    def bd(x16):
        return jnp.where(blk, jnp.concatenate([x16] * HEADS_PER_GROUP, axis=0), zero16)

    seqs = range(nb)
    chains = [(bb, g) for bb in seqs for g in range(N_GROUPS_GDN)]

    def each(fn, *lists):
        return [fn(*args) for args in zip(*lists)]

    w = cw_ref[...]
    xc = [jnp.concatenate([prev_scr[bb], z_ref[bb]], axis=0) for bb in seqs]
    y = each(lambda a: (a * w[3:4] + pltpu.roll(a, 1, 0) * w[2:3]
                        + pltpu.roll(a, 2, 0) * w[1:2] + pltpu.roll(a, 3, 0) * w[0:1]), xc)
    for bb in seqs:
        prev_scr[bb] = xc[bb][C:C + SUBLANES]
    qkv = each(lambda a: jax.nn.silu(a[SUBLANES:SUBLANES + C]), y)

    valid = (_iota((C, LANES), 0) + c * C) < t_valid
    zab = [zab_ref[bb] for bb in seqs]
    neg_a = -jnp.exp(alog_ref[...])
    gfull = each(lambda a: jnp.where(valid, neg_a * jax.nn.softplus(a + dtb_ref[...]), 0.0), zab)
    bfull = each(lambda a: jnp.where(valid, jax.nn.sigmoid(a), 0.0), zab)
    gcum_s = each(lambda a: _dot_01x(tri, a), gfull)

    lane_r = _iota((LANES, W), 0)
    head_c = _iota((LANES, W), 1) // HEAD_DIM
    e_g = [_one_hot_f(lane_r == head_c + g * HEADS_PER_GROUP) for g in range(N_GROUPS_GDN)]
    e_b = [_one_hot_f(lane_r == head_c + g * HEADS_PER_GROUP + N_HEADS) for g in range(N_GROUPS_GDN)]

    def cols(part, g):
        return slice(part * MIX_HALF + g * W, part * MIX_HALF + (g + 1) * W)

    q = [qkv[bb][:, cols(0, g)] for bb, g in chains]
    k = [qkv[bb][:, cols(1, g)] for bb, g in chains]
    v = [qkv[bb][:, cols(2, g)] for bb, g in chains]
    q = each(lambda a: a * lax.rsqrt(blocksum(a * a) + 1e-6) * (HEAD_DIM ** -0.5), q)
    k = each(lambda a: a * lax.rsqrt(blocksum(a * a) + 1e-6), k)
    gexp = [_dot_x01(gcum_s[bb], e_g[g]) for bb, g in chains]
    bexp = [_dot_x01(bfull[bb], e_b[g]) for bb, g in chains]
    grow = each(lambda a: jnp.sum(jnp.where(eye_all, a, 0.0), axis=0, keepdims=True), gexp)
    decay = each(lambda a, r: jnp.where(incl, jnp.exp(jnp.where(incl, a - r, 0.0)), 0.0), gexp, grow)

    kb = each(lambda a, b: a * b, k, bexp)
    kkqk = each(lambda a, b, kk: _dot_nt(jnp.concatenate([a, b], axis=0).astype(BF16), bd(kk.astype(BF16))),
                kb, q, k)
    lmat = each(lambda a, d: jnp.where(strict, a[:C] * d, 0.0), kkqk, decay)
    qkd = each(lambda a, d: a[C:] * d, kkqk, decay)

    tinv = each(lambda a: eye_f - a, lmat)
    p16 = each(lambda a: a.astype(BF16), lmat)
    bp = each(bd, p16)
    for _ in range(int(math.log2(C)) - 1):
        p16 = each(lambda a, b: _dot(a, b).astype(BF16), p16, bp)
        bp = each(bd, p16)
        tinv = each(lambda t, b: t + _dot(t.astype(BF16), b), tinv, bp)

    eg = each(jnp.exp, gexp)
    vb = each(lambda a, b: a * b, v, bexp)
    kbg = each(lambda a, b: a * b, kb, eg)
    corr = each(lambda t, a, b: _dot((t - eye_f).astype(BF16),
                                     jnp.concatenate([bd(a.astype(BF16)), bd(b.astype(BF16))], axis=1)),
                tinv, vb, kbg)
    u = each(lambda a, cr: a + cr[:, :W], vb, corr)
    wm = each(lambda a, cr: a + cr[:, W:], kbg, corr)
    s = [s_scr[bb * N_GROUPS_GDN + g] for bb, g in chains]
    s16 = each(lambda a: a.astype(BF16), s)
    v_new = each(lambda a, b, st: a - _dot(b.astype(BF16), st), u, wm, s16)
    o = each(lambda a, e, st, d, vn: _dot((a * e).astype(BF16), st) + _dot(d.astype(BF16), bd(vn.astype(BF16))),
             q, eg, s16, qkd, v_new)
    glast = each(lambda a: a[C - 1:C, :], gexp)
    kd = each(lambda a, gl, ge: a * jnp.exp(gl - ge), k, glast, gexp)
    upd = each(lambda a, vn: _dot_tn(a.astype(BF16), vn.astype(BF16)), kd, v_new)
    s_new = each(lambda st, gl, up: st * jnp.exp(gl) + jnp.where(blk, up, 0.0), s, glast, upd)
    for (bb, g), st in zip(chains, s_new):
        s_scr[bb * N_GROUPS_GDN + g] = st

    on = each(lambda a: a * lax.rsqrt(blocksum(a * a) * (1.0 / HEAD_DIM) + NORM_EPS) * nw_ref[...], o)
    for (bb, g), a in zip(chains, on):
        o_ref[bb, :, cols(0, g)] = (a * jax.nn.silu(zg_ref[bb, :, cols(0, g)])).astype(o_ref.dtype)

    @pl.when(c == n_chunks - 1)
    def _fin():
        ett = _one_hot_f((_iota((W, HEAD_DIM), 0) % HEAD_DIM) == _iota((W, HEAD_DIM), 1))
        for sg in range(nb * N_GROUPS_GDN):
            bb, g = divmod(sg, N_GROUPS_GDN)
            sout = _dot_x01(s_scr[sg], ett)
            sfin_ref[bb, g * HEADS_PER_GROUP:(g + 1) * HEADS_PER_GROUP] = sout.reshape(
                HEADS_PER_GROUP, HEAD_DIM, HEAD_DIM)


def _gdn(zqkv, zgate, zab, conv_w, alog_row, dtb_row, norm_row, conv_init, s0, t_valid):
    bsz, t, _ = zqkv.shape
    C = GDN_CHUNK
    assert t % C == 0
    n_chunks = t // C
    nb = 2 if bsz % 2 == 0 else 1
    zero_init = s0 is None
    if zero_init:
        s0 = jnp.zeros((nb, N_HEADS, HEAD_DIM, HEAD_DIM), F32)
        s0_map = lambda b, c: (0, 0, 0, 0)
    else:
        s0_map = lambda b, c: (b, 0, 0, 0)
    chunk = lambda b, c: (b, c, 0)
    fixed = lambda b, c: (0, 0)
    in_specs = [
        pl.BlockSpec((nb, C, CONV_DIM), chunk),
        pl.BlockSpec((nb, C, MIX_HALF), chunk),
        pl.BlockSpec((nb, C, LANES), chunk),
        pl.BlockSpec((CONV_W, CONV_DIM), fixed),
        pl.BlockSpec((1, LANES), fixed),
        pl.BlockSpec((1, LANES), fixed),
        pl.BlockSpec((1, GROUP_W), fixed),
        pl.BlockSpec((nb, SUBLANES, CONV_DIM), lambda b, c: (b, 0, 0)),
        pl.BlockSpec((nb, N_HEADS, HEAD_DIM, HEAD_DIM), s0_map),
    ]
    out_specs = (
        pl.BlockSpec((nb, C, MIX_HALF), chunk),
        pl.BlockSpec((nb, N_HEADS, HEAD_DIM, HEAD_DIM), lambda b, c: (b, 0, 0, 0)),
    )
    out_shape = (
        jax.ShapeDtypeStruct((bsz, t, MIX_HALF), BF16),
        jax.ShapeDtypeStruct((bsz, N_HEADS, HEAD_DIM, HEAD_DIM), F32),
    )
    body = functools.partial(_gdn_body, t_valid=t_valid, zero_init=zero_init, nb=nb)
    return pl.pallas_call(
        body,
        grid=(bsz // nb, n_chunks),
        in_specs=in_specs,
        out_specs=out_specs,
        out_shape=out_shape,
        scratch_shapes=[pltpu.VMEM((nb * N_GROUPS_GDN, GROUP_W, GROUP_W), F32),
                        pltpu.VMEM((nb, SUBLANES, CONV_DIM), F32)],
        compiler_params=_params(("parallel", "arbitrary")),
        name="gdn",
    )(zqkv, zgate, zab, conv_w, alog_row, dtb_row, norm_row, conv_init, s0)


def _t5_bucket(rel):
    n = jnp.maximum(rel, 0)
    max_exact = N_BUCKETS // 2
    large = max_exact + (jnp.log(jnp.maximum(n, 1).astype(F32) / max_exact)
                         / math.log(MAX_DISTANCE / max_exact) * (N_BUCKETS - max_exact)).astype(jnp.int32)
    large = jnp.minimum(large, N_BUCKETS - 1)
    return jnp.where(n < max_exact, n, large)


def _rel_bias_shifted(rel_bias, rel):
    rb = rel_bias.astype(F32)
    sel = _t5_bucket(rel)[..., None] == jnp.arange(N_BUCKETS, dtype=jnp.int32)
    picked = jnp.sum(jnp.where(sel[..., None], rb, 0.0), axis=-2)
    return picked - rb[N_BUCKETS - 1]


def _lambda(lam_ref):
    lam = lam_ref[...]
    a = jnp.exp(jnp.sum(lam[0:1] * lam[1:2], axis=-1, keepdims=True))
    b = jnp.exp(jnp.sum(lam[2:3] * lam[3:4], axis=-1, keepdims=True))
    return a - b + LAMBDA_INIT


def _subln(o, w_row):
    ms = jnp.mean(o * o, axis=-1, keepdims=True)
    return o * lax.rsqrt(ms + SUBLN_EPS) * w_row * (1.0 - LAMBDA_INIT)


def _loop_pairs(n, body):
    def two(i, carry):
        body(2 * i)
        body(2 * i + 1)
        return carry

    def one(j, carry):
        body(j)
        return carry

    lax.fori_loop(0, n // 2, two, 0)
    lax.fori_loop((n // 2) * 2, n, one, 0)


ATTN_HEADS_PER_STEP = 4


def _pattn_body(q_ref, k_ref, v_ref, bias_ref, lam_ref, sw_ref, o_ref, o_scr, mv_scr, acc_scr, *, tq):
    qi = pl.program_id(1)
    hp = pl.program_id(2)
    tk = tq
    heads = range(ATTN_HEADS_PER_STEP)
    prev = jnp.maximum(qi - 1, 0)
    prev_off = jnp.where(qi > 0, 0.0, -jnp.inf)
    zero = jnp.zeros((), BF16)

    def fold(s):
        out = s[:, :LANES]
        for cidx in range(1, tk // LANES):
            out = jnp.maximum(out, s[:, cidx * LANES:(cidx + 1) * LANES])
        return out

    def k_tile(u, j):
        return k_ref[u, pl.ds(pl.multiple_of(j * tk, tk), tk), :]

    def v_tile(u, j):
        return v_ref[u, pl.ds(pl.multiple_of(j * tk, tk), tk), :]

    def both(x):
        return jnp.concatenate([x, x], axis=0)

    lane = _iota((tq, LANES), 1)
    q = [q_ref[u] for u in heads]
    qs = [jnp.concatenate([jnp.where(lane < MAP_DIM, a, zero), jnp.where(lane >= MAP_DIM, a, zero)], axis=0)
          for a in q]
    b0 = [both(bias_ref[u, 0]) for u in heads]
    b1 = [both(bias_ref[u, 1] + prev_off) for u in heads]

    s_diag = [_dot_nt(qs[u], k_tile(u, qi)) for u in heads]
    s_prev = [_dot_nt(qs[u], k_tile(u, prev)) for u in heads]
    for u in heads:
        mv_scr[u] = jnp.maximum(fold(s_diag[u] + b0[u]), fold(s_prev[u] + b1[u]))

    def max_tile(j):
        ss = [_dot_nt(qs[u], k_tile(u, j)) for u in heads]
        for u in heads:
            mv_scr[u] = jnp.maximum(mv_scr[u], fold(ss[u]))

    _loop_pairs(prev, max_tile)
    m = [jnp.max(mv_scr[u], axis=-1, keepdims=True) for u in heads]
    m_hi = [a.astype(BF16) for a in m]
    m_lo = [(a - b.astype(F32)).astype(BF16) for a, b in zip(m, m_hi)]
    lane2 = _iota((2 * tq, LANES), 1)
    q2 = [jnp.where(lane2 == HEAD_DIM, -m_hi[u], jnp.where(lane2 == HEAD_DIM + 1, -m_lo[u], qs[u]))
          for u in heads]

    def pv(j, bias):
        ss = [_dot_nt(q2[u], k_tile(u, j)) for u in heads]
        if bias is not None:
            ss = [ss[u] + bias[u] for u in heads]
        ps = [jnp.exp2(a).astype(BF16) for a in ss]
        return [_dot(ps[u], v_tile(u, j)) for u in heads]

    pv_diag = pv(qi, b0)
    pv_prev = pv(prev, b1)
    for u in heads:
        acc_scr[u] = pv_diag[u] + pv_prev[u]

    def acc_tile(j):
        pvs = pv(j, None)
        for u in heads:
            acc_scr[u] += pvs[u]

    _loop_pairs(prev, acc_tile)
    lam = _lambda(lam_ref)
    for u in heads:
        acc = acc_scr[u]
        o = acc[:, :HEAD_DIM] / acc[:, HEAD_DIM:HEAD_DIM + 1]
        od = o[:tq] - lam * o[tq:]
        o_scr[hp * ATTN_HEADS_PER_STEP + u] = _subln(od, sw_ref[...])

    @pl.when(hp == N_HEADS // ATTN_HEADS_PER_STEP - 1)
    def _store():
        o_ref[...] = jnp.concatenate([o_scr[i] for i in range(N_HEADS)], axis=-1).astype(o_ref.dtype)


def _prompt_attention(qh, kh, vh, bias_tab, lam_vecs, subln_row, tq):
    _, bsz, t, _ = qh.shape
    assert t % tq == 0
    nq = t // tq
    hps = ATTN_HEADS_PER_STEP
    body = functools.partial(_pattn_body, tq=tq)
    return pl.pallas_call(
        body,
        grid=(bsz, nq, N_HEADS // hps),
        in_specs=[
            pl.BlockSpec((hps, None, tq, LANES), lambda b, i, h: (h, b, i, 0)),
            pl.BlockSpec((hps, None, t, LANES), lambda b, i, h: (h, b, 0, 0)),
            pl.BlockSpec((hps, None, t, LANES), lambda b, i, h: (h, b, 0, 0)),
            pl.BlockSpec((hps, 2, tq, tq), lambda b, i, h: (h, 0, 0, 0)),
            pl.BlockSpec((4, MAP_DIM), lambda b, i, h: (0, 0)),
            pl.BlockSpec((1, HEAD_DIM), lambda b, i, h: (0, 0)),
        ],
        out_specs=pl.BlockSpec((None, tq, MIX_HALF), lambda b, i, h: (b, i, 0)),
        out_shape=jax.ShapeDtypeStruct((bsz, t, MIX_HALF), BF16),
        scratch_shapes=[pltpu.VMEM((N_HEADS, tq, HEAD_DIM), F32),
                        pltpu.VMEM((hps, 2 * tq, LANES), F32),
                        pltpu.VMEM((hps, 2 * tq, LANES), F32)],
        compiler_params=_params(("parallel", "parallel", "arbitrary")),
        name="prompt_attn",
    )(qh, kh, vh, bias_tab, lam_vecs, subln_row)


def _prompt_bias_tables(rel_bias, tq):
    r = jnp.arange(-tq + 1, tq, dtype=jnp.int32)
    f0 = jnp.where((r >= 0)[:, None], _rel_bias_shifted(rel_bias, r), -jnp.inf)
    f1 = _rel_bias_shifted(rel_bias, r + tq)

    def toeplitz(f):
        period = 2 * tq
        g = jnp.concatenate([f[tq - 1::-1], f[:1], f[:tq - 1:-1]], axis=0)
        x = jnp.tile(g, (tq, 1))[:tq * (period - 1)].reshape(tq, period - 1, f.shape[-1])
        return x[:, :tq]

    return jnp.transpose(jnp.stack([toeplitz(f0), toeplitz(f1)]), (3, 0, 1, 2)) * LOG2E


def _sattn_body(pt_ref, q_ref, kn_ref, vn_ref, bnear_ref, bnew_ref, lam_ref, sw_ref,
                ck_hbm, cv_hbm, o_ref, kbuf, vbuf, sem, m_scr, l_scr, acc_scr, *, ch, n_chunks, n_seq):
    b = pl.program_id(0)
    c = pl.program_id(1)
    step = b * n_chunks + c
    slot = step % 2
    nr = q_ref.shape[0]
    t_new = nr // (2 * N_HEADS)

    def copies(bb, cc, sl):
        out = []
        for p in range(ch):
            pg = pt_ref[bb, cc * ch + p]
            out.append(pltpu.make_async_copy(ck_hbm.at[0, pg], kbuf.at[sl, p], sem.at[0, sl]))
            out.append(pltpu.make_async_copy(cv_hbm.at[0, pg], vbuf.at[sl, p], sem.at[1, sl]))
        return out

    @pl.when(step == 0)
    def _prime():
        for cp in copies(0, 0, 0):
            cp.start()

    @pl.when(step + 1 < n_seq * n_chunks)
    def _prefetch():
        nxt = step + 1
        for cp in copies(nxt // n_chunks, nxt % n_chunks, 1 - slot):
            cp.start()

    for cp in copies(b, c, slot):
        cp.wait()

    @pl.when(c == 0)
    def _init():
        m_scr[...] = jnp.full(m_scr.shape, -jnp.inf, F32)
        l_scr[...] = jnp.zeros(l_scr.shape, F32)
        acc_scr[...] = jnp.zeros(acc_scr.shape, F32)

    row = _iota((nr, MIX_HALF), 0)
    col = _iota((nr, MIX_HALF), 1)
    own = ((col // HEAD_DIM) == ((row % (nr // 2)) // t_new)) & (((col % HEAD_DIM) // MAP_DIM) == (row // (nr // 2)))
    own_head = (col // HEAD_DIM) == ((row % (nr // 2)) // t_new)
    qbd = jnp.where(own, q_ref[...], jnp.zeros((), BF16))

    is_last = c == n_chunks - 1
    rows_hd = N_HEADS * HEAD_DIM
    kt = jnp.concatenate([kbuf[slot, p].reshape(rows_hd, PAGE) for p in range(ch)], axis=-1).astype(BF16)
    s = _dot(qbd, kt)
    near = jnp.where(is_last, bnear_ref[...], 0.0)
    if ch > 1:
        s = jnp.concatenate([s[:, :(ch - 1) * PAGE], s[:, (ch - 1) * PAGE:] + near], axis=-1)
    else:
        s = s + near

    m_prev = m_scr[:, 0:1]
    l_prev = l_scr[:, 0:1]
    m_new = jnp.maximum(m_prev, jnp.max(s, axis=-1, keepdims=True))
    alpha = jnp.exp(m_prev - m_new)
    pr = jnp.exp(s - m_new)
    l_new = alpha * l_prev + jnp.sum(pr, axis=-1, keepdims=True)
    vt = jnp.concatenate([vbuf[slot, p].reshape(rows_hd, PAGE) for p in range(ch)], axis=-1).astype(BF16)
    acc = alpha * acc_scr[...] + _dot_nt(pr.astype(BF16), vt)
    m_scr[...] = jnp.broadcast_to(m_new, m_scr.shape)
    l_scr[...] = jnp.broadcast_to(l_new, l_scr.shape)
    acc_scr[...] = acc

    @pl.when(is_last)
    def _fin():
        sn = _dot_nt(qbd, kn_ref[...].astype(BF16)) + bnew_ref[...]
        m2 = jnp.maximum(m_new, jnp.max(sn, axis=-1, keepdims=True))
        a2 = jnp.exp(m_new - m2)
        p2 = jnp.exp(sn - m2)
        l2 = a2 * l_new + jnp.sum(p2, axis=-1, keepdims=True)
        acc2 = a2 * acc + _dot(p2.astype(BF16), vn_ref[...].astype(BF16))
        o_full = jnp.where(own_head, acc2 / l2, 0.0)
        fold = _one_hot_f((_iota((MIX_HALF, HEAD_DIM), 0) % HEAD_DIM) == _iota((MIX_HALF, HEAD_DIM), 1))
        o = _dot_x01(o_full, fold)
        half = nr // 2
        od = o[:half] - _lambda(lam_ref) * o[half:]
        o_ref[...] = _subln(od, sw_ref[...])


def _sample_attention(page_table, q_rep, k_new, v_new, cache_kt, cache_vt, bnear, bnew,
                      lam_vecs, subln_row, ch):
    n_seq, n_pages = page_table.shape
    assert n_pages % ch == 0
    n_chunks = n_pages // ch
    nr = q_rep.shape[1]
    t_pad = k_new.shape[1]
    body = functools.partial(_sattn_body, ch=ch, n_chunks=n_chunks, n_seq=n_seq)
    grid_spec = pltpu.PrefetchScalarGridSpec(
        num_scalar_prefetch=1,
        grid=(n_seq, n_chunks),
        in_specs=[
            pl.BlockSpec((None, nr, MIX_HALF), lambda b, c, pt: (b, 0, 0)),
            pl.BlockSpec((None, t_pad, MIX_HALF), lambda b, c, pt: (b, 0, 0)),
            pl.BlockSpec((None, t_pad, MIX_HALF), lambda b, c, pt: (b, 0, 0)),
            pl.BlockSpec((nr, PAGE), lambda b, c, pt: (0, 0)),
            pl.BlockSpec((nr, t_pad), lambda b, c, pt: (0, 0)),
            pl.BlockSpec((4, MAP_DIM), lambda b, c, pt: (0, 0)),
            pl.BlockSpec((1, HEAD_DIM), lambda b, c, pt: (0, 0)),
            pl.BlockSpec(memory_space=pl.ANY),
            pl.BlockSpec(memory_space=pl.ANY),
        ],
        out_specs=pl.BlockSpec((None, nr // 2, HEAD_DIM), lambda b, c, pt: (b, 0, 0)),
        scratch_shapes=[
            pltpu.VMEM((2, ch, N_HEADS, HEAD_DIM, PAGE), F32),
            pltpu.VMEM((2, ch, N_HEADS, HEAD_DIM, PAGE), F32),
            pltpu.SemaphoreType.DMA((2, 2)),
            pltpu.VMEM((nr, LANES), F32),
            pltpu.VMEM((nr, LANES), F32),
            pltpu.VMEM((nr, MIX_HALF), F32),
        ],
    )
    return pl.pallas_call(
        body,
        grid_spec=grid_spec,
        out_shape=jax.ShapeDtypeStruct((n_seq, nr // 2, HEAD_DIM), F32),
        compiler_params=_params(("arbitrary", "arbitrary")),
        name="sample_attn",
    )(page_table, q_rep, k_new, v_new, bnear, bnew, lam_vecs, subln_row, cache_kt, cache_vt)


def _sample_bias_tables(rel_bias, t_new, t_pad):
    H = N_HEADS
    nr = 2 * H * t_new
    row = jnp.arange(nr, dtype=jnp.int32)
    row_h = (row % (H * t_new)) // t_new
    row_t = row % t_new
    pick = (row_h[:, None] == jnp.arange(H, dtype=jnp.int32)[None, :])

    def table(rel):
        allh = _rel_bias_shifted(rel_bias, rel)
        return jnp.sum(jnp.where(pick[:, None, :], allh, 0.0), axis=-1)

    pos = jnp.arange(PAGE, dtype=jnp.int32)
    bnear = table(PAGE + row_t[:, None] - pos[None, :])
    tn = jnp.arange(t_pad, dtype=jnp.int32)
    reln = row_t[:, None] - tn[None, :]
    ok = (reln >= 0) & (tn[None, :] < t_new)
    bnew = jnp.where(ok, table(reln), -jnp.inf)
    return bnear.astype(F32), bnew.astype(F32)


def _outproj_body(x_ref, gdn_ref, diff_ref, wo_ref, nw_ref, wr_ref, x2_ref, hp_ref, route_ref):
    x2 = (x_ref[...] + _dot(gdn_ref[...], wo_ref[:MIX_HALF, :])
          + _dot(diff_ref[...], wo_ref[MIX_HALF:, :]))
    x2_ref[...] = x2
    h = x2 * lax.rsqrt(jnp.mean(x2 * x2, axis=-1, keepdims=True) + NORM_EPS) * nw_ref[...]
    h1, h2 = _split2(h)
    logits = _dot(h1, wr_ref[0]) + (_dot(h1, wr_ref[1]) + _dot(h2, wr_ref[0]))
    tm = logits.shape[0]
    lane = _iota((tm, LANES), 1)
    neg = -jnp.inf
    big = jnp.int32(10_000)
    gl = jnp.where(lane < N_ROUTE_GROUPS, logits, neg)
    gm = jnp.max(gl, axis=-1, keepdims=True)
    g_idx = jnp.min(jnp.where(gl == gm, lane, big), axis=-1, keepdims=True)
    g_w = 1.0 / jnp.sum(jnp.exp(gl - gm), axis=-1, keepdims=True)
    lo = N_ROUTE_GROUPS + EXPERTS_PER_GROUP * g_idx
    el = jnp.where((lane >= lo) & (lane < lo + EXPERTS_PER_GROUP), logits, neg)
    em = jnp.max(el, axis=-1, keepdims=True)
    i1 = jnp.min(jnp.where(el == em, lane, big), axis=-1, keepdims=True)
    el2 = jnp.where(lane == i1, neg, el)
    em2 = jnp.max(el2, axis=-1, keepdims=True)
    i2 = jnp.min(jnp.where(el2 == em2, lane, big), axis=-1, keepdims=True)
    e2 = jnp.exp(em2 - em)
    w1 = g_w / (1.0 + e2)
    w2 = g_w * e2 / (1.0 + e2)
    route = jnp.where(lane == 0, (i1 - N_ROUTE_GROUPS).astype(F32),
                      jnp.where(lane == 1, (i2 - N_ROUTE_GROUPS).astype(F32),
                                jnp.where(lane == 2, w1, jnp.where(lane == 3, w2, 0.0))))
    route_ref[...] = route
    hb = h.astype(BF16).astype(F32)
    lo_bits = pltpu.bitcast(hb[:, :D_MODEL // 2], jnp.uint32) >> 16
    hi_bits = pltpu.bitcast(hb[:, D_MODEL // 2:], jnp.uint32) & jnp.uint32(0xFFFF0000)
    hp_ref[...] = lo_bits | hi_bits


def _outproj(x2d, gdn2d, diff2d, w_out16, norm_w, wr_pair, tm):
    n = x2d.shape[0]
    assert n % tm == 0
    row = lambda i: (i, 0)
    fixed = lambda i: (0, 0)
    return pl.pallas_call(
        _outproj_body,
        grid=(n // tm,),
        in_specs=[
            pl.BlockSpec((tm, D_MODEL), row),
            pl.BlockSpec((tm, MIX_HALF), row),
            pl.BlockSpec((tm, MIX_HALF), row),
            pl.BlockSpec((D_MODEL, D_MODEL), fixed),
            pl.BlockSpec((1, D_MODEL), fixed),
            pl.BlockSpec((2, D_MODEL, LANES), lambda i: (0, 0, 0)),
        ],
        out_specs=(pl.BlockSpec((tm, D_MODEL), row),
                   pl.BlockSpec((tm, D_MODEL // 2), row),
                   pl.BlockSpec((tm, LANES), row)),
        out_shape=(jax.ShapeDtypeStruct((n, D_MODEL), F32),
                   jax.ShapeDtypeStruct((n, D_MODEL // 2), jnp.uint32),
                   jax.ShapeDtypeStruct((n, LANES), F32)),
        compiler_params=_params(("parallel",)),
        name="outproj_router",
    )(x2d, gdn2d, diff2d, w_out16, norm_w, wr_pair)


def _moe_plan(e_idx, w, tile, max_blocks):
    n = e_idx.shape[0]
    n_tiles = n // tile
    m = 2 * tile
    experts = jnp.arange(N_EXPERTS, dtype=jnp.int32)
    e = e_idx.reshape(n_tiles, m)
    hit = e[..., None] == experts
    onehot = hit.astype(jnp.int32)
    sub = math.gcd(m, MOE_BLOCK)
    tri = (jnp.arange(sub)[:, None] >= jnp.arange(sub)[None, :]).astype(BF16)
    within = jnp.einsum("ij,tcje->tcie", tri, hit.astype(BF16).reshape(n_tiles, m // sub, sub, N_EXPERTS),
                        preferred_element_type=F32)
    chunk_tot = within[:, :, -1, :]
    chunk_off = jnp.cumsum(chunk_tot, axis=1) - chunk_tot
    csum = (within + chunk_off[:, :, None, :]).reshape(n_tiles, m, N_EXPERTS).astype(jnp.int32)
    counts = csum[:, -1, :]
    nb_e = (counts + MOE_BLOCK - 1) // MOE_BLOCK
    pad_ends = jnp.cumsum(nb_e, axis=1)
    pad_starts = pad_ends - nb_e
    nblk = pad_ends[:, -1]
    dest = jnp.sum(onehot * (pad_starts[:, None, :] * MOE_BLOCK + csum - 1), axis=-1)
    blk = jnp.arange(max_blocks, dtype=jnp.int32)
    block_e = jnp.sum((pad_ends[:, None, :] <= blk[None, :, None]).astype(jnp.int32), axis=-1)
    block_e = jnp.minimum(block_e, N_EXPERTS - 1)
    oh_b = (block_e[..., None] == experts).astype(jnp.int32)
    cnt_b = jnp.sum(oh_b * counts[:, None, :], axis=-1)
    ps_b = jnp.sum(oh_b * pad_starts[:, None, :], axis=-1)
    live = blk[None, :] < nblk[:, None]
    nvalid = jnp.where(live, jnp.clip(cnt_b - (blk[None, :] - ps_b) * MOE_BLOCK, 0, MOE_BLOCK), 0)
    last_e = jnp.max(jnp.where(counts > 0, experts[None, :], 0), axis=-1, keepdims=True)
    block_e = jnp.where(live, block_e, last_e)
    return (block_e.reshape(-1).astype(jnp.int32), nblk.astype(jnp.int32), nvalid.reshape(-1).astype(jnp.int32),
            dest.astype(jnp.int32).reshape(n_tiles, 1, m), w.astype(F32).reshape(n_tiles, 1, m))


_SCATTER_BATCH = 4
_INVERT_BATCH = 8


def _moe_body(be_ref, nblk_ref, nvalid_ref, hp_ref, dest_ref, wa_ref, wg_ref, wu_ref, wd_ref,
              y_ref, asg_s, xg_scr, yb_scr, *, max_blocks):
    t = pl.program_id(0)
    b = pl.program_id(1)
    m = dest_ref.shape[1]

    @pl.when(b == 0)
    def _start_tile():
        y_ref[...] = jnp.zeros(y_ref.shape, F32)

        def invert(ab, carry):
            a0 = ab * _INVERT_BATCH
            dests = [dest_ref[0, a0 + u] for u in range(_INVERT_BATCH)]
            for u in range(_INVERT_BATCH):
                asg_s[dests[u]] = a0 + u
            return carry

        lax.fori_loop(0, m // _INVERT_BATCH, invert, 0)

    @pl.when(b < nblk_ref[t])
    def _block():
        base = b * MOE_BLOCK
        nv = nvalid_ref[t * max_blocks + b]

        def gather(i, carry):
            tok = asg_s[base + jnp.minimum(i, nv - 1)] >> 1
            xg_scr[pl.ds(i, 1), :] = hp_ref[pl.ds(tok, 1), :]
            return carry

        lax.fori_loop(0, MOE_BLOCK, gather, 0, unroll=8)
        xp = xg_scr[...]
        half = D_MODEL // 2
        x_lo = pltpu.bitcast(xp << 16, F32).astype(BF16)
        x_hi = pltpu.bitcast(xp & jnp.uint32(0xFFFF0000), F32).astype(BF16)
        gt = _dot(x_lo, wg_ref[:half, :]) + _dot(x_hi, wg_ref[half:, :])
        up = _dot(x_lo, wu_ref[:half, :]) + _dot(x_hi, wu_ref[half:, :])
        act = (jax.nn.silu(gt) * up).astype(BF16)
        yb_scr[...] = _dot(act, wd_ref[...])

        def scatter_batch(ib, carry):
            i0 = ib * _SCATTER_BATCH
            asgs = [asg_s[base + i0 + u] for u in range(_SCATTER_BATCH)]
            toks = [a >> 1 for a in asgs]
            vals = [y_ref[pl.ds(toks[u], 1), :] + wa_ref[0, asgs[u]] * yb_scr[pl.ds(i0 + u, 1), :]
                    for u in range(_SCATTER_BATCH)]
            for u in range(_SCATTER_BATCH):
                y_ref[pl.ds(toks[u], 1), :] = vals[u]
            return carry

        nfull = nv // _SCATTER_BATCH
        lax.fori_loop(0, nfull, scatter_batch, 0)

        def scatter_one(i, carry):
            a = asg_s[base + i]
            tok = a >> 1
            y_ref[pl.ds(tok, 1), :] = y_ref[pl.ds(tok, 1), :] + wa_ref[0, a] * yb_scr[pl.ds(i, 1), :]
            return carry

        lax.fori_loop(nfull * _SCATTER_BATCH, nv, scatter_one, 0)


def _moe(hp, plan, wg16, wu16, wd16, tile, max_blocks):
    n = hp.shape[0]
    n_tiles = n // tile
    block_e, nblk, nvalid, dest, wa = plan
    m = 2 * tile
    body = functools.partial(_moe_body, max_blocks=max_blocks)
    half = D_MODEL // 2
    wmap = lambda t, b, be, nb, nv: (be[t * max_blocks + b], 0, 0)
    tmap = lambda t, b, be, nb, nv: (t, 0, 0)
    grid_spec = pltpu.PrefetchScalarGridSpec(
        num_scalar_prefetch=3,
        grid=(n_tiles, max_blocks),
        in_specs=[
            pl.BlockSpec((tile, half), lambda t, b, be, nb, nv: (t, 0)),
            pl.BlockSpec((None, 1, m), tmap, memory_space=pltpu.SMEM),
            pl.BlockSpec((None, 1, m), tmap, memory_space=pltpu.SMEM),
            pl.BlockSpec((None, D_MODEL, EXPERT_FF), wmap),
            pl.BlockSpec((None, D_MODEL, EXPERT_FF), wmap),
            pl.BlockSpec((None, EXPERT_FF, D_MODEL), wmap),
        ],
        out_specs=pl.BlockSpec((tile, D_MODEL), lambda t, b, be, nb, nv: (t, 0)),
        scratch_shapes=[pltpu.SMEM((max_blocks * MOE_BLOCK,), jnp.int32),
                        pltpu.VMEM((MOE_BLOCK, half), jnp.uint32),
                        pltpu.VMEM((MOE_BLOCK, D_MODEL), F32)],
    )
    return pl.pallas_call(
        body,
        grid_spec=grid_spec,
        out_shape=jax.ShapeDtypeStruct((n, D_MODEL), F32),
        compiler_params=_params(("arbitrary", "arbitrary")),
        name="moe",
    )(block_e, nblk, nvalid, hp, dest, wa, wg16, wu16, wd16)


def _final_body(x_ref, y_ref, nw_ref, o_ref):
    x = x_ref[...] + y_ref[...]
    o_ref[...] = x * lax.rsqrt(jnp.mean(x * x, axis=-1, keepdims=True) + NORM_EPS) * nw_ref[...]


def _final_norm(x2, y, norm_w, tm):
    n = x2.shape[0]
    row = lambda i: (i, 0)
    return pl.pallas_call(
        _final_body,
        grid=(n // tm,),
        in_specs=[pl.BlockSpec((tm, D_MODEL), row), pl.BlockSpec((tm, D_MODEL), row),
                  pl.BlockSpec((1, D_MODEL), lambda i: (0, 0))],
        out_specs=pl.BlockSpec((tm, D_MODEL), row),
        out_shape=jax.ShapeDtypeStruct((n, D_MODEL), F32),
        compiler_params=_params(("parallel",)),
        name="final_norm",
    )(x2, y, norm_w)


def _pick_tile(n, pref):
    t = min(n, pref)
    while n % t:
        t //= 2
    return t


def _ffn(x2d, gdn2d, diff2d, w_out16, norm_ffn_row, wr_pair, wg16, wu16, wd16, norm_final_row):
    n = x2d.shape[0]
    tm = _pick_tile(n, 512)
    x2, hp, route = _outproj(x2d, gdn2d, diff2d, w_out16, norm_ffn_row, wr_pair, tm)
    e_idx = route[:, 0:2].astype(jnp.int32)
    w = route[:, 2:4]
    tile = _pick_tile(n, 2048)
    max_blocks = 2 * tile // MOE_BLOCK + N_EXPERTS
    plan = _moe_plan(e_idx, w, tile, max_blocks)
    y = _moe(hp, plan, wg16, wu16, wd16, tile, max_blocks)
    return _final_norm(x2, y, norm_final_row, tm)


def kernel(x_prompt, x_sample, cache_k, cache_v, state_conv, state_gdn, page_table, norm_mix, w_in, conv_w, gdn_a_log, gdn_dt_bias, gdn_norm, lambda_q1, lambda_k1, lambda_q2, lambda_k2, diff_subln, rel_bias, w_out, norm_ffn, router_group, router_expert, w_gate, w_up, w_down, norm_final):
    bsz, seq, _ = x_prompt.shape
    dbsz, dseq, _ = x_sample.shape
    H = N_HEADS
    l = 0

    wl = w_in[l]
    n_ab = 2 * H
    w_packed = jnp.concatenate(
        [wl[:, :_C_GATE[1]], wl[:, _C_GATE[1]:_C_GATE[1] + n_ab],
         jnp.zeros((D_MODEL, LANES - n_ab), wl.dtype), wl[:, _C_GATE[1] + n_ab:]], axis=1).astype(BF16)
    norm_mix_row = norm_mix[l].reshape(1, D_MODEL)
    pad_h = jnp.zeros((LANES - H,), F32)
    alog_row = jnp.concatenate([gdn_a_log[l].astype(F32), pad_h]).reshape(1, LANES)
    dtb_row = jnp.concatenate([gdn_dt_bias[l].astype(F32), pad_h]).reshape(1, LANES)
    gdn_norm_row = jnp.tile(gdn_norm[l].astype(F32), HEADS_PER_GROUP).reshape(1, GROUP_W)
    lam_vecs = jnp.stack([lambda_q1[l], lambda_k1[l], lambda_q2[l], lambda_k2[l]]).astype(F32)
    subln_row = diff_subln[l].astype(F32).reshape(1, HEAD_DIM)
    w_out16 = w_out[l].astype(BF16)
    norm_ffn_row = norm_ffn[l].reshape(1, D_MODEL)
    wr = jnp.concatenate([router_group[l], router_expert[l].reshape(D_MODEL, N_EXPERTS),
                          jnp.zeros((D_MODEL, LANES - N_ROUTE_GROUPS - N_EXPERTS), F32)], axis=1)
    wr1 = wr.astype(BF16)
    wr_pair = jnp.stack([wr1, (wr - wr1.astype(F32)).astype(BF16)])
    wg16 = w_gate[l].astype(BF16)
    wu16 = w_up[l].astype(BF16)
    wd16 = w_down[l].astype(BF16)
    norm_final_row = norm_final.reshape(1, D_MODEL)

    n_p = bsz * seq
    xp2d = x_prompt.reshape(n_p, D_MODEL)
    zqkv, zgate, zab, kf, vf, _, qh, kh, vh = _inproj(xp2d, norm_mix_row, w_packed, _pick_tile(n_p, 256))
    conv_zero = jnp.zeros((bsz, SUBLANES, CONV_DIM), F32)
    gdn_p, s_p = _gdn(zqkv.reshape(bsz, seq, CONV_DIM), zgate.reshape(bsz, seq, MIX_HALF),
                      zab.reshape(bsz, seq, LANES), conv_w[l], alog_row, dtb_row, gdn_norm_row,
                      conv_zero, None, seq)
    tq = 256
    bias_tab = _prompt_bias_tables(rel_bias, tq)
    hm = lambda a: a.reshape(H, bsz, seq, a.shape[-1])
    diff_p = _prompt_attention(hm(qh), hm(kh), hm(vh), bias_tab, lam_vecs, subln_row, tq)
    y_prompt = _ffn(xp2d, gdn_p.reshape(n_p, MIX_HALF), diff_p.reshape(n_p, MIX_HALF), w_out16,
                    norm_ffn_row, wr_pair, wg16, wu16, wd16, norm_final_row).reshape(bsz, seq, D_MODEL)
    new_k_prompt = kf.reshape(1, bsz, seq, H, HEAD_DIM)
    new_v_prompt = vf.reshape(1, bsz, seq, H, HEAD_DIM)
    new_conv_prompt = zqkv.reshape(bsz, seq, CONV_DIM)[:, seq - (CONV_W - 1):, :][None]
    new_gdn_prompt = s_p[None]

    n_s = dbsz * dseq
    xs2d = x_sample.reshape(n_s, D_MODEL)
    zqkv_s, zgate_s, zab_s, kf_s, vf_s, qf_s, _, _, _ = _inproj(xs2d, norm_mix_row, w_packed, _pick_tile(n_s, 256))
    C = GDN_CHUNK
    padt = lambda a: jnp.pad(a.reshape(dbsz, dseq, -1), ((0, 0), (0, C - dseq), (0, 0)))
    conv_init = jnp.pad(state_conv[l].astype(F32), ((0, 0), (SUBLANES - (CONV_W - 1), 0), (0, 0)))
    gdn_s, s_s = _gdn(padt(zqkv_s), padt(zgate_s), padt(zab_s), conv_w[l], alog_row, dtb_row,
                      gdn_norm_row, conv_init, state_gdn[l].astype(F32), dseq)
    gdn_s = gdn_s[:, :dseq, :].reshape(n_s, MIX_HALF)
    t_pad = BF16_ROWS
    q_rep = jnp.broadcast_to(qf_s.reshape(dbsz, 1, 1, dseq, MIX_HALF),
                             (dbsz, 2, H, dseq, MIX_HALF)).reshape(dbsz, 2 * H * dseq, MIX_HALF)
    newpad = lambda a: jnp.pad(a.reshape(dbsz, dseq, MIX_HALF), ((0, 0), (0, t_pad - dseq), (0, 0)))
    bnear, bnew = _sample_bias_tables(rel_bias, dseq, t_pad)
    cache_kt = jnp.transpose(cache_k, (0, 1, 3, 4, 2))
    cache_vt = jnp.transpose(cache_v, (0, 1, 3, 4, 2))
    n_pages = page_table.shape[1]
    ch = 16 if n_pages % 16 == 0 else 1
    diff_s = _sample_attention(page_table, q_rep, newpad(kf_s), newpad(vf_s), cache_kt, cache_vt,
                               bnear, bnew, lam_vecs, subln_row, ch)
    diff_s = jnp.transpose(diff_s.reshape(dbsz, H, dseq, HEAD_DIM), (0, 2, 1, 3)).reshape(n_s, MIX_HALF)
    y_sample = _ffn(xs2d, gdn_s, diff_s.astype(BF16), w_out16, norm_ffn_row, wr_pair, wg16, wu16, wd16,
                    norm_final_row).reshape(dbsz, dseq, D_MODEL)
    new_k_sample = kf_s.reshape(1, dbsz, dseq, H, HEAD_DIM)
    new_v_sample = vf_s.reshape(1, dbsz, dseq, H, HEAD_DIM)
    zq3 = zqkv_s.reshape(dbsz, dseq, CONV_DIM)
    new_conv_sample = jnp.concatenate([state_conv[l].astype(F32), zq3], axis=1)[:, -(CONV_W - 1):, :][None]
    new_gdn_sample = s_s[None]

    return (y_prompt, y_sample, new_k_prompt, new_v_prompt, new_k_sample, new_v_sample,
            new_conv_prompt, new_conv_sample, new_gdn_prompt, new_gdn_sample)
```

```python
import functools
import math

import jax
import jax.numpy as jnp
from jax import lax
from jax.experimental import pallas as pl
from jax.experimental.pallas import tpu as pltpu

F32 = jnp.float32
BF16 = jnp.bfloat16

D_MODEL = 1024
N_HEADS = 8
HEAD_DIM = 64
MAP_DIM = 32
MIX_HALF = N_HEADS * HEAD_DIM
CONV_W = 4
CONV_DIM = 3 * MIX_HALF
GDN_CHUNK = 64
GDN_CHUNK_SAMPLE = 16
HEADS_PER_GROUP = 4
GROUP_W = HEADS_PER_GROUP * HEAD_DIM
N_GROUPS_GDN = N_HEADS // HEADS_PER_GROUP
PAGE = 128
N_BUCKETS = 32
MAX_DISTANCE = 128
N_ROUTE_GROUPS = 4
EXPERTS_PER_GROUP = 8
N_EXPERTS = N_ROUTE_GROUPS * EXPERTS_PER_GROUP
EXPERT_FF = D_MODEL // 2
MOE_BLOCK = 128
NORM_EPS = 1e-6
SUBLN_EPS = 1e-5
LAMBDA_INIT = 0.8 - 0.6 * math.exp(-0.3 * 0)
LOG2E = math.log2(math.e)

LANES = 128
SUBLANES = 8
BF16_ROWS = 16
VMEM_LIMIT = 48 * 1024 * 1024

_C_QKV = (0, CONV_DIM)
_C_GATE = (_C_QKV[1], _C_QKV[1] + MIX_HALF)
_C_AB = (_C_GATE[1], _C_GATE[1] + LANES)
_C_DQ = (_C_AB[1], _C_AB[1] + MIX_HALF)
_C_DK = (_C_DQ[1], _C_DQ[1] + MIX_HALF)
_C_DV = (_C_DK[1], _C_DK[1] + MIX_HALF)
IN_COLS_PACKED = _C_DV[1]


def _params(sem, vmem=VMEM_LIMIT):
    return pltpu.CompilerParams(dimension_semantics=sem, vmem_limit_bytes=vmem)


def _dot(a, b):
    return jnp.dot(a, b, preferred_element_type=F32)


def _dot_nt(a, b):
    return lax.dot_general(a, b, (((1,), (1,)), ((), ())), preferred_element_type=F32)


def _dot_tn(a, b):
    return lax.dot_general(a, b, (((0,), (0,)), ((), ())), preferred_element_type=F32)


def _split2(x):
    x1 = x.astype(BF16)
    x2 = (x - x1.astype(F32)).astype(BF16)
    return x1, x2


def _split3(x):
    x1 = x.astype(BF16)
    r = x - x1.astype(F32)
    x2 = r.astype(BF16)
    x3 = (r - x2.astype(F32)).astype(BF16)
    return x1, x2, x3


def _dot_x01(x, m01):
    x1, x2, x3 = _split3(x)
    return _dot(x1, m01) + (_dot(x2, m01) + _dot(x3, m01))


def _dot_01x(m01, x):
    x1, x2, x3 = _split3(x)
    return _dot(m01, x1) + (_dot(m01, x2) + _dot(m01, x3))


def _iota(shape, dim):
    return lax.broadcasted_iota(jnp.int32, shape, dim)


def _one_hot_f(cond):
    return jnp.where(cond, 1.0, 0.0).astype(BF16)


def _inproj_body(x_ref, nw_ref, w_ref, zqkv_ref, zgate_ref, zab_ref, kf_ref, vf_ref,
                 qf_ref, qh_ref, kh_ref, vh_ref):
    x = x_ref[...]
    h = x * lax.rsqrt(jnp.mean(x * x, axis=-1, keepdims=True) + NORM_EPS) * nw_ref[...]
    hb = h.astype(BF16)

    def proj(cols):
        return _dot(hb, w_ref[:, cols[0]:cols[1]])

    zqkv_ref[...] = proj(_C_QKV)
    zgate_ref[...] = proj(_C_GATE)
    zab_ref[...] = proj(_C_AB)
    zq = proj(_C_DQ) * (MAP_DIM ** -0.5)
    zk = proj(_C_DK)
    zv = proj(_C_DV)
    kf_ref[...] = zk
    vf_ref[...] = zv
    qf_ref[...] = zq.astype(BF16)
    tm = x.shape[0]
    zq2 = zq * LOG2E
    zeros = jnp.zeros((tm, HEAD_DIM), BF16)
    ones = jnp.ones((tm, HEAD_DIM), BF16)
    two_ones = jnp.where(_iota((tm, HEAD_DIM), 1) < 2, 1.0, 0.0).astype(BF16)
    for h_i in range(N_HEADS):
        sl = slice(HEAD_DIM * h_i, HEAD_DIM * (h_i + 1))
        qh_ref[h_i] = jnp.concatenate([zq2[:, sl].astype(BF16), zeros], axis=-1)
        kh_ref[h_i] = jnp.concatenate([zk[:, sl].astype(BF16), two_ones], axis=-1)
        vh_ref[h_i] = jnp.concatenate([zv[:, sl].astype(BF16), ones], axis=-1)


def _inproj(x2d, norm_w, w_packed, tm):
    n = x2d.shape[0]
    assert n % tm == 0
    row = lambda i: (i, 0)
    fixed = lambda i: (0, 0)
    hm = lambda i: (0, i, 0)
    out_shape = (
        jax.ShapeDtypeStruct((n, CONV_DIM), F32),
        jax.ShapeDtypeStruct((n, MIX_HALF), F32),
        jax.ShapeDtypeStruct((n, LANES), F32),
        jax.ShapeDtypeStruct((n, MIX_HALF), F32),
        jax.ShapeDtypeStruct((n, MIX_HALF), F32),
        jax.ShapeDtypeStruct((n, MIX_HALF), BF16),
        jax.ShapeDtypeStruct((N_HEADS, n, LANES), BF16),
        jax.ShapeDtypeStruct((N_HEADS, n, LANES), BF16),
        jax.ShapeDtypeStruct((N_HEADS, n, LANES), BF16),
    )
    out_specs = (
        pl.BlockSpec((tm, CONV_DIM), row),
        pl.BlockSpec((tm, MIX_HALF), row),
        pl.BlockSpec((tm, LANES), row),
        pl.BlockSpec((tm, MIX_HALF), row),
        pl.BlockSpec((tm, MIX_HALF), row),
        pl.BlockSpec((tm, MIX_HALF), row),
        pl.BlockSpec((N_HEADS, tm, LANES), hm),
        pl.BlockSpec((N_HEADS, tm, LANES), hm),
        pl.BlockSpec((N_HEADS, tm, LANES), hm),
    )
    return pl.pallas_call(
        _inproj_body,
        grid=(n // tm,),
        in_specs=[pl.BlockSpec((tm, D_MODEL), row),
                  pl.BlockSpec((1, D_MODEL), fixed),
                  pl.BlockSpec((D_MODEL, IN_COLS_PACKED), fixed)],
        out_specs=out_specs,
        out_shape=out_shape,
        compiler_params=_params(("parallel",)),
        name="inproj",
    )(x2d, norm_w, w_packed)


def _gdn_body(z_ref, zg_ref, zab_ref, cw_ref, alog_ref, dtb_ref, nw_ref, ci_ref, s0_ref,
              o_ref, sfin_ref, s_scr, prev_scr, *, t_valid, zero_init, nb, chunk):
    C = chunk
    W = GROUP_W
    WC = HEADS_PER_GROUP * C
    c = pl.program_id(1)
    n_chunks = pl.num_programs(1)

    rw = _iota((W, W), 0)
    cw = _iota((W, W), 1)
    blk = (rw // HEAD_DIM) == (cw // HEAD_DIM)
    ones_blk = _one_hot_f(blk)
    mask_kd = (_iota((WC, W), 0) // C) == (_iota((WC, W), 1) // HEAD_DIM)
    mask_cc = (_iota((WC, WC), 0) // C) == (_iota((WC, WC), 1) // C)
    ri = _iota((C, WC), 0)
    cj = _iota((C, WC), 1) % C
    incl = ri >= cj
    strict = ri > cj
    eye_all = ri == cj
    eye_f = jnp.where(eye_all, 1.0, 0.0)
    zero16 = jnp.zeros((), BF16)

    @pl.when(c == 0)
    def _init():
        prev_scr[...] = ci_ref[...]
        for sg in range(nb * N_GROUPS_GDN):
            bb, g = divmod(sg, N_GROUPS_GDN)
            if zero_init:
                s_scr[sg] = jnp.zeros((W, W), F32)
            else:
                s0 = s0_ref[bb, g * HEADS_PER_GROUP:(g + 1) * HEADS_PER_GROUP].reshape(W, HEAD_DIM)
                et = _one_hot_f(_iota((HEAD_DIM, W), 0) == (_iota((HEAD_DIM, W), 1) % HEAD_DIM))
                s_scr[sg] = jnp.where(blk, _dot_x01(s0, et), 0.0)

    tri = _one_hot_f(_iota((C, C), 0) >= _iota((C, C), 1))

    def blocksum(v):
        return _dot(v.astype(BF16), ones_blk)

    def bd(x16):
        return jnp.where(mask_kd, jnp.concatenate([x16] * HEADS_PER_GROUP, axis=0), zero16)

    def bd_cc(x16):
        return jnp.where(mask_cc, jnp.concatenate([x16] * HEADS_PER_GROUP, axis=0), zero16)

    seqs = range(nb)
    chains = [(bb, g) for bb in seqs for g in range(N_GROUPS_GDN)]

    def each(fn, *lists):
        return [fn(*args) for args in zip(*lists)]

    w = cw_ref[...]
    xc = [jnp.concatenate([prev_scr[bb], z_ref[bb]], axis=0) for bb in seqs]
    y = each(lambda a: (a * w[3:4] + pltpu.roll(a, 1, 0) * w[2:3]
                        + pltpu.roll(a, 2, 0) * w[1:2] + pltpu.roll(a, 3, 0) * w[0:1]), xc)
    for bb in seqs:
        prev_scr[bb] = xc[bb][C:C + SUBLANES]
    qkv = each(lambda a: jax.nn.silu(a[SUBLANES:SUBLANES + C]), y)

    valid = (_iota((C, LANES), 0) + c * C) < t_valid
    zab = [zab_ref[bb] for bb in seqs]
    neg_a = -jnp.exp(alog_ref[...])
    gfull = each(lambda a: jnp.where(valid, neg_a * jax.nn.softplus(a + dtb_ref[...]), 0.0), zab)
    bfull = each(lambda a: jnp.where(valid, jax.nn.sigmoid(a), 0.0), zab)
    gcum_s = each(lambda a: _dot_01x(tri, a), gfull)

    lane_r = _iota((LANES, W), 0)
    head_c = _iota((LANES, W), 1) // HEAD_DIM
    e_g = [_one_hot_f(lane_r == head_c + g * HEADS_PER_GROUP) for g in range(N_GROUPS_GDN)]
    e_b = [_one_hot_f(lane_r == head_c + g * HEADS_PER_GROUP + N_HEADS) for g in range(N_GROUPS_GDN)]

    def cols(part, g):
        return slice(part * MIX_HALF + g * W, part * MIX_HALF + (g + 1) * W)

    q = [qkv[bb][:, cols(0, g)] for bb, g in chains]
    k = [qkv[bb][:, cols(1, g)] for bb, g in chains]
    v = [qkv[bb][:, cols(2, g)] for bb, g in chains]
    q = each(lambda a: a * lax.rsqrt(blocksum(a * a) + 1e-6) * (HEAD_DIM ** -0.5), q)
    k = each(lambda a: a * lax.rsqrt(blocksum(a * a) + 1e-6), k)
    gexp = [_dot_x01(gcum_s[bb], e_g[g]) for bb, g in chains]
    bexp = [_dot_x01(bfull[bb], e_b[g]) for bb, g in chains]
    if WC == W:
        gexp_c = gexp
    else:
        lane_rc = _iota((LANES, WC), 0)
        head_cc = _iota((LANES, WC), 1) // C
        e_gc = [_one_hot_f(lane_rc == head_cc + g * HEADS_PER_GROUP) for g in range(N_GROUPS_GDN)]
        gexp_c = [_dot_x01(gcum_s[bb], e_gc[g]) for bb, g in chains]
    grow = each(lambda a: jnp.sum(jnp.where(eye_all, a, 0.0), axis=0, keepdims=True), gexp_c)
    decay = each(lambda a, r: jnp.where(incl, jnp.exp(jnp.where(incl, a - r, 0.0)), 0.0), gexp_c, grow)

    kb = each(lambda a, b: a * b, k, bexp)
    kkqk = each(lambda a, b, kk: _dot_nt(jnp.concatenate([a, b], axis=0).astype(BF16), bd(kk.astype(BF16))),
                kb, q, k)
    lmat = each(lambda a, d: jnp.where(strict, a[:C] * d, 0.0), kkqk, decay)
    qkd = each(lambda a, d: a[C:] * d, kkqk, decay)

    tinv = each(lambda a: eye_f - a, lmat)
    p16 = each(lambda a: a.astype(BF16), lmat)
    bp = each(bd_cc, p16)
    for _ in range(int(math.log2(C)) - 1):
        p16 = each(lambda a, b: _dot(a, b).astype(BF16), p16, bp)
        bp = each(bd_cc, p16)
        tinv = each(lambda t, b: t + _dot(t.astype(BF16), b), tinv, bp)

    eg = each(jnp.exp, gexp)
    vb = each(lambda a, b: a * b, v, bexp)
    kbg = each(lambda a, b: a * b, kb, eg)
    corr = each(lambda t, a, b: _dot((t - eye_f).astype(BF16),
                                     jnp.concatenate([bd(a.astype(BF16)), bd(b.astype(BF16))], axis=1)),
                tinv, vb, kbg)
    u = each(lambda a, cr: a + cr[:, :W], vb, corr)
    wm = each(lambda a, cr: a + cr[:, W:], kbg, corr)
    s = [s_scr[bb * N_GROUPS_GDN + g] for bb, g in chains]
    s16 = each(lambda a: a.astype(BF16), s)
    v_new = each(lambda a, b, st: a - _dot(b.astype(BF16), st), u, wm, s16)
    o = each(lambda a, e, st, d, vn: _dot((a * e).astype(BF16), st) + _dot(d.astype(BF16), bd(vn.astype(BF16))),
             q, eg, s16, qkd, v_new)
    glast = each(lambda a: a[C - 1:C, :], gexp)
    kd = each(lambda a, gl, ge: a * jnp.exp(gl - ge), k, glast, gexp)
    upd = each(lambda a, vn: _dot_tn(a.astype(BF16), vn.astype(BF16)), kd, v_new)
    s_new = each(lambda st, gl, up: st * jnp.exp(gl) + jnp.where(blk, up, 0.0), s, glast, upd)
    for (bb, g), st in zip(chains, s_new):
        s_scr[bb * N_GROUPS_GDN + g] = st

    on = each(lambda a: a * lax.rsqrt(blocksum(a * a) * (1.0 / HEAD_DIM) + NORM_EPS) * nw_ref[...], o)
    for (bb, g), a in zip(chains, on):
        o_ref[bb, :, cols(0, g)] = (a * jax.nn.silu(zg_ref[bb, :, cols(0, g)])).astype(o_ref.dtype)

    @pl.when(c == n_chunks - 1)
    def _fin():
        ett = _one_hot_f((_iota((W, HEAD_DIM), 0) % HEAD_DIM) == _iota((W, HEAD_DIM), 1))
        for sg in range(nb * N_GROUPS_GDN):
            bb, g = divmod(sg, N_GROUPS_GDN)
            sout = _dot_x01(s_scr[sg], ett)
            sfin_ref[bb, g * HEADS_PER_GROUP:(g + 1) * HEADS_PER_GROUP] = sout.reshape(
                HEADS_PER_GROUP, HEAD_DIM, HEAD_DIM)


def _gdn(zqkv, zgate, zab, conv_w, alog_row, dtb_row, norm_row, conv_init, s0, t_valid, C):
    bsz, t, _ = zqkv.shape
    assert t % C == 0 and C % BF16_ROWS == 0
    n_chunks = t // C
    nb = 2 if bsz % 2 == 0 else 1
    zero_init = s0 is None
    if zero_init:
        s0 = jnp.zeros((nb, N_HEADS, HEAD_DIM, HEAD_DIM), F32)
        s0_map = lambda b, c: (0, 0, 0, 0)
    else:
        s0_map = lambda b, c: (b, 0, 0, 0)
    chunk = lambda b, c: (b, c, 0)
    fixed = lambda b, c: (0, 0)
    in_specs = [
        pl.BlockSpec((nb, C, CONV_DIM), chunk),
        pl.BlockSpec((nb, C, MIX_HALF), chunk),
        pl.BlockSpec((nb, C, LANES), chunk),
        pl.BlockSpec((CONV_W, CONV_DIM), fixed),
        pl.BlockSpec((1, LANES), fixed),
        pl.BlockSpec((1, LANES), fixed),
        pl.BlockSpec((1, GROUP_W), fixed),
        pl.BlockSpec((nb, SUBLANES, CONV_DIM), lambda b, c: (b, 0, 0)),
        pl.BlockSpec((nb, N_HEADS, HEAD_DIM, HEAD_DIM), s0_map),
    ]
    out_specs = (
        pl.BlockSpec((nb, C, MIX_HALF), chunk),
        pl.BlockSpec((nb, N_HEADS, HEAD_DIM, HEAD_DIM), lambda b, c: (b, 0, 0, 0)),
    )
    out_shape = (
        jax.ShapeDtypeStruct((bsz, t, MIX_HALF), BF16),
        jax.ShapeDtypeStruct((bsz, N_HEADS, HEAD_DIM, HEAD_DIM), F32),
    )
    body = functools.partial(_gdn_body, t_valid=t_valid, zero_init=zero_init, nb=nb, chunk=C)
    return pl.pallas_call(
        body,
        grid=(bsz // nb, n_chunks),
        in_specs=in_specs,
        out_specs=out_specs,
        out_shape=out_shape,
        scratch_shapes=[pltpu.VMEM((nb * N_GROUPS_GDN, GROUP_W, GROUP_W), F32),
                        pltpu.VMEM((nb, SUBLANES, CONV_DIM), F32)],
        compiler_params=_params(("parallel", "arbitrary")),
        name="gdn",
    )(zqkv, zgate, zab, conv_w, alog_row, dtb_row, norm_row, conv_init, s0)


def _t5_bucket(rel):
    n = jnp.maximum(rel, 0)
    max_exact = N_BUCKETS // 2
    large = max_exact + (jnp.log(jnp.maximum(n, 1).astype(F32) / max_exact)
                         / math.log(MAX_DISTANCE / max_exact) * (N_BUCKETS - max_exact)).astype(jnp.int32)
    large = jnp.minimum(large, N_BUCKETS - 1)
    return jnp.where(n < max_exact, n, large)


def _rel_bias_shifted(rel_bias, rel):
    rb = rel_bias.astype(F32)
    sel = _t5_bucket(rel)[..., None] == jnp.arange(N_BUCKETS, dtype=jnp.int32)
    picked = jnp.sum(jnp.where(sel[..., None], rb, 0.0), axis=-2)
    return picked - rb[N_BUCKETS - 1]


def _lambda(lam_ref):
    lam = lam_ref[...]
    a = jnp.exp(jnp.sum(lam[0:1] * lam[1:2], axis=-1, keepdims=True))
    b = jnp.exp(jnp.sum(lam[2:3] * lam[3:4], axis=-1, keepdims=True))
    return a - b + LAMBDA_INIT


def _subln(o, w_row):
    ms = jnp.mean(o * o, axis=-1, keepdims=True)
    return o * lax.rsqrt(ms + SUBLN_EPS) * w_row * (1.0 - LAMBDA_INIT)


def _loop_pairs(n, body):
    def two(i, carry):
        body(2 * i)
        body(2 * i + 1)
        return carry

    def one(j, carry):
        body(j)
        return carry

    lax.fori_loop(0, n // 2, two, 0)
    lax.fori_loop((n // 2) * 2, n, one, 0)


ATTN_HEADS_PER_STEP = 8


def _pattn_body(q_ref, k_ref, v_ref, bias_ref, lam_ref, sw_ref, o_ref, o_scr, mv_scr, acc_scr, *, tq):
    qi = pl.program_id(1)
    hp = pl.program_id(2)
    tk = tq
    heads = range(ATTN_HEADS_PER_STEP)
    prev = jnp.maximum(qi - 1, 0)
    prev_off = jnp.where(qi > 0, 0.0, -jnp.inf)
    zero = jnp.zeros((), BF16)

    def fold(s):
        out = s[:, :LANES]
        for cidx in range(1, tk // LANES):
            out = jnp.maximum(out, s[:, cidx * LANES:(cidx + 1) * LANES])
        return out

    def k_tile(u, j):
        return k_ref[u, pl.ds(pl.multiple_of(j * tk, tk), tk), :]

    def v_tile(u, j):
        return v_ref[u, pl.ds(pl.multiple_of(j * tk, tk), tk), :]

    def both(x):
        return jnp.concatenate([x, x], axis=0)

    lane = _iota((tq, LANES), 1)
    chains = [(u, mp) for u in heads for mp in range(2)]
    map_lanes = [lane < MAP_DIM, lane >= MAP_DIM]
    q = [q_ref[u] for u in heads]
    qm = [jnp.where(map_lanes[mp], q[u], zero) for u, mp in chains]
    b0 = [bias_ref[u, 0] for u in heads]
    b1 = [bias_ref[u, 1] + prev_off for u in heads]

    s_diag = [_dot_nt(qm[c], k_tile(u, qi)) for c, (u, _) in enumerate(chains)]
    s_prev = [_dot_nt(qm[c], k_tile(u, prev)) for c, (u, _) in enumerate(chains)]
    for c, (u, _) in enumerate(chains):
        mv_scr[c] = jnp.maximum(fold(s_diag[c] + b0[u]), fold(s_prev[c] + b1[u]))

    def max_tile(j):
        ss = [_dot_nt(qm[c], k_tile(u, j)) for c, (u, _) in enumerate(chains)]
        for c in range(len(chains)):
            mv_scr[c] = jnp.maximum(mv_scr[c], fold(ss[c]))

    _loop_pairs(prev, max_tile)
    m = [jnp.max(mv_scr[c], axis=-1, keepdims=True) for c in range(len(chains))]
    m_hi = [a.astype(BF16) for a in m]
    m_lo = [(a - b.astype(F32)).astype(BF16) for a, b in zip(m, m_hi)]
    q2 = [jnp.where(lane == HEAD_DIM, -m_hi[c], jnp.where(lane == HEAD_DIM + 1, -m_lo[c], qm[c]))
          for c in range(len(chains))]

    def pv(j, bias):
        ss = [_dot_nt(q2[c], k_tile(u, j)) for c, (u, _) in enumerate(chains)]
        if bias is not None:
            ss = [ss[c] + bias[u] for c, (u, _) in enumerate(chains)]
        ps = [jnp.exp2(a).astype(BF16) for a in ss]
        return [_dot(ps[c], v_tile(u, j)) for c, (u, _) in enumerate(chains)]

    pv_diag = pv(qi, b0)
    pv_prev = pv(prev, b1)
    for c in range(len(chains)):
        acc_scr[c] = pv_diag[c] + pv_prev[c]

    def acc_tile(j):
        pvs = pv(j, None)
        for c in range(len(chains)):
            acc_scr[c] += pvs[c]

    _loop_pairs(prev, acc_tile)
    lam = _lambda(lam_ref)
    outs = []
    for c in range(len(chains)):
        acc = acc_scr[c]
        outs.append(acc[:, :HEAD_DIM] / acc[:, HEAD_DIM:HEAD_DIM + 1])
    for u in heads:
        od = outs[2 * u] - lam * outs[2 * u + 1]
        o_scr[hp * ATTN_HEADS_PER_STEP + u] = _subln(od, sw_ref[...])

    @pl.when(hp == N_HEADS // ATTN_HEADS_PER_STEP - 1)
    def _store():
        o_ref[...] = jnp.concatenate([o_scr[i] for i in range(N_HEADS)], axis=-1).astype(o_ref.dtype)


def _prompt_attention(qh, kh, vh, bias_tab, lam_vecs, subln_row, tq):
    _, bsz, t, _ = qh.shape
    assert t % tq == 0
    nq = t // tq
    hps = ATTN_HEADS_PER_STEP
    body = functools.partial(_pattn_body, tq=tq)
    return pl.pallas_call(
        body,
        grid=(bsz, nq, N_HEADS // hps),
        in_specs=[
            pl.BlockSpec((hps, None, tq, LANES), lambda b, i, h: (h, b, i, 0)),
            pl.BlockSpec((hps, None, t, LANES), lambda b, i, h: (h, b, 0, 0)),
            pl.BlockSpec((hps, None, t, LANES), lambda b, i, h: (h, b, 0, 0)),
            pl.BlockSpec((hps, 2, tq, tq), lambda b, i, h: (h, 0, 0, 0)),
            pl.BlockSpec((4, MAP_DIM), lambda b, i, h: (0, 0)),
            pl.BlockSpec((1, HEAD_DIM), lambda b, i, h: (0, 0)),
        ],
        out_specs=pl.BlockSpec((None, tq, MIX_HALF), lambda b, i, h: (b, i, 0)),
        out_shape=jax.ShapeDtypeStruct((bsz, t, MIX_HALF), BF16),
        scratch_shapes=[pltpu.VMEM((N_HEADS, tq, HEAD_DIM), F32),
                        pltpu.VMEM((2 * hps, tq, LANES), F32),
                        pltpu.VMEM((2 * hps, tq, LANES), F32)],
        compiler_params=_params(("parallel", "parallel", "arbitrary")),
        name="prompt_attn",
    )(qh, kh, vh, bias_tab, lam_vecs, subln_row)


def _prompt_bias_tables(rel_bias, tq):
    r = jnp.arange(-tq + 1, tq, dtype=jnp.int32)
    f0 = jnp.where((r >= 0)[:, None], _rel_bias_shifted(rel_bias, r), -jnp.inf)
    f1 = _rel_bias_shifted(rel_bias, r + tq)

    def toeplitz(f):
        period = 2 * tq
        g = jnp.concatenate([f[tq - 1::-1], f[:1], f[:tq - 1:-1]], axis=0)
        x = jnp.tile(g, (tq, 1))[:tq * (period - 1)].reshape(tq, period - 1, f.shape[-1])
        return x[:, :tq]

    return jnp.transpose(jnp.stack([toeplitz(f0), toeplitz(f1)]), (3, 0, 1, 2)) * LOG2E


def _sattn_body(pt_ref, q_ref, kn_ref, vn_ref, bnear_ref, bnew_ref, lam_ref, sw_ref,
                ck_hbm, cv_hbm, o_ref, kbuf, vbuf, sem, m_scr, l_scr, acc_scr, *, ch, n_chunks, n_seq):
    b = pl.program_id(0)
    c = pl.program_id(1)
    step = b * n_chunks + c
    slot = step % 2
    nr = q_ref.shape[0]
    t_new = nr // (2 * N_HEADS)

    def copies(bb, cc, sl):
        out = []
        for p in range(ch):
            pg = pt_ref[bb, cc * ch + p]
            out.append(pltpu.make_async_copy(ck_hbm.at[0, pg], kbuf.at[sl, p], sem.at[0, sl]))
            out.append(pltpu.make_async_copy(cv_hbm.at[0, pg], vbuf.at[sl, p], sem.at[1, sl]))
        return out

    @pl.when(step == 0)
    def _prime():
        for cp in copies(0, 0, 0):
            cp.start()

    @pl.when(step + 1 < n_seq * n_chunks)
    def _prefetch():
        nxt = step + 1
        for cp in copies(nxt // n_chunks, nxt % n_chunks, 1 - slot):
            cp.start()

    for cp in copies(b, c, slot):
        cp.wait()

    @pl.when(c == 0)
    def _init():
        m_scr[...] = jnp.full(m_scr.shape, -jnp.inf, F32)
        l_scr[...] = jnp.zeros(l_scr.shape, F32)
        acc_scr[...] = jnp.zeros(acc_scr.shape, F32)

    row = _iota((nr, MIX_HALF), 0)
    col = _iota((nr, MIX_HALF), 1)
    own = ((col // HEAD_DIM) == ((row % (nr // 2)) // t_new)) & (((col % HEAD_DIM) // MAP_DIM) == (row // (nr // 2)))
    own_head = (col // HEAD_DIM) == ((row % (nr // 2)) // t_new)
    qbd = jnp.where(own, q_ref[...], jnp.zeros((), BF16))

    is_last = c == n_chunks - 1
    rows_hd = N_HEADS * HEAD_DIM
    kt = jnp.concatenate([kbuf[slot, p].reshape(rows_hd, PAGE) for p in range(ch)], axis=-1).astype(BF16)
    s = _dot(qbd, kt)
    near = jnp.where(is_last, bnear_ref[...], 0.0)
    if ch > 1:
        s = jnp.concatenate([s[:, :(ch - 1) * PAGE], s[:, (ch - 1) * PAGE:] + near], axis=-1)
    else:
        s = s + near

    m_prev = m_scr[:, 0:1]
    l_prev = l_scr[:, 0:1]
    m_new = jnp.maximum(m_prev, jnp.max(s, axis=-1, keepdims=True))
    alpha = jnp.exp(m_prev - m_new)
    pr = jnp.exp(s - m_new)
    l_new = alpha * l_prev + jnp.sum(pr, axis=-1, keepdims=True)
    vt = jnp.concatenate([vbuf[slot, p].reshape(rows_hd, PAGE) for p in range(ch)], axis=-1).astype(BF16)
    acc = alpha * acc_scr[...] + _dot_nt(pr.astype(BF16), vt)
    m_scr[...] = jnp.broadcast_to(m_new, m_scr.shape)
    l_scr[...] = jnp.broadcast_to(l_new, l_scr.shape)
    acc_scr[...] = acc

    @pl.when(is_last)
    def _fin():
        sn = _dot_nt(qbd, kn_ref[...].astype(BF16)) + bnew_ref[...]
        m2 = jnp.maximum(m_new, jnp.max(sn, axis=-1, keepdims=True))
        a2 = jnp.exp(m_new - m2)
        p2 = jnp.exp(sn - m2)
        l2 = a2 * l_new + jnp.sum(p2, axis=-1, keepdims=True)
        acc2 = a2 * acc + _dot(p2.astype(BF16), vn_ref[...].astype(BF16))
        o_full = jnp.where(own_head, acc2 / l2, 0.0)
        fold = _one_hot_f((_iota((MIX_HALF, HEAD_DIM), 0) % HEAD_DIM) == _iota((MIX_HALF, HEAD_DIM), 1))
        o = _dot_x01(o_full, fold)
        half = nr // 2
        od = o[:half] - _lambda(lam_ref) * o[half:]
        o_ref[...] = _subln(od, sw_ref[...])


def _sample_attention(page_table, q_rep, k_new, v_new, cache_kt, cache_vt, bnear, bnew,
                      lam_vecs, subln_row, ch):
    n_seq, n_pages = page_table.shape
    assert n_pages % ch == 0
    n_chunks = n_pages // ch
    nr = q_rep.shape[1]
    t_pad = k_new.shape[1]
    body = functools.partial(_sattn_body, ch=ch, n_chunks=n_chunks, n_seq=n_seq)
    grid_spec = pltpu.PrefetchScalarGridSpec(
        num_scalar_prefetch=1,
        grid=(n_seq, n_chunks),
        in_specs=[
            pl.BlockSpec((None, nr, MIX_HALF), lambda b, c, pt: (b, 0, 0)),
            pl.BlockSpec((None, t_pad, MIX_HALF), lambda b, c, pt: (b, 0, 0)),
            pl.BlockSpec((None, t_pad, MIX_HALF), lambda b, c, pt: (b, 0, 0)),
            pl.BlockSpec((nr, PAGE), lambda b, c, pt: (0, 0)),
            pl.BlockSpec((nr, t_pad), lambda b, c, pt: (0, 0)),
            pl.BlockSpec((4, MAP_DIM), lambda b, c, pt: (0, 0)),
            pl.BlockSpec((1, HEAD_DIM), lambda b, c, pt: (0, 0)),
            pl.BlockSpec(memory_space=pl.ANY),
            pl.BlockSpec(memory_space=pl.ANY),
        ],
        out_specs=pl.BlockSpec((None, nr // 2, HEAD_DIM), lambda b, c, pt: (b, 0, 0)),
        scratch_shapes=[
            pltpu.VMEM((2, ch, N_HEADS, HEAD_DIM, PAGE), F32),
            pltpu.VMEM((2, ch, N_HEADS, HEAD_DIM, PAGE), F32),
            pltpu.SemaphoreType.DMA((2, 2)),
            pltpu.VMEM((nr, LANES), F32),
            pltpu.VMEM((nr, LANES), F32),
            pltpu.VMEM((nr, MIX_HALF), F32),
        ],
    )
    return pl.pallas_call(
        body,
        grid_spec=grid_spec,
        out_shape=jax.ShapeDtypeStruct((n_seq, nr // 2, HEAD_DIM), F32),
        compiler_params=_params(("arbitrary", "arbitrary")),
        name="sample_attn",
    )(page_table, q_rep, k_new, v_new, bnear, bnew, lam_vecs, subln_row, cache_kt, cache_vt)


def _sample_bias_tables(rel_bias, t_new, t_pad):
    H = N_HEADS
    nr = 2 * H * t_new
    row = jnp.arange(nr, dtype=jnp.int32)
    row_h = (row % (H * t_new)) // t_new
    row_t = row % t_new
    pick = (row_h[:, None] == jnp.arange(H, dtype=jnp.int32)[None, :])

    def table(rel):
        allh = _rel_bias_shifted(rel_bias, rel)
        return jnp.sum(jnp.where(pick[:, None, :], allh, 0.0), axis=-1)

    pos = jnp.arange(PAGE, dtype=jnp.int32)
    bnear = table(PAGE + row_t[:, None] - pos[None, :])
    tn = jnp.arange(t_pad, dtype=jnp.int32)
    reln = row_t[:, None] - tn[None, :]
    ok = (reln >= 0) & (tn[None, :] < t_new)
    bnew = jnp.where(ok, table(reln), -jnp.inf)
    return bnear.astype(F32), bnew.astype(F32)


def _outproj_body(x_ref, gdn_ref, diff_ref, wo_ref, nw_ref, wr_ref, x2_ref, hp_ref, route_ref):
    x2 = (x_ref[...] + _dot(gdn_ref[...], wo_ref[:MIX_HALF, :])
          + _dot(diff_ref[...], wo_ref[MIX_HALF:, :]))
    x2_ref[...] = x2
    h = x2 * lax.rsqrt(jnp.mean(x2 * x2, axis=-1, keepdims=True) + NORM_EPS) * nw_ref[...]
    h1, h2 = _split2(h)
    logits = _dot(h1, wr_ref[0]) + (_dot(h1, wr_ref[1]) + _dot(h2, wr_ref[0]))
    tm = logits.shape[0]
    lane = _iota((tm, LANES), 1)
    neg = -jnp.inf
    big = jnp.int32(10_000)
    gl = jnp.where(lane < N_ROUTE_GROUPS, logits, neg)
    gm = jnp.max(gl, axis=-1, keepdims=True)
    g_idx = jnp.min(jnp.where(gl == gm, lane, big), axis=-1, keepdims=True)
    g_w = 1.0 / jnp.sum(jnp.exp(gl - gm), axis=-1, keepdims=True)
    lo = N_ROUTE_GROUPS + EXPERTS_PER_GROUP * g_idx
    el = jnp.where((lane >= lo) & (lane < lo + EXPERTS_PER_GROUP), logits, neg)
    em = jnp.max(el, axis=-1, keepdims=True)
    i1 = jnp.min(jnp.where(el == em, lane, big), axis=-1, keepdims=True)
    el2 = jnp.where(lane == i1, neg, el)
    em2 = jnp.max(el2, axis=-1, keepdims=True)
    i2 = jnp.min(jnp.where(el2 == em2, lane, big), axis=-1, keepdims=True)
    e2 = jnp.exp(em2 - em)
    w1 = g_w / (1.0 + e2)
    w2 = g_w * e2 / (1.0 + e2)
    route = jnp.where(lane == 0, (i1 - N_ROUTE_GROUPS).astype(F32),
                      jnp.where(lane == 1, (i2 - N_ROUTE_GROUPS).astype(F32),
                                jnp.where(lane == 2, w1, jnp.where(lane == 3, w2, 0.0))))
    route_ref[...] = route
    hb = h.astype(BF16).astype(F32)
    lo_bits = pltpu.bitcast(hb[:, :D_MODEL // 2], jnp.uint32) >> 16
    hi_bits = pltpu.bitcast(hb[:, D_MODEL // 2:], jnp.uint32) & jnp.uint32(0xFFFF0000)
    hp_ref[...] = lo_bits | hi_bits


def _outproj(x2d, gdn2d, diff2d, w_out16, norm_w, wr_pair, tm):
    n = x2d.shape[0]
    assert n % tm == 0
    row = lambda i: (i, 0)
    fixed = lambda i: (0, 0)
    return pl.pallas_call(
        _outproj_body,
        grid=(n // tm,),
        in_specs=[
            pl.BlockSpec((tm, D_MODEL), row),
            pl.BlockSpec((tm, MIX_HALF), row),
            pl.BlockSpec((tm, MIX_HALF), row),
            pl.BlockSpec((D_MODEL, D_MODEL), fixed),
            pl.BlockSpec((1, D_MODEL), fixed),
            pl.BlockSpec((2, D_MODEL, LANES), lambda i: (0, 0, 0)),
        ],
        out_specs=(pl.BlockSpec((tm, D_MODEL), row),
                   pl.BlockSpec((tm, D_MODEL // 2), row),
                   pl.BlockSpec((tm, LANES), row)),
        out_shape=(jax.ShapeDtypeStruct((n, D_MODEL), F32),
                   jax.ShapeDtypeStruct((n, D_MODEL // 2), jnp.uint32),
                   jax.ShapeDtypeStruct((n, LANES), F32)),
        compiler_params=_params(("parallel",)),
        name="outproj_router",
    )(x2d, gdn2d, diff2d, w_out16, norm_w, wr_pair)


def _moe_plan(e_idx, w, tile, max_blocks):
    n = e_idx.shape[0]
    n_tiles = n // tile
    m = 2 * tile
    experts = jnp.arange(N_EXPERTS, dtype=jnp.int32)
    e = e_idx.reshape(n_tiles, m)
    hit = e[..., None] == experts
    onehot = hit.astype(jnp.int32)
    sub = math.gcd(m, MOE_BLOCK)
    tri = (jnp.arange(sub)[:, None] >= jnp.arange(sub)[None, :]).astype(BF16)
    within = jnp.einsum("ij,tcje->tcie", tri, hit.astype(BF16).reshape(n_tiles, m // sub, sub, N_EXPERTS),
                        preferred_element_type=F32)
    chunk_tot = within[:, :, -1, :]
    chunk_off = jnp.cumsum(chunk_tot, axis=1) - chunk_tot
    csum = (within + chunk_off[:, :, None, :]).reshape(n_tiles, m, N_EXPERTS).astype(jnp.int32)
    counts = csum[:, -1, :]
    nb_e = (counts + MOE_BLOCK - 1) // MOE_BLOCK
    pad_ends = jnp.cumsum(nb_e, axis=1)
    pad_starts = pad_ends - nb_e
    nblk = pad_ends[:, -1]
    dest = jnp.sum(onehot * (pad_starts[:, None, :] * MOE_BLOCK + csum - 1), axis=-1)
    blk = jnp.arange(max_blocks, dtype=jnp.int32)
    block_e = jnp.sum((pad_ends[:, None, :] <= blk[None, :, None]).astype(jnp.int32), axis=-1)
    block_e = jnp.minimum(block_e, N_EXPERTS - 1)
    oh_b = (block_e[..., None] == experts).astype(jnp.int32)
    cnt_b = jnp.sum(oh_b * counts[:, None, :], axis=-1)
    ps_b = jnp.sum(oh_b * pad_starts[:, None, :], axis=-1)
    live = blk[None, :] < nblk[:, None]
    nvalid = jnp.where(live, jnp.clip(cnt_b - (blk[None, :] - ps_b) * MOE_BLOCK, 0, MOE_BLOCK), 0)
    last_e = jnp.max(jnp.where(counts > 0, experts[None, :], 0), axis=-1, keepdims=True)
    block_e = jnp.where(live, block_e, last_e)
    return (block_e.reshape(-1).astype(jnp.int32), nblk.astype(jnp.int32), nvalid.reshape(-1).astype(jnp.int32),
            dest.astype(jnp.int32).reshape(n_tiles, 1, m), w.astype(F32).reshape(n_tiles, 1, m))


_SCATTER_BATCH = 4
_INVERT_BATCH = 8
_NORM_ROWS = 256


def _moe_body(be_ref, nblk_ref, nvalid_ref, hp_ref, dest_ref, wa_ref, wg_ref, wu_ref, wd_ref,
              nf_ref, x2_hbm, y_ref, asg_s, xg_scr, yb_scr, sem, *, max_blocks):
    t = pl.program_id(0)
    b = pl.program_id(1)
    m = dest_ref.shape[1]
    tile = y_ref.shape[0]

    @pl.when(b == 0)
    def _start_tile():
        residual = pltpu.make_async_copy(x2_hbm.at[pl.ds(pl.multiple_of(t * tile, tile), tile)], y_ref, sem.at[0])
        residual.start()

        def invert(ab, carry):
            a0 = ab * _INVERT_BATCH
            dests = [dest_ref[0, a0 + u] for u in range(_INVERT_BATCH)]
            for u in range(_INVERT_BATCH):
                asg_s[dests[u]] = a0 + u
            return carry

        lax.fori_loop(0, m // _INVERT_BATCH, invert, 0)
        residual.wait()

    @pl.when(b < nblk_ref[t])
    def _block():
        base = b * MOE_BLOCK
        nv = nvalid_ref[t * max_blocks + b]

        def gather(i, carry):
            tok = asg_s[base + jnp.minimum(i, nv - 1)] >> 1
            xg_scr[pl.ds(i, 1), :] = hp_ref[pl.ds(tok, 1), :]
            return carry

        lax.fori_loop(0, MOE_BLOCK, gather, 0, unroll=8)
        xp = xg_scr[...]
        half = D_MODEL // 2
        x_lo = pltpu.bitcast(xp << 16, F32).astype(BF16)
        x_hi = pltpu.bitcast(xp & jnp.uint32(0xFFFF0000), F32).astype(BF16)
        gt = _dot(x_lo, wg_ref[:half, :]) + _dot(x_hi, wg_ref[half:, :])
        up = _dot(x_lo, wu_ref[:half, :]) + _dot(x_hi, wu_ref[half:, :])
        act = (jax.nn.silu(gt) * up).astype(BF16)
        yb_scr[...] = _dot(act, wd_ref[...])

        def scatter_batch(ib, carry):
            i0 = ib * _SCATTER_BATCH
            asgs = [asg_s[base + i0 + u] for u in range(_SCATTER_BATCH)]
            toks = [a >> 1 for a in asgs]
            vals = [y_ref[pl.ds(toks[u], 1), :] + wa_ref[0, asgs[u]] * yb_scr[pl.ds(i0 + u, 1), :]
                    for u in range(_SCATTER_BATCH)]
            for u in range(_SCATTER_BATCH):
                y_ref[pl.ds(toks[u], 1), :] = vals[u]
            return carry

        nfull = nv // _SCATTER_BATCH
        lax.fori_loop(0, nfull, scatter_batch, 0)

        def scatter_one(i, carry):
            a = asg_s[base + i]
            tok = a >> 1
            y_ref[pl.ds(tok, 1), :] = y_ref[pl.ds(tok, 1), :] + wa_ref[0, a] * yb_scr[pl.ds(i, 1), :]
            return carry

        lax.fori_loop(nfull * _SCATTER_BATCH, nv, scatter_one, 0)

    @pl.when(b == max_blocks - 1)
    def _final_norm():
        rows = min(tile, _NORM_ROWS)

        def norm_rows(i, carry):
            sl = pl.ds(pl.multiple_of(i * rows, rows), rows)
            x = y_ref[sl, :]
            y_ref[sl, :] = x * lax.rsqrt(jnp.mean(x * x, axis=-1, keepdims=True) + NORM_EPS) * nf_ref[...]
            return carry

        lax.fori_loop(0, tile // rows, norm_rows, 0)


def _moe(x2, hp, plan, wg16, wu16, wd16, norm_final_row, tile, max_blocks):
    n = hp.shape[0]
    n_tiles = n // tile
    block_e, nblk, nvalid, dest, wa = plan
    m = 2 * tile
    body = functools.partial(_moe_body, max_blocks=max_blocks)
    half = D_MODEL // 2
    wmap = lambda t, b, be, nb, nv: (be[t * max_blocks + b], 0, 0)
    tmap = lambda t, b, be, nb, nv: (t, 0, 0)
    grid_spec = pltpu.PrefetchScalarGridSpec(
        num_scalar_prefetch=3,
        grid=(n_tiles, max_blocks),
        in_specs=[
            pl.BlockSpec((tile, half), lambda t, b, be, nb, nv: (t, 0)),
            pl.BlockSpec((None, 1, m), tmap, memory_space=pltpu.SMEM),
            pl.BlockSpec((None, 1, m), tmap, memory_space=pltpu.SMEM),
            pl.BlockSpec((None, D_MODEL, EXPERT_FF), wmap),
            pl.BlockSpec((None, D_MODEL, EXPERT_FF), wmap),
            pl.BlockSpec((None, EXPERT_FF, D_MODEL), wmap),
            pl.BlockSpec((1, D_MODEL), lambda t, b, be, nb, nv: (0, 0)),
            pl.BlockSpec(memory_space=pl.ANY),
        ],
        out_specs=pl.BlockSpec((tile, D_MODEL), lambda t, b, be, nb, nv: (t, 0)),
        scratch_shapes=[pltpu.SMEM((max_blocks * MOE_BLOCK,), jnp.int32),
                        pltpu.VMEM((MOE_BLOCK, half), jnp.uint32),
                        pltpu.VMEM((MOE_BLOCK, D_MODEL), F32),
                        pltpu.SemaphoreType.DMA((1,))],
    )
    return pl.pallas_call(
        body,
        grid_spec=grid_spec,
        out_shape=jax.ShapeDtypeStruct((n, D_MODEL), F32),
        compiler_params=_params(("arbitrary", "arbitrary")),
        name="moe",
    )(block_e, nblk, nvalid, hp, dest, wa, wg16, wu16, wd16, norm_final_row, x2)


def _pick_tile(n, pref):
    t = min(n, pref)
    while n % t:
        t //= 2
    return t


def _ffn(x2d, gdn2d, diff2d, w_out16, norm_ffn_row, wr_pair, wg16, wu16, wd16, norm_final_row):
    n = x2d.shape[0]
    tm = _pick_tile(n, 512)
    x2, hp, route = _outproj(x2d, gdn2d, diff2d, w_out16, norm_ffn_row, wr_pair, tm)
    e_idx = route[:, 0:2].astype(jnp.int32)
    w = route[:, 2:4]
    tile = _pick_tile(n, 2048)
    max_blocks = 2 * tile // MOE_BLOCK + N_EXPERTS
    plan = _moe_plan(e_idx, w, tile, max_blocks)
    return _moe(x2, hp, plan, wg16, wu16, wd16, norm_final_row, tile, max_blocks)


def kernel(x_prompt, x_sample, cache_k, cache_v, state_conv, state_gdn, page_table, norm_mix, w_in, conv_w, gdn_a_log, gdn_dt_bias, gdn_norm, lambda_q1, lambda_k1, lambda_q2, lambda_k2, diff_subln, rel_bias, w_out, norm_ffn, router_group, router_expert, w_gate, w_up, w_down, norm_final):
    bsz, seq, _ = x_prompt.shape
    dbsz, dseq, _ = x_sample.shape
    H = N_HEADS
    l = 0

    wl = w_in[l]
    n_ab = 2 * H
    w_packed = jnp.concatenate(
        [wl[:, :_C_GATE[1]], wl[:, _C_GATE[1]:_C_GATE[1] + n_ab],
         jnp.zeros((D_MODEL, LANES - n_ab), wl.dtype), wl[:, _C_GATE[1] + n_ab:]], axis=1).astype(BF16)
    norm_mix_row = norm_mix[l].reshape(1, D_MODEL)
    pad_h = jnp.zeros((LANES - H,), F32)
    alog_row = jnp.concatenate([gdn_a_log[l].astype(F32), pad_h]).reshape(1, LANES)
    dtb_row = jnp.concatenate([gdn_dt_bias[l].astype(F32), pad_h]).reshape(1, LANES)
    gdn_norm_row = jnp.tile(gdn_norm[l].astype(F32), HEADS_PER_GROUP).reshape(1, GROUP_W)
    lam_vecs = jnp.stack([lambda_q1[l], lambda_k1[l], lambda_q2[l], lambda_k2[l]]).astype(F32)
    subln_row = diff_subln[l].astype(F32).reshape(1, HEAD_DIM)
    w_out16 = w_out[l].astype(BF16)
    norm_ffn_row = norm_ffn[l].reshape(1, D_MODEL)
    wr = jnp.concatenate([router_group[l], router_expert[l].reshape(D_MODEL, N_EXPERTS),
                          jnp.zeros((D_MODEL, LANES - N_ROUTE_GROUPS - N_EXPERTS), F32)], axis=1)
    wr1 = wr.astype(BF16)
    wr_pair = jnp.stack([wr1, (wr - wr1.astype(F32)).astype(BF16)])
    wg16 = w_gate[l].astype(BF16)
    wu16 = w_up[l].astype(BF16)
    wd16 = w_down[l].astype(BF16)
    norm_final_row = norm_final.reshape(1, D_MODEL)

    n_p = bsz * seq
    xp2d = x_prompt.reshape(n_p, D_MODEL)
    zqkv, zgate, zab, kf, vf, _, qh, kh, vh = _inproj(xp2d, norm_mix_row, w_packed, _pick_tile(n_p, 256))
    conv_zero = jnp.zeros((bsz, SUBLANES, CONV_DIM), F32)
    gdn_p, s_p = _gdn(zqkv.reshape(bsz, seq, CONV_DIM), zgate.reshape(bsz, seq, MIX_HALF),
                      zab.reshape(bsz, seq, LANES), conv_w[l], alog_row, dtb_row, gdn_norm_row,
                      conv_zero, None, seq, GDN_CHUNK)
    tq = 256
    bias_tab = _prompt_bias_tables(rel_bias, tq)
    hm = lambda a: a.reshape(H, bsz, seq, a.shape[-1])
    diff_p = _prompt_attention(hm(qh), hm(kh), hm(vh), bias_tab, lam_vecs, subln_row, tq)
    y_prompt = _ffn(xp2d, gdn_p.reshape(n_p, MIX_HALF), diff_p.reshape(n_p, MIX_HALF), w_out16,
                    norm_ffn_row, wr_pair, wg16, wu16, wd16, norm_final_row).reshape(bsz, seq, D_MODEL)
    new_k_prompt = kf.reshape(1, bsz, seq, H, HEAD_DIM)
    new_v_prompt = vf.reshape(1, bsz, seq, H, HEAD_DIM)
    new_conv_prompt = zqkv.reshape(bsz, seq, CONV_DIM)[:, seq - (CONV_W - 1):, :][None]
    new_gdn_prompt = s_p[None]

    n_s = dbsz * dseq
    xs2d = x_sample.reshape(n_s, D_MODEL)
    zqkv_s, zgate_s, zab_s, kf_s, vf_s, qf_s, _, _, _ = _inproj(xs2d, norm_mix_row, w_packed, _pick_tile(n_s, 256))
    C = GDN_CHUNK_SAMPLE
    assert dseq <= C
    padt = lambda a: jnp.pad(a.reshape(dbsz, dseq, -1), ((0, 0), (0, C - dseq), (0, 0)))
    conv_init = jnp.pad(state_conv[l].astype(F32), ((0, 0), (SUBLANES - (CONV_W - 1), 0), (0, 0)))
    gdn_s, s_s = _gdn(padt(zqkv_s), padt(zgate_s), padt(zab_s), conv_w[l], alog_row, dtb_row,
                      gdn_norm_row, conv_init, state_gdn[l].astype(F32), dseq, C)
    gdn_s = gdn_s[:, :dseq, :].reshape(n_s, MIX_HALF)
    t_pad = BF16_ROWS
    q_rep = jnp.broadcast_to(qf_s.reshape(dbsz, 1, 1, dseq, MIX_HALF),
                             (dbsz, 2, H, dseq, MIX_HALF)).reshape(dbsz, 2 * H * dseq, MIX_HALF)
    newpad = lambda a: jnp.pad(a.reshape(dbsz, dseq, MIX_HALF), ((0, 0), (0, t_pad - dseq), (0, 0)))
    bnear, bnew = _sample_bias_tables(rel_bias, dseq, t_pad)
    cache_kt = jnp.transpose(cache_k, (0, 1, 3, 4, 2))
    cache_vt = jnp.transpose(cache_v, (0, 1, 3, 4, 2))
    n_pages = page_table.shape[1]
    ch = 16 if n_pages % 16 == 0 else 1
    diff_s = _sample_attention(page_table, q_rep, newpad(kf_s), newpad(vf_s), cache_kt, cache_vt,
                               bnear, bnew, lam_vecs, subln_row, ch)
    diff_s = jnp.transpose(diff_s.reshape(dbsz, H, dseq, HEAD_DIM), (0, 2, 1, 3)).reshape(n_s, MIX_HALF)
    y_sample = _ffn(xs2d, gdn_s, diff_s.astype(BF16), w_out16, norm_ffn_row, wr_pair, wg16, wu16, wd16,
                    norm_final_row).reshape(dbsz, dseq, D_MODEL)
    new_k_sample = kf_s.reshape(1, dbsz, dseq, H, HEAD_DIM)
    new_v_sample = vf_s.reshape(1, dbsz, dseq, H, HEAD_DIM)
    zq3 = zqkv_s.reshape(dbsz, dseq, CONV_DIM)
    new_conv_sample = jnp.concatenate([state_conv[l].astype(F32), zq3], axis=1)[:, -(CONV_W - 1):, :][None]
    new_gdn_sample = s_s[None]

    return (y_prompt, y_sample, new_k_prompt, new_v_prompt, new_k_sample, new_v_sample,
            new_conv_prompt, new_conv_sample, new_gdn_prompt, new_gdn_sample)
```

```python
import functools
import math

import jax
import jax.numpy as jnp
from jax import lax
from jax.experimental import pallas as pl
from jax.experimental.pallas import tpu as pltpu

F32 = jnp.float32
BF16 = jnp.bfloat16

D_MODEL = 1024
N_HEADS = 8
HEAD_DIM = 64
MAP_DIM = 32
MIX_HALF = N_HEADS * HEAD_DIM
CONV_W = 4
CONV_DIM = 3 * MIX_HALF
GDN_CHUNK = 64
GDN_CHUNK_SAMPLE = 16
HEADS_PER_GROUP = 4
GROUP_W = HEADS_PER_GROUP * HEAD_DIM
N_GROUPS_GDN = N_HEADS // HEADS_PER_GROUP
PAGE = 128
N_BUCKETS = 32
MAX_DISTANCE = 128
N_ROUTE_GROUPS = 4
EXPERTS_PER_GROUP = 8
N_EXPERTS = N_ROUTE_GROUPS * EXPERTS_PER_GROUP
EXPERT_FF = D_MODEL // 2
MOE_BLOCK = 128
NORM_EPS = 1e-6
SUBLN_EPS = 1e-5
LAMBDA_INIT = 0.8 - 0.6 * math.exp(-0.3 * 0)
LOG2E = math.log2(math.e)

LANES = 128
SUBLANES = 8
BF16_ROWS = 16
VMEM_LIMIT = 48 * 1024 * 1024

_C_QKV = (0, CONV_DIM)
_C_GATE = (_C_QKV[1], _C_QKV[1] + MIX_HALF)
_C_AB = (_C_GATE[1], _C_GATE[1] + LANES)
_C_DQ = (_C_AB[1], _C_AB[1] + MIX_HALF)
_C_DK = (_C_DQ[1], _C_DQ[1] + MIX_HALF)
_C_DV = (_C_DK[1], _C_DK[1] + MIX_HALF)
IN_COLS_PACKED = _C_DV[1]


def _params(sem, vmem=VMEM_LIMIT):
    return pltpu.CompilerParams(dimension_semantics=sem, vmem_limit_bytes=vmem)


def _dot(a, b):
    return jnp.dot(a, b, preferred_element_type=F32)


def _dot_nt(a, b):
    return lax.dot_general(a, b, (((1,), (1,)), ((), ())), preferred_element_type=F32)


def _dot_tn(a, b):
    return lax.dot_general(a, b, (((0,), (0,)), ((), ())), preferred_element_type=F32)


def _split2(x):
    x1 = x.astype(BF16)
    x2 = (x - x1.astype(F32)).astype(BF16)
    return x1, x2


def _split3(x):
    x1 = x.astype(BF16)
    r = x - x1.astype(F32)
    x2 = r.astype(BF16)
    x3 = (r - x2.astype(F32)).astype(BF16)
    return x1, x2, x3


def _dot_x01(x, m01):
    x1, x2, x3 = _split3(x)
    return _dot(x1, m01) + (_dot(x2, m01) + _dot(x3, m01))


def _dot_01x(m01, x):
    x1, x2, x3 = _split3(x)
    return _dot(m01, x1) + (_dot(m01, x2) + _dot(m01, x3))


def _iota(shape, dim):
    return lax.broadcasted_iota(jnp.int32, shape, dim)


def _one_hot_f(cond):
    return jnp.where(cond, 1.0, 0.0).astype(BF16)


def _inproj_body(x_ref, nw_ref, w_ref, wkvt_ref, zqkv_ref, zgate_ref, zab_ref, kf_ref, vf_ref,
                 qf_ref, qh_ref, kh_ref, vh_ref, *, kv_token_minor):
    x = x_ref[...]
    h = x * lax.rsqrt(jnp.mean(x * x, axis=-1, keepdims=True) + NORM_EPS) * nw_ref[...]
    hb = h.astype(BF16)

    def proj(cols):
        return _dot(hb, w_ref[:, cols[0]:cols[1]])

    zqkv_ref[...] = proj(_C_QKV)
    zgate_ref[...] = proj(_C_GATE)
    zab_ref[...] = proj(_C_AB)
    zq = proj(_C_DQ) * (MAP_DIM ** -0.5)
    zk = proj(_C_DK)
    zv = proj(_C_DV)
    if kv_token_minor:
        kf_ref[...] = _dot_nt(wkvt_ref[0], hb)
        vf_ref[...] = _dot_nt(wkvt_ref[1], hb)
    else:
        kf_ref[...] = zk
        vf_ref[...] = zv
    qf_ref[...] = zq.astype(BF16)
    tm = x.shape[0]
    zq2 = zq * LOG2E
    zeros = jnp.zeros((tm, HEAD_DIM), BF16)
    ones = jnp.ones((tm, HEAD_DIM), BF16)
    two_ones = jnp.where(_iota((tm, HEAD_DIM), 1) < 2, 1.0, 0.0).astype(BF16)
    for h_i in range(N_HEADS):
        sl = slice(HEAD_DIM * h_i, HEAD_DIM * (h_i + 1))
        qh_ref[h_i] = jnp.concatenate([zq2[:, sl].astype(BF16), zeros], axis=-1)
        kh_ref[h_i] = jnp.concatenate([zk[:, sl].astype(BF16), two_ones], axis=-1)
        vh_ref[h_i] = jnp.concatenate([zv[:, sl].astype(BF16), ones], axis=-1)


def _inproj(x2d, norm_w, w_packed, wkv_t, tm, seq_len=None):
    n = x2d.shape[0]
    assert n % tm == 0
    row = lambda i: (i, 0)
    fixed = lambda i: (0, 0)
    hm = lambda i: (0, i, 0)
    kv_token_minor = seq_len is not None
    if kv_token_minor:
        assert seq_len % tm == 0 and n % seq_len == 0
        tiles = seq_len // tm
        kv_shape = jax.ShapeDtypeStruct((n // seq_len, MIX_HALF, seq_len), F32)
        kv_spec = pl.BlockSpec((None, MIX_HALF, tm), lambda i: (i // tiles, 0, i % tiles))
    else:
        kv_shape = jax.ShapeDtypeStruct((n, MIX_HALF), F32)
        kv_spec = pl.BlockSpec((tm, MIX_HALF), row)
    out_shape = (
        jax.ShapeDtypeStruct((n, CONV_DIM), F32),
        jax.ShapeDtypeStruct((n, MIX_HALF), F32),
        jax.ShapeDtypeStruct((n, LANES), F32),
        kv_shape,
        kv_shape,
        jax.ShapeDtypeStruct((n, MIX_HALF), BF16),
        jax.ShapeDtypeStruct((N_HEADS, n, LANES), BF16),
        jax.ShapeDtypeStruct((N_HEADS, n, LANES), BF16),
        jax.ShapeDtypeStruct((N_HEADS, n, LANES), BF16),
    )
    out_specs = (
        pl.BlockSpec((tm, CONV_DIM), row),
        pl.BlockSpec((tm, MIX_HALF), row),
        pl.BlockSpec((tm, LANES), row),
        kv_spec,
        kv_spec,
        pl.BlockSpec((tm, MIX_HALF), row),
        pl.BlockSpec((N_HEADS, tm, LANES), hm),
        pl.BlockSpec((N_HEADS, tm, LANES), hm),
        pl.BlockSpec((N_HEADS, tm, LANES), hm),
    )
    return pl.pallas_call(
        functools.partial(_inproj_body, kv_token_minor=kv_token_minor),
        grid=(n // tm,),
        in_specs=[pl.BlockSpec((tm, D_MODEL), row),
                  pl.BlockSpec((1, D_MODEL), fixed),
                  pl.BlockSpec((D_MODEL, IN_COLS_PACKED), fixed),
                  pl.BlockSpec((2, MIX_HALF, D_MODEL), lambda i: (0, 0, 0))],
        out_specs=out_specs,
        out_shape=out_shape,
        compiler_params=_params(("parallel",)),
        name="inproj",
    )(x2d, norm_w, w_packed, wkv_t)


def _gdn_body(z_ref, zg_ref, zab_ref, cw_ref, alog_ref, dtb_ref, nw_ref, ci_ref, s0_ref,
              o_ref, sfin_ref, s_scr, prev_scr, *, t_valid, zero_init, nb, chunk):
    C = chunk
    W = GROUP_W
    WC = HEADS_PER_GROUP * C
    c = pl.program_id(1)
    n_chunks = pl.num_programs(1)

    rw = _iota((W, W), 0)
    cw = _iota((W, W), 1)
    blk = (rw // HEAD_DIM) == (cw // HEAD_DIM)
    ones_blk = _one_hot_f(blk)
    mask_kd = (_iota((WC, W), 0) // C) == (_iota((WC, W), 1) // HEAD_DIM)
    mask_cc = (_iota((WC, WC), 0) // C) == (_iota((WC, WC), 1) // C)
    ri = _iota((C, WC), 0)
    cj = _iota((C, WC), 1) % C
    incl = ri >= cj
    strict = ri > cj
    eye_all = ri == cj
    eye_f = jnp.where(eye_all, 1.0, 0.0)
    zero16 = jnp.zeros((), BF16)

    @pl.when(c == 0)
    def _init():
        prev_scr[...] = ci_ref[...]
        for sg in range(nb * N_GROUPS_GDN):
            bb, g = divmod(sg, N_GROUPS_GDN)
            if zero_init:
                s_scr[sg] = jnp.zeros((W, W), F32)
            else:
                s0 = s0_ref[bb, g * HEADS_PER_GROUP:(g + 1) * HEADS_PER_GROUP].reshape(W, HEAD_DIM)
                et = _one_hot_f(_iota((HEAD_DIM, W), 0) == (_iota((HEAD_DIM, W), 1) % HEAD_DIM))
                s_scr[sg] = jnp.where(blk, _dot_x01(s0, et), 0.0)

    tri = _one_hot_f(_iota((C, C), 0) >= _iota((C, C), 1))

    def blocksum(v):
        return _dot(v.astype(BF16), ones_blk)

    def bd(x16):
        return jnp.where(mask_kd, jnp.concatenate([x16] * HEADS_PER_GROUP, axis=0), zero16)

    def bd_cc(x16):
        return jnp.where(mask_cc, jnp.concatenate([x16] * HEADS_PER_GROUP, axis=0), zero16)

    seqs = range(nb)
    chains = [(bb, g) for bb in seqs for g in range(N_GROUPS_GDN)]

    def each(fn, *lists):
        return [fn(*args) for args in zip(*lists)]

    w = cw_ref[...]
    xc = [jnp.concatenate([prev_scr[bb], z_ref[bb]], axis=0) for bb in seqs]
    y = each(lambda a: (a * w[3:4] + pltpu.roll(a, 1, 0) * w[2:3]
                        + pltpu.roll(a, 2, 0) * w[1:2] + pltpu.roll(a, 3, 0) * w[0:1]), xc)
    for bb in seqs:
        prev_scr[bb] = xc[bb][C:C + SUBLANES]
    qkv = each(lambda a: jax.nn.silu(a[SUBLANES:SUBLANES + C]), y)

    valid = (_iota((C, LANES), 0) + c * C) < t_valid
    zab = [zab_ref[bb] for bb in seqs]
    neg_a = -jnp.exp(alog_ref[...])
    gfull = each(lambda a: jnp.where(valid, neg_a * jax.nn.softplus(a + dtb_ref[...]), 0.0), zab)
    bfull = each(lambda a: jnp.where(valid, jax.nn.sigmoid(a), 0.0), zab)
    gcum_s = each(lambda a: _dot_01x(tri, a), gfull)

    lane_r = _iota((LANES, W), 0)
    head_c = _iota((LANES, W), 1) // HEAD_DIM
    e_g = [_one_hot_f(lane_r == head_c + g * HEADS_PER_GROUP) for g in range(N_GROUPS_GDN)]
    e_b = [_one_hot_f(lane_r == head_c + g * HEADS_PER_GROUP + N_HEADS) for g in range(N_GROUPS_GDN)]

    def cols(part, g):
        return slice(part * MIX_HALF + g * W, part * MIX_HALF + (g + 1) * W)

    q = [qkv[bb][:, cols(0, g)] for bb, g in chains]
    k = [qkv[bb][:, cols(1, g)] for bb, g in chains]
    v = [qkv[bb][:, cols(2, g)] for bb, g in chains]
    q = each(lambda a: a * lax.rsqrt(blocksum(a * a) + 1e-6) * (HEAD_DIM ** -0.5), q)
    k = each(lambda a: a * lax.rsqrt(blocksum(a * a) + 1e-6), k)
    gexp = [_dot_x01(gcum_s[bb], e_g[g]) for bb, g in chains]
    bexp = [_dot_x01(bfull[bb], e_b[g]) for bb, g in chains]
    if WC == W:
        gexp_c = gexp
    else:
        lane_rc = _iota((LANES, WC), 0)
        head_cc = _iota((LANES, WC), 1) // C
        e_gc = [_one_hot_f(lane_rc == head_cc + g * HEADS_PER_GROUP) for g in range(N_GROUPS_GDN)]
        gexp_c = [_dot_x01(gcum_s[bb], e_gc[g]) for bb, g in chains]
    grow = each(lambda a: jnp.sum(jnp.where(eye_all, a, 0.0), axis=0, keepdims=True), gexp_c)
    decay = each(lambda a, r: jnp.where(incl, jnp.exp(jnp.where(incl, a - r, 0.0)), 0.0), gexp_c, grow)

    kb = each(lambda a, b: a * b, k, bexp)
    kkqk = each(lambda a, b, kk: _dot_nt(jnp.concatenate([a, b], axis=0).astype(BF16), bd(kk.astype(BF16))),
                kb, q, k)
    lmat = each(lambda a, d: jnp.where(strict, a[:C] * d, 0.0), kkqk, decay)
    qkd = each(lambda a, d: a[C:] * d, kkqk, decay)

    tinv = each(lambda a: eye_f - a, lmat)
    p16 = each(lambda a: a.astype(BF16), lmat)
    bp = each(bd_cc, p16)
    for _ in range(int(math.log2(C)) - 1):
        p16 = each(lambda a, b: _dot(a, b).astype(BF16), p16, bp)
        bp = each(bd_cc, p16)
        tinv = each(lambda t, b: t + _dot(t.astype(BF16), b), tinv, bp)

    eg = each(jnp.exp, gexp)
    vb = each(lambda a, b: a * b, v, bexp)
    kbg = each(lambda a, b: a * b, kb, eg)
    corr = each(lambda t, a, b: _dot((t - eye_f).astype(BF16),
                                     jnp.concatenate([bd(a.astype(BF16)), bd(b.astype(BF16))], axis=1)),
                tinv, vb, kbg)
    u = each(lambda a, cr: a + cr[:, :W], vb, corr)
    wm = each(lambda a, cr: a + cr[:, W:], kbg, corr)
    s = [s_scr[bb * N_GROUPS_GDN + g] for bb, g in chains]
    s16 = each(lambda a: a.astype(BF16), s)
    v_new = each(lambda a, b, st: a - _dot(b.astype(BF16), st), u, wm, s16)
    o = each(lambda a, e, st, d, vn: _dot((a * e).astype(BF16), st) + _dot(d.astype(BF16), bd(vn.astype(BF16))),
             q, eg, s16, qkd, v_new)
    glast = each(lambda a: a[C - 1:C, :], gexp)
    kd = each(lambda a, gl, ge: a * jnp.exp(gl - ge), k, glast, gexp)
    upd = each(lambda a, vn: _dot_tn(a.astype(BF16), vn.astype(BF16)), kd, v_new)
    s_new = each(lambda st, gl, up: st * jnp.exp(gl) + jnp.where(blk, up, 0.0), s, glast, upd)
    for (bb, g), st in zip(chains, s_new):
        s_scr[bb * N_GROUPS_GDN + g] = st

    on = each(lambda a: a * lax.rsqrt(blocksum(a * a) * (1.0 / HEAD_DIM) + NORM_EPS) * nw_ref[...], o)
    for (bb, g), a in zip(chains, on):
        o_ref[bb, :, cols(0, g)] = (a * jax.nn.silu(zg_ref[bb, :, cols(0, g)])).astype(o_ref.dtype)

    @pl.when(c == n_chunks - 1)
    def _fin():
        ett = _one_hot_f((_iota((W, HEAD_DIM), 0) % HEAD_DIM) == _iota((W, HEAD_DIM), 1))
        for sg in range(nb * N_GROUPS_GDN):
            bb, g = divmod(sg, N_GROUPS_GDN)
            sout = _dot_x01(s_scr[sg], ett)
            sfin_ref[bb, g * HEADS_PER_GROUP:(g + 1) * HEADS_PER_GROUP] = sout.reshape(
                HEADS_PER_GROUP, HEAD_DIM, HEAD_DIM)


def _gdn(zqkv, zgate, zab, conv_w, alog_row, dtb_row, norm_row, conv_init, s0, t_valid, C):
    bsz, t, _ = zqkv.shape
    assert t % C == 0 and C % BF16_ROWS == 0
    n_chunks = t // C
    nb = 4 if bsz % 4 == 0 else (2 if bsz % 2 == 0 else 1)
    zero_init = s0 is None
    if zero_init:
        s0 = jnp.zeros((nb, N_HEADS, HEAD_DIM, HEAD_DIM), F32)
        s0_map = lambda b, c: (0, 0, 0, 0)
    else:
        s0_map = lambda b, c: (b, 0, 0, 0)
    chunk = lambda b, c: (b, c, 0)
    fixed = lambda b, c: (0, 0)
    in_specs = [
        pl.BlockSpec((nb, C, CONV_DIM), chunk),
        pl.BlockSpec((nb, C, MIX_HALF), chunk),
        pl.BlockSpec((nb, C, LANES), chunk),
        pl.BlockSpec((CONV_W, CONV_DIM), fixed),
        pl.BlockSpec((1, LANES), fixed),
        pl.BlockSpec((1, LANES), fixed),
        pl.BlockSpec((1, GROUP_W), fixed),
        pl.BlockSpec((nb, SUBLANES, CONV_DIM), lambda b, c: (b, 0, 0)),
        pl.BlockSpec((nb, N_HEADS, HEAD_DIM, HEAD_DIM), s0_map),
    ]
    out_specs = (
        pl.BlockSpec((nb, C, MIX_HALF), chunk),
        pl.BlockSpec((nb, N_HEADS, HEAD_DIM, HEAD_DIM), lambda b, c: (b, 0, 0, 0)),
    )
    out_shape = (
        jax.ShapeDtypeStruct((bsz, t, MIX_HALF), BF16),
        jax.ShapeDtypeStruct((bsz, N_HEADS, HEAD_DIM, HEAD_DIM), F32),
    )
    body = functools.partial(_gdn_body, t_valid=t_valid, zero_init=zero_init, nb=nb, chunk=C)
    return pl.pallas_call(
        body,
        grid=(bsz // nb, n_chunks),
        in_specs=in_specs,
        out_specs=out_specs,
        out_shape=out_shape,
        scratch_shapes=[pltpu.VMEM((nb * N_GROUPS_GDN, GROUP_W, GROUP_W), F32),
                        pltpu.VMEM((nb, SUBLANES, CONV_DIM), F32)],
        compiler_params=_params(("parallel", "arbitrary")),
        name="gdn",
    )(zqkv, zgate, zab, conv_w, alog_row, dtb_row, norm_row, conv_init, s0)


def _t5_bucket(rel):
    n = jnp.maximum(rel, 0)
    max_exact = N_BUCKETS // 2
    large = max_exact + (jnp.log(jnp.maximum(n, 1).astype(F32) / max_exact)
                         / math.log(MAX_DISTANCE / max_exact) * (N_BUCKETS - max_exact)).astype(jnp.int32)
    large = jnp.minimum(large, N_BUCKETS - 1)
    return jnp.where(n < max_exact, n, large)


def _rel_bias_shifted(rel_bias, rel):
    rb = rel_bias.astype(F32)
    sel = _t5_bucket(rel)[..., None] == jnp.arange(N_BUCKETS, dtype=jnp.int32)
    picked = jnp.sum(jnp.where(sel[..., None], rb, 0.0), axis=-2)
    return picked - rb[N_BUCKETS - 1]


def _lambda(lam_ref):
    lam = lam_ref[...]
    a = jnp.exp(jnp.sum(lam[0:1] * lam[1:2], axis=-1, keepdims=True))
    b = jnp.exp(jnp.sum(lam[2:3] * lam[3:4], axis=-1, keepdims=True))
    return a - b + LAMBDA_INIT


def _subln(o, w_row):
    ms = jnp.mean(o * o, axis=-1, keepdims=True)
    return o * lax.rsqrt(ms + SUBLN_EPS) * w_row * (1.0 - LAMBDA_INIT)


def _loop_pairs(n, body):
    def two(i, carry):
        body(2 * i)
        body(2 * i + 1)
        return carry

    def one(j, carry):
        body(j)
        return carry

    lax.fori_loop(0, n // 2, two, 0)
    lax.fori_loop((n // 2) * 2, n, one, 0)


ATTN_HEADS_PER_STEP = 8


def _pattn_body(q_ref, k_ref, v_ref, bias_ref, lam_ref, sw_ref, o_ref, o_scr, mv_scr, acc_scr, *, tq):
    qi = pl.program_id(1)
    hp = pl.program_id(2)
    tk = tq
    heads = range(ATTN_HEADS_PER_STEP)
    prev = jnp.maximum(qi - 1, 0)
    prev_off = jnp.where(qi > 0, 0.0, -jnp.inf)
    zero = jnp.zeros((), BF16)

    def fold(s):
        out = s[:, :LANES]
        for cidx in range(1, tk // LANES):
            out = jnp.maximum(out, s[:, cidx * LANES:(cidx + 1) * LANES])
        return out

    def k_tile(u, j):
        return k_ref[u, pl.ds(pl.multiple_of(j * tk, tk), tk), :]

    def v_tile(u, j):
        return v_ref[u, pl.ds(pl.multiple_of(j * tk, tk), tk), :]

    def both(x):
        return jnp.concatenate([x, x], axis=0)

    lane = _iota((tq, LANES), 1)
    chains = [(u, mp) for u in heads for mp in range(2)]
    map_lanes = [lane < MAP_DIM, lane >= MAP_DIM]
    q = [q_ref[u] for u in heads]
    qm = [jnp.where(map_lanes[mp], q[u], zero) for u, mp in chains]
    b0 = [bias_ref[u, 0] for u in heads]
    b1 = [bias_ref[u, 1] + prev_off for u in heads]

    s_diag = [_dot_nt(qm[c], k_tile(u, qi)) for c, (u, _) in enumerate(chains)]
    s_prev = [_dot_nt(qm[c], k_tile(u, prev)) for c, (u, _) in enumerate(chains)]
    for c, (u, _) in enumerate(chains):
        mv_scr[c] = jnp.maximum(fold(s_diag[c] + b0[u]), fold(s_prev[c] + b1[u]))

    def max_tile(j):
        ss = [_dot_nt(qm[c], k_tile(u, j)) for c, (u, _) in enumerate(chains)]
        for c in range(len(chains)):
            mv_scr[c] = jnp.maximum(mv_scr[c], fold(ss[c]))

    _loop_pairs(prev, max_tile)
    m = [jnp.max(mv_scr[c], axis=-1, keepdims=True) for c in range(len(chains))]
    m_hi = [a.astype(BF16) for a in m]
    m_lo = [(a - b.astype(F32)).astype(BF16) for a, b in zip(m, m_hi)]
    q2 = [jnp.where(lane == HEAD_DIM, -m_hi[c], jnp.where(lane == HEAD_DIM + 1, -m_lo[c], qm[c]))
          for c in range(len(chains))]

    def pv(j, bias):
        ss = [_dot_nt(q2[c], k_tile(u, j)) for c, (u, _) in enumerate(chains)]
        if bias is not None:
            ss = [ss[c] + bias[u] for c, (u, _) in enumerate(chains)]
        ps = [jnp.exp2(a).astype(BF16) for a in ss]
        return [_dot(ps[c], v_tile(u, j)) for c, (u, _) in enumerate(chains)]

    pv_diag = pv(qi, b0)
    pv_prev = pv(prev, b1)
    for c in range(len(chains)):
        acc_scr[c] = pv_diag[c] + pv_prev[c]

    def acc_tile(j):
        pvs = pv(j, None)
        for c in range(len(chains)):
            acc_scr[c] += pvs[c]

    _loop_pairs(prev, acc_tile)
    lam = _lambda(lam_ref)
    outs = []
    for c in range(len(chains)):
        acc = acc_scr[c]
        outs.append(acc[:, :HEAD_DIM] / acc[:, HEAD_DIM:HEAD_DIM + 1])
    for u in heads:
        od = outs[2 * u] - lam * outs[2 * u + 1]
        o_scr[hp * ATTN_HEADS_PER_STEP + u] = _subln(od, sw_ref[...])

    @pl.when(hp == N_HEADS // ATTN_HEADS_PER_STEP - 1)
    def _store():
        o_ref[...] = jnp.concatenate([o_scr[i] for i in range(N_HEADS)], axis=-1).astype(o_ref.dtype)


def _prompt_attention(qh, kh, vh, bias_tab, lam_vecs, subln_row, tq):
    _, bsz, t, _ = qh.shape
    assert t % tq == 0
    nq = t // tq
    hps = ATTN_HEADS_PER_STEP
    body = functools.partial(_pattn_body, tq=tq)
    return pl.pallas_call(
        body,
        grid=(bsz, nq, N_HEADS // hps),
        in_specs=[
            pl.BlockSpec((hps, None, tq, LANES), lambda b, i, h: (h, b, i, 0)),
            pl.BlockSpec((hps, None, t, LANES), lambda b, i, h: (h, b, 0, 0)),
            pl.BlockSpec((hps, None, t, LANES), lambda b, i, h: (h, b, 0, 0)),
            pl.BlockSpec((hps, 2, tq, tq), lambda b, i, h: (h, 0, 0, 0)),
            pl.BlockSpec((4, MAP_DIM), lambda b, i, h: (0, 0)),
            pl.BlockSpec((1, HEAD_DIM), lambda b, i, h: (0, 0)),
        ],
        out_specs=pl.BlockSpec((None, tq, MIX_HALF), lambda b, i, h: (b, i, 0)),
        out_shape=jax.ShapeDtypeStruct((bsz, t, MIX_HALF), BF16),
        scratch_shapes=[pltpu.VMEM((N_HEADS, tq, HEAD_DIM), F32),
                        pltpu.VMEM((2 * hps, tq, LANES), F32),
                        pltpu.VMEM((2 * hps, tq, LANES), F32)],
        compiler_params=_params(("parallel", "parallel", "arbitrary")),
        name="prompt_attn",
    )(qh, kh, vh, bias_tab, lam_vecs, subln_row)


def _prompt_bias_tables(rel_bias, tq):
    r = jnp.arange(-tq + 1, tq, dtype=jnp.int32)
    f0 = jnp.where((r >= 0)[:, None], _rel_bias_shifted(rel_bias, r), -jnp.inf)
    f1 = _rel_bias_shifted(rel_bias, r + tq)

    def toeplitz(f):
        period = 2 * tq
        g = jnp.concatenate([f[tq - 1::-1], f[:1], f[:tq - 1:-1]], axis=0)
        x = jnp.tile(g, (tq, 1))[:tq * (period - 1)].reshape(tq, period - 1, f.shape[-1])
        return x[:, :tq]

    return jnp.transpose(jnp.stack([toeplitz(f0), toeplitz(f1)]), (3, 0, 1, 2)) * LOG2E


def _sattn_body(pt_ref, q_ref, kn_ref, vn_ref, bnear_ref, bnew_ref, lam_ref, sw_ref,
                ck_hbm, cv_hbm, o_ref, kbuf, vbuf, sem, m_scr, l_scr, acc_scr, *, ch, n_chunks, n_seq):
    b = pl.program_id(0)
    c = pl.program_id(1)
    step = b * n_chunks + c
    slot = step % 2
    nr = q_ref.shape[0]
    t_new = nr // (2 * N_HEADS)

    def copies(bb, cc, sl):
        out = []
        for p in range(ch):
            pg = pt_ref[bb, cc * ch + p]
            out.append(pltpu.make_async_copy(ck_hbm.at[0, pg], kbuf.at[sl, p], sem.at[0, sl]))
            out.append(pltpu.make_async_copy(cv_hbm.at[0, pg], vbuf.at[sl, p], sem.at[1, sl]))
        return out

    @pl.when(step == 0)
    def _prime():
        for cp in copies(0, 0, 0):
            cp.start()

    @pl.when(step + 1 < n_seq * n_chunks)
    def _prefetch():
        nxt = step + 1
        for cp in copies(nxt // n_chunks, nxt % n_chunks, 1 - slot):
            cp.start()

    for cp in copies(b, c, slot):
        cp.wait()

    @pl.when(c == 0)
    def _init():
        m_scr[...] = jnp.full(m_scr.shape, -jnp.inf, F32)
        l_scr[...] = jnp.zeros(l_scr.shape, F32)
        acc_scr[...] = jnp.zeros(acc_scr.shape, F32)

    row = _iota((nr, MIX_HALF), 0)
    col = _iota((nr, MIX_HALF), 1)
    own = ((col // HEAD_DIM) == ((row % (nr // 2)) // t_new)) & (((col % HEAD_DIM) // MAP_DIM) == (row // (nr // 2)))
    own_head = (col // HEAD_DIM) == ((row % (nr // 2)) // t_new)
    qbd = jnp.where(own, q_ref[...], jnp.zeros((), BF16))

    is_last = c == n_chunks - 1
    rows_hd = N_HEADS * HEAD_DIM
    kt = jnp.concatenate([kbuf[slot, p].reshape(rows_hd, PAGE) for p in range(ch)], axis=-1).astype(BF16)
    s = _dot(qbd, kt)
    near = jnp.where(is_last, bnear_ref[...], 0.0)
    if ch > 1:
        s = jnp.concatenate([s[:, :(ch - 1) * PAGE], s[:, (ch - 1) * PAGE:] + near], axis=-1)
    else:
        s = s + near

    m_prev = m_scr[:, 0:1]
    l_prev = l_scr[:, 0:1]
    m_new = jnp.maximum(m_prev, jnp.max(s, axis=-1, keepdims=True))
    alpha = jnp.exp(m_prev - m_new)
    pr = jnp.exp(s - m_new)
    l_new = alpha * l_prev + jnp.sum(pr, axis=-1, keepdims=True)
    vt = jnp.concatenate([vbuf[slot, p].reshape(rows_hd, PAGE) for p in range(ch)], axis=-1).astype(BF16)
    acc = alpha * acc_scr[...] + _dot_nt(pr.astype(BF16), vt)
    m_scr[...] = jnp.broadcast_to(m_new, m_scr.shape)
    l_scr[...] = jnp.broadcast_to(l_new, l_scr.shape)
    acc_scr[...] = acc

    @pl.when(is_last)
    def _fin():
        sn = _dot_nt(qbd, kn_ref[...].astype(BF16)) + bnew_ref[...]
        m2 = jnp.maximum(m_new, jnp.max(sn, axis=-1, keepdims=True))
        a2 = jnp.exp(m_new - m2)
        p2 = jnp.exp(sn - m2)
        l2 = a2 * l_new + jnp.sum(p2, axis=-1, keepdims=True)
        acc2 = a2 * acc + _dot(p2.astype(BF16), vn_ref[...].astype(BF16))
        o_full = jnp.where(own_head, acc2 / l2, 0.0)
        fold = _one_hot_f((_iota((MIX_HALF, HEAD_DIM), 0) % HEAD_DIM) == _iota((MIX_HALF, HEAD_DIM), 1))
        o = _dot_x01(o_full, fold)
        half = nr // 2
        od = o[:half] - _lambda(lam_ref) * o[half:]
        o_ref[...] = _subln(od, sw_ref[...])


def _sample_attention(page_table, q_rep, k_new, v_new, cache_kt, cache_vt, bnear, bnew,
                      lam_vecs, subln_row, ch):
    n_seq, n_pages = page_table.shape
    assert n_pages % ch == 0
    n_chunks = n_pages // ch
    nr = q_rep.shape[1]
    t_pad = k_new.shape[1]
    body = functools.partial(_sattn_body, ch=ch, n_chunks=n_chunks, n_seq=n_seq)
    grid_spec = pltpu.PrefetchScalarGridSpec(
        num_scalar_prefetch=1,
        grid=(n_seq, n_chunks),
        in_specs=[
            pl.BlockSpec((None, nr, MIX_HALF), lambda b, c, pt: (b, 0, 0)),
            pl.BlockSpec((None, t_pad, MIX_HALF), lambda b, c, pt: (b, 0, 0)),
            pl.BlockSpec((None, t_pad, MIX_HALF), lambda b, c, pt: (b, 0, 0)),
            pl.BlockSpec((nr, PAGE), lambda b, c, pt: (0, 0)),
            pl.BlockSpec((nr, t_pad), lambda b, c, pt: (0, 0)),
            pl.BlockSpec((4, MAP_DIM), lambda b, c, pt: (0, 0)),
            pl.BlockSpec((1, HEAD_DIM), lambda b, c, pt: (0, 0)),
            pl.BlockSpec(memory_space=pl.ANY),
            pl.BlockSpec(memory_space=pl.ANY),
        ],
        out_specs=pl.BlockSpec((None, nr // 2, HEAD_DIM), lambda b, c, pt: (b, 0, 0)),
        scratch_shapes=[
            pltpu.VMEM((2, ch, N_HEADS, HEAD_DIM, PAGE), F32),
            pltpu.VMEM((2, ch, N_HEADS, HEAD_DIM, PAGE), F32),
            pltpu.SemaphoreType.DMA((2, 2)),
            pltpu.VMEM((nr, LANES), F32),
            pltpu.VMEM((nr, LANES), F32),
            pltpu.VMEM((nr, MIX_HALF), F32),
        ],
    )
    return pl.pallas_call(
        body,
        grid_spec=grid_spec,
        out_shape=jax.ShapeDtypeStruct((n_seq, nr // 2, HEAD_DIM), F32),
        compiler_params=_params(("arbitrary", "arbitrary")),
        name="sample_attn",
    )(page_table, q_rep, k_new, v_new, bnear, bnew, lam_vecs, subln_row, cache_kt, cache_vt)


def _sample_bias_tables(rel_bias, t_new, t_pad):
    H = N_HEADS
    nr = 2 * H * t_new
    row = jnp.arange(nr, dtype=jnp.int32)
    row_h = (row % (H * t_new)) // t_new
    row_t = row % t_new
    pick = (row_h[:, None] == jnp.arange(H, dtype=jnp.int32)[None, :])

    def table(rel):
        allh = _rel_bias_shifted(rel_bias, rel)
        return jnp.sum(jnp.where(pick[:, None, :], allh, 0.0), axis=-1)

    pos = jnp.arange(PAGE, dtype=jnp.int32)
    bnear = table(PAGE + row_t[:, None] - pos[None, :])
    tn = jnp.arange(t_pad, dtype=jnp.int32)
    reln = row_t[:, None] - tn[None, :]
    ok = (reln >= 0) & (tn[None, :] < t_new)
    bnew = jnp.where(ok, table(reln), -jnp.inf)
    return bnear.astype(F32), bnew.astype(F32)


def _outproj_body(x_ref, gdn_ref, diff_ref, wo_ref, nw_ref, wr_ref, x2_ref, hp_ref, route_ref):
    x2 = (x_ref[...] + _dot(gdn_ref[...], wo_ref[:MIX_HALF, :])
          + _dot(diff_ref[...], wo_ref[MIX_HALF:, :]))
    x2_ref[...] = x2
    h = x2 * lax.rsqrt(jnp.mean(x2 * x2, axis=-1, keepdims=True) + NORM_EPS) * nw_ref[...]
    h1, h2 = _split2(h)
    logits = _dot(h1, wr_ref[0]) + (_dot(h1, wr_ref[1]) + _dot(h2, wr_ref[0]))
    tm = logits.shape[0]
    lane = _iota((tm, LANES), 1)
    neg = -jnp.inf
    big = jnp.int32(10_000)
    gl = jnp.where(lane < N_ROUTE_GROUPS, logits, neg)
    gm = jnp.max(gl, axis=-1, keepdims=True)
    g_idx = jnp.min(jnp.where(gl == gm, lane, big), axis=-1, keepdims=True)
    g_w = 1.0 / jnp.sum(jnp.exp(gl - gm), axis=-1, keepdims=True)
    lo = N_ROUTE_GROUPS + EXPERTS_PER_GROUP * g_idx
    el = jnp.where((lane >= lo) & (lane < lo + EXPERTS_PER_GROUP), logits, neg)
    em = jnp.max(el, axis=-1, keepdims=True)
    i1 = jnp.min(jnp.where(el == em, lane, big), axis=-1, keepdims=True)
    el2 = jnp.where(lane == i1, neg, el)
    em2 = jnp.max(el2, axis=-1, keepdims=True)
    i2 = jnp.min(jnp.where(el2 == em2, lane, big), axis=-1, keepdims=True)
    e2 = jnp.exp(em2 - em)
    w1 = g_w / (1.0 + e2)
    w2 = g_w * e2 / (1.0 + e2)
    route = jnp.where(lane == 0, (i1 - N_ROUTE_GROUPS).astype(F32),
                      jnp.where(lane == 1, (i2 - N_ROUTE_GROUPS).astype(F32),
                                jnp.where(lane == 2, w1, jnp.where(lane == 3, w2, 0.0))))
    route_ref[...] = route
    hb = h.astype(BF16).astype(F32)
    lo_bits = pltpu.bitcast(hb[:, :D_MODEL // 2], jnp.uint32) >> 16
    hi_bits = pltpu.bitcast(hb[:, D_MODEL // 2:], jnp.uint32) & jnp.uint32(0xFFFF0000)
    hp_ref[...] = lo_bits | hi_bits


def _outproj(x2d, gdn2d, diff2d, w_out16, norm_w, wr_pair, tm):
    n = x2d.shape[0]
    assert n % tm == 0
    row = lambda i: (i, 0)
    fixed = lambda i: (0, 0)
    return pl.pallas_call(
        _outproj_body,
        grid=(n // tm,),
        in_specs=[
            pl.BlockSpec((tm, D_MODEL), row),
            pl.BlockSpec((tm, MIX_HALF), row),
            pl.BlockSpec((tm, MIX_HALF), row),
            pl.BlockSpec((D_MODEL, D_MODEL), fixed),
            pl.BlockSpec((1, D_MODEL), fixed),
            pl.BlockSpec((2, D_MODEL, LANES), lambda i: (0, 0, 0)),
        ],
        out_specs=(pl.BlockSpec((tm, D_MODEL), row),
                   pl.BlockSpec((tm, D_MODEL // 2), row),
                   pl.BlockSpec((tm, LANES), row)),
        out_shape=(jax.ShapeDtypeStruct((n, D_MODEL), F32),
                   jax.ShapeDtypeStruct((n, D_MODEL // 2), jnp.uint32),
                   jax.ShapeDtypeStruct((n, LANES), F32)),
        compiler_params=_params(("parallel",)),
        name="outproj_router",
    )(x2d, gdn2d, diff2d, w_out16, norm_w, wr_pair)


def _moe_plan(e_idx, w, tile, max_blocks):
    n = e_idx.shape[0]
    n_tiles = n // tile
    m = 2 * tile
    experts = jnp.arange(N_EXPERTS, dtype=jnp.int32)
    e = e_idx.reshape(n_tiles, m)
    hit = e[..., None] == experts
    onehot = hit.astype(jnp.int32)
    sub = math.gcd(m, MOE_BLOCK)
    tri = (jnp.arange(sub)[:, None] >= jnp.arange(sub)[None, :]).astype(BF16)
    within = jnp.einsum("ij,tcje->tcie", tri, hit.astype(BF16).reshape(n_tiles, m // sub, sub, N_EXPERTS),
                        preferred_element_type=F32)
    chunk_tot = within[:, :, -1, :]
    chunk_off = jnp.cumsum(chunk_tot, axis=1) - chunk_tot
    csum = (within + chunk_off[:, :, None, :]).reshape(n_tiles, m, N_EXPERTS).astype(jnp.int32)
    counts = csum[:, -1, :]
    nb_e = (counts + MOE_BLOCK - 1) // MOE_BLOCK
    pad_ends = jnp.cumsum(nb_e, axis=1)
    pad_starts = pad_ends - nb_e
    nblk = pad_ends[:, -1]
    dest = jnp.sum(onehot * (pad_starts[:, None, :] * MOE_BLOCK + csum - 1), axis=-1)
    blk = jnp.arange(max_blocks, dtype=jnp.int32)
    block_e = jnp.sum((pad_ends[:, None, :] <= blk[None, :, None]).astype(jnp.int32), axis=-1)
    block_e = jnp.minimum(block_e, N_EXPERTS - 1)
    oh_b = (block_e[..., None] == experts).astype(jnp.int32)
    cnt_b = jnp.sum(oh_b * counts[:, None, :], axis=-1)
    ps_b = jnp.sum(oh_b * pad_starts[:, None, :], axis=-1)
    live = blk[None, :] < nblk[:, None]
    nvalid = jnp.where(live, jnp.clip(cnt_b - (blk[None, :] - ps_b) * MOE_BLOCK, 0, MOE_BLOCK), 0)
    last_e = jnp.max(jnp.where(counts > 0, experts[None, :], 0), axis=-1, keepdims=True)
    block_e = jnp.where(live, block_e, last_e)
    return (block_e.reshape(-1).astype(jnp.int32), nblk.astype(jnp.int32), nvalid.reshape(-1).astype(jnp.int32),
            dest.astype(jnp.int32).reshape(n_tiles, 1, m), w.astype(F32).reshape(n_tiles, 1, m))


_SCATTER_BATCH = 4
_INVERT_BATCH = 8
_NORM_ROWS = 256


def _moe_body(be_ref, nblk_ref, nvalid_ref, hp_ref, dest_ref, wa_ref, wg_ref, wu_ref, wd_ref,
              nf_ref, x2_hbm, y_ref, asg_s, xg_scr, yb_scr, sem, *, max_blocks):
    t = pl.program_id(0)
    b = pl.program_id(1)
    m = dest_ref.shape[1]
    tile = y_ref.shape[0]

    @pl.when(b == 0)
    def _start_tile():
        residual = pltpu.make_async_copy(x2_hbm.at[pl.ds(pl.multiple_of(t * tile, tile), tile)], y_ref, sem.at[0])
        residual.start()
        xg_scr[...] = jnp.zeros(xg_scr.shape, xg_scr.dtype)

        def invert(ab, carry):
            a0 = ab * _INVERT_BATCH
            dests = [dest_ref[0, a0 + u] for u in range(_INVERT_BATCH)]
            for u in range(_INVERT_BATCH):
                asg_s[dests[u]] = a0 + u
            return carry

        lax.fori_loop(0, m // _INVERT_BATCH, invert, 0)
        residual.wait()

    @pl.when(b < nblk_ref[t])
    def _block():
        base = b * MOE_BLOCK
        nv = nvalid_ref[t * max_blocks + b]

        def gather_group(ig, carry):
            i0 = ig * SUBLANES
            toks = [asg_s[base + jnp.minimum(i0 + u, nv - 1)] >> 1 for u in range(SUBLANES)]
            rows = [hp_ref[pl.ds(toks[u], 1), :] for u in range(SUBLANES)]
            for u in range(SUBLANES):
                xg_scr[pl.ds(i0 + u, 1), :] = rows[u]
            return carry

        lax.fori_loop(0, (nv + SUBLANES - 1) // SUBLANES, gather_group, 0)
        xp = xg_scr[...]
        half = D_MODEL // 2
        x_lo = pltpu.bitcast(xp << 16, F32).astype(BF16)
        x_hi = pltpu.bitcast(xp & jnp.uint32(0xFFFF0000), F32).astype(BF16)
        gt = _dot(x_lo, wg_ref[:half, :]) + _dot(x_hi, wg_ref[half:, :])
        up = _dot(x_lo, wu_ref[:half, :]) + _dot(x_hi, wu_ref[half:, :])
        act = (jax.nn.silu(gt) * up).astype(BF16)
        yb_scr[...] = _dot(act, wd_ref[...])

        def scatter_batch(ib, carry):
            i0 = ib * _SCATTER_BATCH
            asgs = [asg_s[base + i0 + u] for u in range(_SCATTER_BATCH)]
            toks = [a >> 1 for a in asgs]
            vals = [y_ref[pl.ds(toks[u], 1), :] + wa_ref[0, asgs[u]] * yb_scr[pl.ds(i0 + u, 1), :]
                    for u in range(_SCATTER_BATCH)]
            for u in range(_SCATTER_BATCH):
                y_ref[pl.ds(toks[u], 1), :] = vals[u]
            return carry

        nfull = nv // _SCATTER_BATCH
        lax.fori_loop(0, nfull, scatter_batch, 0)

        def scatter_one(i, carry):
            a = asg_s[base + i]
            tok = a >> 1
            y_ref[pl.ds(tok, 1), :] = y_ref[pl.ds(tok, 1), :] + wa_ref[0, a] * yb_scr[pl.ds(i, 1), :]
            return carry

        lax.fori_loop(nfull * _SCATTER_BATCH, nv, scatter_one, 0)

    @pl.when(b == max_blocks - 1)
    def _final_norm():
        rows = min(tile, _NORM_ROWS)

        def norm_rows(i, carry):
            sl = pl.ds(pl.multiple_of(i * rows, rows), rows)
            x = y_ref[sl, :]
            y_ref[sl, :] = x * lax.rsqrt(jnp.mean(x * x, axis=-1, keepdims=True) + NORM_EPS) * nf_ref[...]
            return carry

        lax.fori_loop(0, tile // rows, norm_rows, 0)


def _moe(x2, hp, plan, wg16, wu16, wd16, norm_final_row, tile, max_blocks):
    n = hp.shape[0]
    n_tiles = n // tile
    block_e, nblk, nvalid, dest, wa = plan
    m = 2 * tile
    body = functools.partial(_moe_body, max_blocks=max_blocks)
    half = D_MODEL // 2
    wmap = lambda t, b, be, nb, nv: (be[t * max_blocks + b], 0, 0)
    tmap = lambda t, b, be, nb, nv: (t, 0, 0)
    grid_spec = pltpu.PrefetchScalarGridSpec(
        num_scalar_prefetch=3,
        grid=(n_tiles, max_blocks),
        in_specs=[
            pl.BlockSpec((tile, half), lambda t, b, be, nb, nv: (t, 0)),
            pl.BlockSpec((None, 1, m), tmap, memory_space=pltpu.SMEM),
            pl.BlockSpec((None, 1, m), tmap, memory_space=pltpu.SMEM),
            pl.BlockSpec((None, D_MODEL, EXPERT_FF), wmap),
            pl.BlockSpec((None, D_MODEL, EXPERT_FF), wmap),
            pl.BlockSpec((None, EXPERT_FF, D_MODEL), wmap),
            pl.BlockSpec((1, D_MODEL), lambda t, b, be, nb, nv: (0, 0)),
            pl.BlockSpec(memory_space=pl.ANY),
        ],
        out_specs=pl.BlockSpec((tile, D_MODEL), lambda t, b, be, nb, nv: (t, 0)),
        scratch_shapes=[pltpu.SMEM((max_blocks * MOE_BLOCK,), jnp.int32),
                        pltpu.VMEM((MOE_BLOCK, half), jnp.uint32),
                        pltpu.VMEM((MOE_BLOCK, D_MODEL), F32),
                        pltpu.SemaphoreType.DMA((1,))],
    )
    return pl.pallas_call(
        body,
        grid_spec=grid_spec,
        out_shape=jax.ShapeDtypeStruct((n, D_MODEL), F32),
        compiler_params=_params(("arbitrary", "arbitrary")),
        name="moe",
    )(block_e, nblk, nvalid, hp, dest, wa, wg16, wu16, wd16, norm_final_row, x2)


def _pick_tile(n, pref):
    t = min(n, pref)
    while n % t:
        t //= 2
    return t


def _ffn(x2d, gdn2d, diff2d, w_out16, norm_ffn_row, wr_pair, wg16, wu16, wd16, norm_final_row):
    n = x2d.shape[0]
    tm = _pick_tile(n, 512)
    x2, hp, route = _outproj(x2d, gdn2d, diff2d, w_out16, norm_ffn_row, wr_pair, tm)
    e_idx = route[:, 0:2].astype(jnp.int32)
    w = route[:, 2:4]
    tile = _pick_tile(n, 2048)
    max_blocks = 2 * tile // MOE_BLOCK + N_EXPERTS
    plan = _moe_plan(e_idx, w, tile, max_blocks)
    return _moe(x2, hp, plan, wg16, wu16, wd16, norm_final_row, tile, max_blocks)


def kernel(x_prompt, x_sample, cache_k, cache_v, state_conv, state_gdn, page_table, norm_mix, w_in, conv_w, gdn_a_log, gdn_dt_bias, gdn_norm, lambda_q1, lambda_k1, lambda_q2, lambda_k2, diff_subln, rel_bias, w_out, norm_ffn, router_group, router_expert, w_gate, w_up, w_down, norm_final):
    bsz, seq, _ = x_prompt.shape
    dbsz, dseq, _ = x_sample.shape
    H = N_HEADS
    l = 0

    wl = w_in[l]
    n_ab = 2 * H
    w_packed = jnp.concatenate(
        [wl[:, :_C_GATE[1]], wl[:, _C_GATE[1]:_C_GATE[1] + n_ab],
         jnp.zeros((D_MODEL, LANES - n_ab), wl.dtype), wl[:, _C_GATE[1] + n_ab:]], axis=1).astype(BF16)
    norm_mix_row = norm_mix[l].reshape(1, D_MODEL)
    pad_h = jnp.zeros((LANES - H,), F32)
    alog_row = jnp.concatenate([gdn_a_log[l].astype(F32), pad_h]).reshape(1, LANES)
    dtb_row = jnp.concatenate([gdn_dt_bias[l].astype(F32), pad_h]).reshape(1, LANES)
    gdn_norm_row = jnp.tile(gdn_norm[l].astype(F32), HEADS_PER_GROUP).reshape(1, GROUP_W)
    lam_vecs = jnp.stack([lambda_q1[l], lambda_k1[l], lambda_q2[l], lambda_k2[l]]).astype(F32)
    subln_row = diff_subln[l].astype(F32).reshape(1, HEAD_DIM)
    w_out16 = w_out[l].astype(BF16)
    norm_ffn_row = norm_ffn[l].reshape(1, D_MODEL)
    wr = jnp.concatenate([router_group[l], router_expert[l].reshape(D_MODEL, N_EXPERTS),
                          jnp.zeros((D_MODEL, LANES - N_ROUTE_GROUPS - N_EXPERTS), F32)], axis=1)
    wr1 = wr.astype(BF16)
    wr_pair = jnp.stack([wr1, (wr - wr1.astype(F32)).astype(BF16)])
    wg16 = w_gate[l].astype(BF16)
    wu16 = w_up[l].astype(BF16)
    wd16 = w_down[l].astype(BF16)
    norm_final_row = norm_final.reshape(1, D_MODEL)

    n_p = bsz * seq
    xp2d = x_prompt.reshape(n_p, D_MODEL)
    wkv_t = jnp.stack([wl[:, _C_GATE[1] + n_ab + MIX_HALF:_C_GATE[1] + n_ab + 2 * MIX_HALF].T,
                       wl[:, _C_GATE[1] + n_ab + 2 * MIX_HALF:].T]).astype(BF16)
    zqkv, zgate, zab, kf_t, vf_t, _, qh, kh, vh = _inproj(xp2d, norm_mix_row, w_packed, wkv_t,
                                                          _pick_tile(seq, 256), seq_len=seq)
    conv_zero = jnp.zeros((bsz, SUBLANES, CONV_DIM), F32)
    gdn_p, s_p = _gdn(zqkv.reshape(bsz, seq, CONV_DIM), zgate.reshape(bsz, seq, MIX_HALF),
                      zab.reshape(bsz, seq, LANES), conv_w[l], alog_row, dtb_row, gdn_norm_row,
                      conv_zero, None, seq, GDN_CHUNK)
    tq = 256
    bias_tab = _prompt_bias_tables(rel_bias, tq)
    hm = lambda a: a.reshape(H, bsz, seq, a.shape[-1])
    diff_p = _prompt_attention(hm(qh), hm(kh), hm(vh), bias_tab, lam_vecs, subln_row, tq)
    y_prompt = _ffn(xp2d, gdn_p.reshape(n_p, MIX_HALF), diff_p.reshape(n_p, MIX_HALF), w_out16,
                    norm_ffn_row, wr_pair, wg16, wu16, wd16, norm_final_row).reshape(bsz, seq, D_MODEL)
    tok_major = lambda a: jnp.transpose(a.reshape(bsz, H, HEAD_DIM, seq), (0, 3, 1, 2))[None]
    new_k_prompt = tok_major(kf_t)
    new_v_prompt = tok_major(vf_t)
    new_conv_prompt = zqkv.reshape(bsz, seq, CONV_DIM)[:, seq - (CONV_W - 1):, :][None]
    new_gdn_prompt = s_p[None]

    n_s = dbsz * dseq
    xs2d = x_sample.reshape(n_s, D_MODEL)
    zqkv_s, zgate_s, zab_s, kf_s, vf_s, qf_s, _, _, _ = _inproj(xs2d, norm_mix_row, w_packed, wkv_t,
                                                                _pick_tile(n_s, 256))
    C = GDN_CHUNK_SAMPLE
    assert dseq <= C
    padt = lambda a: jnp.pad(a.reshape(dbsz, dseq, -1), ((0, 0), (0, C - dseq), (0, 0)))
    conv_init = jnp.pad(state_conv[l].astype(F32), ((0, 0), (SUBLANES - (CONV_W - 1), 0), (0, 0)))
    gdn_s, s_s = _gdn(padt(zqkv_s), padt(zgate_s), padt(zab_s), conv_w[l], alog_row, dtb_row,
                      gdn_norm_row, conv_init, state_gdn[l].astype(F32), dseq, C)
    gdn_s = gdn_s[:, :dseq, :].reshape(n_s, MIX_HALF)
    t_pad = BF16_ROWS
    q_rep = jnp.broadcast_to(qf_s.reshape(dbsz, 1, 1, dseq, MIX_HALF),
                             (dbsz, 2, H, dseq, MIX_HALF)).reshape(dbsz, 2 * H * dseq, MIX_HALF)
    newpad = lambda a: jnp.pad(a.reshape(dbsz, dseq, MIX_HALF), ((0, 0), (0, t_pad - dseq), (0, 0)))
    bnear, bnew = _sample_bias_tables(rel_bias, dseq, t_pad)
    cache_kt = jnp.transpose(cache_k, (0, 1, 3, 4, 2))
    cache_vt = jnp.transpose(cache_v, (0, 1, 3, 4, 2))
    n_pages = page_table.shape[1]
    ch = 16 if n_pages % 16 == 0 else 1
    diff_s = _sample_attention(page_table, q_rep, newpad(kf_s), newpad(vf_s), cache_kt, cache_vt,
                               bnear, bnew, lam_vecs, subln_row, ch)
    diff_s = jnp.transpose(diff_s.reshape(dbsz, H, dseq, HEAD_DIM), (0, 2, 1, 3)).reshape(n_s, MIX_HALF)
    y_sample = _ffn(xs2d, gdn_s, diff_s.astype(BF16), w_out16, norm_ffn_row, wr_pair, wg16, wu16, wd16,
                    norm_final_row).reshape(dbsz, dseq, D_MODEL)
    new_k_sample = kf_s.reshape(1, dbsz, dseq, H, HEAD_DIM)
    new_v_sample = vf_s.reshape(1, dbsz, dseq, H, HEAD_DIM)
    zq3 = zqkv_s.reshape(dbsz, dseq, CONV_DIM)
    new_conv_sample = jnp.concatenate([state_conv[l].astype(F32), zq3], axis=1)[:, -(CONV_W - 1):, :][None]
    new_gdn_sample = s_s[None]

    return (y_prompt, y_sample, new_k_prompt, new_v_prompt, new_k_sample, new_v_sample,
            new_conv_prompt, new_conv_sample, new_gdn_prompt, new_gdn_sample)
```

```python
import functools
import math

import jax
import jax.numpy as jnp
from jax import lax
from jax.experimental import pallas as pl
from jax.experimental.pallas import tpu as pltpu

F32 = jnp.float32
BF16 = jnp.bfloat16

D_MODEL = 1024
N_HEADS = 8
HEAD_DIM = 64
MAP_DIM = 32
MIX_HALF = N_HEADS * HEAD_DIM
CONV_W = 4
CONV_DIM = 3 * MIX_HALF
GDN_CHUNK = 64
GDN_CHUNK_SAMPLE = 16
HEADS_PER_GROUP = 4
GROUP_W = HEADS_PER_GROUP * HEAD_DIM
N_GROUPS_GDN = N_HEADS // HEADS_PER_GROUP
PAGE = 128
N_BUCKETS = 32
MAX_DISTANCE = 128
N_ROUTE_GROUPS = 4
EXPERTS_PER_GROUP = 8
N_EXPERTS = N_ROUTE_GROUPS * EXPERTS_PER_GROUP
EXPERT_FF = D_MODEL // 2
MOE_BLOCK = 128
NORM_EPS = 1e-6
SUBLN_EPS = 1e-5
LAMBDA_INIT = 0.8 - 0.6 * math.exp(-0.3 * 0)
LOG2E = math.log2(math.e)

LANES = 128
SUBLANES = 8
BF16_ROWS = 16
VMEM_LIMIT = 48 * 1024 * 1024

_C_QKV = (0, CONV_DIM)
_C_GATE = (_C_QKV[1], _C_QKV[1] + MIX_HALF)
_C_AB = (_C_GATE[1], _C_GATE[1] + LANES)
_C_DQ = (_C_AB[1], _C_AB[1] + MIX_HALF)
_C_DK = (_C_DQ[1], _C_DQ[1] + MIX_HALF)
_C_DV = (_C_DK[1], _C_DK[1] + MIX_HALF)
IN_COLS_PACKED = _C_DV[1]


def _params(sem, vmem=VMEM_LIMIT):
    return pltpu.CompilerParams(dimension_semantics=sem, vmem_limit_bytes=vmem)


def _dot(a, b):
    return jnp.dot(a, b, preferred_element_type=F32)


def _dot_nt(a, b):
    return lax.dot_general(a, b, (((1,), (1,)), ((), ())), preferred_element_type=F32)


def _dot_tn(a, b):
    return lax.dot_general(a, b, (((0,), (0,)), ((), ())), preferred_element_type=F32)


def _split2(x):
    x1 = x.astype(BF16)
    x2 = (x - x1.astype(F32)).astype(BF16)
    return x1, x2


def _split3(x):
    x1 = x.astype(BF16)
    r = x - x1.astype(F32)
    x2 = r.astype(BF16)
    x3 = (r - x2.astype(F32)).astype(BF16)
    return x1, x2, x3


def _dot_x01(x, m01):
    x1, x2, x3 = _split3(x)
    return _dot(x1, m01) + (_dot(x2, m01) + _dot(x3, m01))


def _dot_01x(m01, x):
    x1, x2, x3 = _split3(x)
    return _dot(m01, x1) + (_dot(m01, x2) + _dot(m01, x3))


def _iota(shape, dim):
    return lax.broadcasted_iota(jnp.int32, shape, dim)


def _one_hot_f(cond):
    return jnp.where(cond, 1.0, 0.0).astype(BF16)


def _inproj_body(x_ref, nw_ref, w_ref, wkvt_ref, zqkv_ref, zgate_ref, zab_ref, kf_ref, vf_ref,
                 qf_ref, qh_ref, kh_ref, vh_ref, *, kv_token_minor):
    x = x_ref[...]
    h = x * lax.rsqrt(jnp.mean(x * x, axis=-1, keepdims=True) + NORM_EPS) * nw_ref[...]
    hb = h.astype(BF16)

    def proj(cols):
        return _dot(hb, w_ref[:, cols[0]:cols[1]])

    zqkv_ref[...] = proj(_C_QKV)
    zgate_ref[...] = proj(_C_GATE)
    zab_ref[...] = proj(_C_AB)
    zq = proj(_C_DQ) * (MAP_DIM ** -0.5)
    zk = proj(_C_DK)
    zv = proj(_C_DV)
    if kv_token_minor:
        kf_ref[...] = _dot_nt(wkvt_ref[0], hb)
        vf_ref[...] = _dot_nt(wkvt_ref[1], hb)
    else:
        kf_ref[...] = zk
        vf_ref[...] = zv
    qf_ref[...] = zq.astype(BF16)
    tm = x.shape[0]
    zq2 = zq * LOG2E
    zeros = jnp.zeros((tm, HEAD_DIM), BF16)
    ones = jnp.ones((tm, HEAD_DIM), BF16)
    two_ones = jnp.where(_iota((tm, HEAD_DIM), 1) < 2, 1.0, 0.0).astype(BF16)
    for h_i in range(N_HEADS):
        sl = slice(HEAD_DIM * h_i, HEAD_DIM * (h_i + 1))
        qh_ref[h_i] = jnp.concatenate([zq2[:, sl].astype(BF16), zeros], axis=-1)
        kh_ref[h_i] = jnp.concatenate([zk[:, sl].astype(BF16), two_ones], axis=-1)
        vh_ref[h_i] = jnp.concatenate([zv[:, sl].astype(BF16), ones], axis=-1)


def _inproj(x2d, norm_w, w_packed, wkv_t, tm, seq_len=None):
    n = x2d.shape[0]
    assert n % tm == 0
    row = lambda i: (i, 0)
    fixed = lambda i: (0, 0)
    hm = lambda i: (0, i, 0)
    kv_token_minor = seq_len is not None
    if kv_token_minor:
        assert seq_len % tm == 0 and n % seq_len == 0
        tiles = seq_len // tm
        kv_shape = jax.ShapeDtypeStruct((n // seq_len, MIX_HALF, seq_len), F32)
        kv_spec = pl.BlockSpec((None, MIX_HALF, tm), lambda i: (i // tiles, 0, i % tiles))
    else:
        kv_shape = jax.ShapeDtypeStruct((n, MIX_HALF), F32)
        kv_spec = pl.BlockSpec((tm, MIX_HALF), row)
    out_shape = (
        jax.ShapeDtypeStruct((n, CONV_DIM), F32),
        jax.ShapeDtypeStruct((n, MIX_HALF), F32),
        jax.ShapeDtypeStruct((n, LANES), F32),
        kv_shape,
        kv_shape,
        jax.ShapeDtypeStruct((n, MIX_HALF), BF16),
        jax.ShapeDtypeStruct((N_HEADS, n, LANES), BF16),
        jax.ShapeDtypeStruct((N_HEADS, n, LANES), BF16),
        jax.ShapeDtypeStruct((N_HEADS, n, LANES), BF16),
    )
    out_specs = (
        pl.BlockSpec((tm, CONV_DIM), row),
        pl.BlockSpec((tm, MIX_HALF), row),
        pl.BlockSpec((tm, LANES), row),
        kv_spec,
        kv_spec,
        pl.BlockSpec((tm, MIX_HALF), row),
        pl.BlockSpec((N_HEADS, tm, LANES), hm),
        pl.BlockSpec((N_HEADS, tm, LANES), hm),
        pl.BlockSpec((N_HEADS, tm, LANES), hm),
    )
    return pl.pallas_call(
        functools.partial(_inproj_body, kv_token_minor=kv_token_minor),
        grid=(n // tm,),
        in_specs=[pl.BlockSpec((tm, D_MODEL), row),
                  pl.BlockSpec((1, D_MODEL), fixed),
                  pl.BlockSpec((D_MODEL, IN_COLS_PACKED), fixed),
                  pl.BlockSpec((2, MIX_HALF, D_MODEL), lambda i: (0, 0, 0))],
        out_specs=out_specs,
        out_shape=out_shape,
        compiler_params=_params(("parallel",)),
        name="inproj",
    )(x2d, norm_w, w_packed, wkv_t)


def _gdn_body(z_ref, zg_ref, zab_ref, cw_ref, alog_ref, dtb_ref, nw_ref, ci_ref, s0_ref,
              o_ref, sfin_ref, s_scr, prev_scr, *, t_valid, zero_init, nb, chunk):
    C = chunk
    W = GROUP_W
    WC = HEADS_PER_GROUP * C
    c = pl.program_id(1)
    n_chunks = pl.num_programs(1)

    rw = _iota((W, W), 0)
    cw = _iota((W, W), 1)
    blk = (rw // HEAD_DIM) == (cw // HEAD_DIM)
    ones_blk = _one_hot_f(blk)
    mask_kd = (_iota((WC, W), 0) // C) == (_iota((WC, W), 1) // HEAD_DIM)
    mask_cc = (_iota((WC, WC), 0) // C) == (_iota((WC, WC), 1) // C)
    ri = _iota((C, WC), 0)
    cj = _iota((C, WC), 1) % C
    incl = ri >= cj
    strict = ri > cj
    eye_all = ri == cj
    eye_f = jnp.where(eye_all, 1.0, 0.0)
    zero16 = jnp.zeros((), BF16)

    @pl.when(c == 0)
    def _init():
        prev_scr[...] = ci_ref[...]
        for sg in range(nb * N_GROUPS_GDN):
            bb, g = divmod(sg, N_GROUPS_GDN)
            if zero_init:
                s_scr[sg] = jnp.zeros((W, W), F32)
            else:
                s0 = s0_ref[bb, g * HEADS_PER_GROUP:(g + 1) * HEADS_PER_GROUP].reshape(W, HEAD_DIM)
                et = _one_hot_f(_iota((HEAD_DIM, W), 0) == (_iota((HEAD_DIM, W), 1) % HEAD_DIM))
                s_scr[sg] = jnp.where(blk, _dot_x01(s0, et), 0.0)

    tri = _one_hot_f(_iota((C, C), 0) >= _iota((C, C), 1))

    def blocksum(v):
        return _dot(v.astype(BF16), ones_blk)

    def bd(x16):
        return jnp.where(mask_kd, jnp.concatenate([x16] * HEADS_PER_GROUP, axis=0), zero16)

    def bd_cc(x16):
        return jnp.where(mask_cc, jnp.concatenate([x16] * HEADS_PER_GROUP, axis=0), zero16)

    seqs = range(nb)
    chains = [(bb, g) for bb in seqs for g in range(N_GROUPS_GDN)]

    def each(fn, *lists):
        return [fn(*args) for args in zip(*lists)]

    w = cw_ref[...]
    xc = [jnp.concatenate([prev_scr[bb], z_ref[bb]], axis=0) for bb in seqs]
    y = each(lambda a: (a * w[3:4] + pltpu.roll(a, 1, 0) * w[2:3]
                        + pltpu.roll(a, 2, 0) * w[1:2] + pltpu.roll(a, 3, 0) * w[0:1]), xc)
    for bb in seqs:
        prev_scr[bb] = xc[bb][C:C + SUBLANES]
    qkv = each(lambda a: jax.nn.silu(a[SUBLANES:SUBLANES + C]), y)

    valid = (_iota((C, LANES), 0) + c * C) < t_valid
    zab = [zab_ref[bb] for bb in seqs]
    neg_a = -jnp.exp(alog_ref[...])
    gfull = each(lambda a: jnp.where(valid, neg_a * jax.nn.softplus(a + dtb_ref[...]), 0.0), zab)
    bfull = each(lambda a: jnp.where(valid, jax.nn.sigmoid(a), 0.0), zab)
    gcum_s = each(lambda a: _dot_01x(tri, a), gfull)

    lane_r = _iota((LANES, W), 0)
    head_c = _iota((LANES, W), 1) // HEAD_DIM
    e_g = [_one_hot_f(lane_r == head_c + g * HEADS_PER_GROUP) for g in range(N_GROUPS_GDN)]
    e_b = [_one_hot_f(lane_r == head_c + g * HEADS_PER_GROUP + N_HEADS) for g in range(N_GROUPS_GDN)]

    def cols(part, g):
        return slice(part * MIX_HALF + g * W, part * MIX_HALF + (g + 1) * W)

    q = [qkv[bb][:, cols(0, g)] for bb, g in chains]
    k = [qkv[bb][:, cols(1, g)] for bb, g in chains]
    v = [qkv[bb][:, cols(2, g)] for bb, g in chains]
    q = each(lambda a: a * lax.rsqrt(blocksum(a * a) + 1e-6) * (HEAD_DIM ** -0.5), q)
    k = each(lambda a: a * lax.rsqrt(blocksum(a * a) + 1e-6), k)
    gexp = [_dot_x01(gcum_s[bb], e_g[g]) for bb, g in chains]
    bexp = [_dot_x01(bfull[bb], e_b[g]) for bb, g in chains]
    if WC == W:
        gexp_c = gexp
    else:
        lane_rc = _iota((LANES, WC), 0)
        head_cc = _iota((LANES, WC), 1) // C
        e_gc = [_one_hot_f(lane_rc == head_cc + g * HEADS_PER_GROUP) for g in range(N_GROUPS_GDN)]
        gexp_c = [_dot_x01(gcum_s[bb], e_gc[g]) for bb, g in chains]
    grow = each(lambda a: jnp.sum(jnp.where(eye_all, a, 0.0), axis=0, keepdims=True), gexp_c)
    decay = each(lambda a, r: jnp.where(incl, jnp.exp(jnp.where(incl, a - r, 0.0)), 0.0), gexp_c, grow)

    kb = each(lambda a, b: a * b, k, bexp)
    kkqk = each(lambda a, b, kk: _dot_nt(jnp.concatenate([a, b], axis=0).astype(BF16), bd(kk.astype(BF16))),
                kb, q, k)
    lmat = each(lambda a, d: jnp.where(strict, a[:C] * d, 0.0), kkqk, decay)
    qkd = each(lambda a, d: a[C:] * d, kkqk, decay)

    tinv = each(lambda a: eye_f - a, lmat)
    p16 = each(lambda a: a.astype(BF16), lmat)
    bp = each(bd_cc, p16)
    for _ in range(int(math.log2(C)) - 1):
        p16 = each(lambda a, b: _dot(a, b).astype(BF16), p16, bp)
        bp = each(bd_cc, p16)
        tinv = each(lambda t, b: t + _dot(t.astype(BF16), b), tinv, bp)

    eg = each(jnp.exp, gexp)
    vb = each(lambda a, b: a * b, v, bexp)
    kbg = each(lambda a, b: a * b, kb, eg)
    corr = each(lambda t, a, b: _dot((t - eye_f).astype(BF16),
                                     jnp.concatenate([bd(a.astype(BF16)), bd(b.astype(BF16))], axis=1)),
                tinv, vb, kbg)
    u = each(lambda a, cr: a + cr[:, :W], vb, corr)
    wm = each(lambda a, cr: a + cr[:, W:], kbg, corr)
    s = [s_scr[bb * N_GROUPS_GDN + g] for bb, g in chains]
    s16 = each(lambda a: a.astype(BF16), s)
    v_new = each(lambda a, b, st: a - _dot(b.astype(BF16), st), u, wm, s16)
    o = each(lambda a, e, st, d, vn: _dot((a * e).astype(BF16), st) + _dot(d.astype(BF16), bd(vn.astype(BF16))),
             q, eg, s16, qkd, v_new)
    glast = each(lambda a: a[C - 1:C, :], gexp)
    kd = each(lambda a, gl, ge: a * jnp.exp(gl - ge), k, glast, gexp)
    upd = each(lambda a, vn: _dot_tn(a.astype(BF16), vn.astype(BF16)), kd, v_new)
    s_new = each(lambda st, gl, up: st * jnp.exp(gl) + jnp.where(blk, up, 0.0), s, glast, upd)
    for (bb, g), st in zip(chains, s_new):
        s_scr[bb * N_GROUPS_GDN + g] = st

    on = each(lambda a: a * lax.rsqrt(blocksum(a * a) * (1.0 / HEAD_DIM) + NORM_EPS) * nw_ref[...], o)
    for (bb, g), a in zip(chains, on):
        o_ref[bb, :, cols(0, g)] = (a * jax.nn.silu(zg_ref[bb, :, cols(0, g)])).astype(o_ref.dtype)

    @pl.when(c == n_chunks - 1)
    def _fin():
        ett = _one_hot_f((_iota((W, HEAD_DIM), 0) % HEAD_DIM) == _iota((W, HEAD_DIM), 1))
        for sg in range(nb * N_GROUPS_GDN):
            bb, g = divmod(sg, N_GROUPS_GDN)
            sout = _dot_x01(s_scr[sg], ett)
            sfin_ref[bb, g * HEADS_PER_GROUP:(g + 1) * HEADS_PER_GROUP] = sout.reshape(
                HEADS_PER_GROUP, HEAD_DIM, HEAD_DIM)


def _gdn(zqkv, zgate, zab, conv_w, alog_row, dtb_row, norm_row, conv_init, s0, t_valid, C):
    bsz, t, _ = zqkv.shape
    assert t % C == 0 and C % BF16_ROWS == 0
    n_chunks = t // C
    nb = 4 if bsz % 4 == 0 else (2 if bsz % 2 == 0 else 1)
    zero_init = s0 is None
    if zero_init:
        s0 = jnp.zeros((nb, N_HEADS, HEAD_DIM, HEAD_DIM), F32)
        s0_map = lambda b, c: (0, 0, 0, 0)
    else:
        s0_map = lambda b, c: (b, 0, 0, 0)
    chunk = lambda b, c: (b, c, 0)
    fixed = lambda b, c: (0, 0)
    in_specs = [
        pl.BlockSpec((nb, C, CONV_DIM), chunk),
        pl.BlockSpec((nb, C, MIX_HALF), chunk),
        pl.BlockSpec((nb, C, LANES), chunk),
        pl.BlockSpec((CONV_W, CONV_DIM), fixed),
        pl.BlockSpec((1, LANES), fixed),
        pl.BlockSpec((1, LANES), fixed),
        pl.BlockSpec((1, GROUP_W), fixed),
        pl.BlockSpec((nb, SUBLANES, CONV_DIM), lambda b, c: (b, 0, 0)),
        pl.BlockSpec((nb, N_HEADS, HEAD_DIM, HEAD_DIM), s0_map),
    ]
    out_specs = (
        pl.BlockSpec((nb, C, MIX_HALF), chunk),
        pl.BlockSpec((nb, N_HEADS, HEAD_DIM, HEAD_DIM), lambda b, c: (b, 0, 0, 0)),
    )
    out_shape = (
        jax.ShapeDtypeStruct((bsz, t, MIX_HALF), BF16),
        jax.ShapeDtypeStruct((bsz, N_HEADS, HEAD_DIM, HEAD_DIM), F32),
    )
    body = functools.partial(_gdn_body, t_valid=t_valid, zero_init=zero_init, nb=nb, chunk=C)
    return pl.pallas_call(
        body,
        grid=(bsz // nb, n_chunks),
        in_specs=in_specs,
        out_specs=out_specs,
        out_shape=out_shape,
        scratch_shapes=[pltpu.VMEM((nb * N_GROUPS_GDN, GROUP_W, GROUP_W), F32),
                        pltpu.VMEM((nb, SUBLANES, CONV_DIM), F32)],
        compiler_params=_params(("parallel", "arbitrary")),
        name="gdn",
    )(zqkv, zgate, zab, conv_w, alog_row, dtb_row, norm_row, conv_init, s0)


def _t5_bucket(rel):
    n = jnp.maximum(rel, 0)
    max_exact = N_BUCKETS // 2
    large = max_exact + (jnp.log(jnp.maximum(n, 1).astype(F32) / max_exact)
                         / math.log(MAX_DISTANCE / max_exact) * (N_BUCKETS - max_exact)).astype(jnp.int32)
    large = jnp.minimum(large, N_BUCKETS - 1)
    return jnp.where(n < max_exact, n, large)


def _rel_bias_shifted(rel_bias, rel):
    rb = rel_bias.astype(F32)
    sel = _t5_bucket(rel)[..., None] == jnp.arange(N_BUCKETS, dtype=jnp.int32)
    picked = jnp.sum(jnp.where(sel[..., None], rb, 0.0), axis=-2)
    return picked - rb[N_BUCKETS - 1]


def _lambda(lam_ref):
    lam = lam_ref[...]
    a = jnp.exp(jnp.sum(lam[0:1] * lam[1:2], axis=-1, keepdims=True))
    b = jnp.exp(jnp.sum(lam[2:3] * lam[3:4], axis=-1, keepdims=True))
    return a - b + LAMBDA_INIT


def _subln(o, w_row):
    ms = jnp.mean(o * o, axis=-1, keepdims=True)
    return o * lax.rsqrt(ms + SUBLN_EPS) * w_row * (1.0 - LAMBDA_INIT)


def _loop_pairs(n, body):
    def two(i, carry):
        body(2 * i)
        body(2 * i + 1)
        return carry

    def one(j, carry):
        body(j)
        return carry

    lax.fori_loop(0, n // 2, two, 0)
    lax.fori_loop((n // 2) * 2, n, one, 0)


ATTN_HEADS_PER_STEP = 8


def _pattn_body(q_ref, k_ref, v_ref, bias_ref, lam_ref, sw_ref, o_ref, o_scr, mv_scr, acc_scr, *, tq):
    qi = pl.program_id(1)
    hp = pl.program_id(2)
    tk = tq
    heads = range(ATTN_HEADS_PER_STEP)
    prev = jnp.maximum(qi - 1, 0)
    prev_off = jnp.where(qi > 0, 0.0, -jnp.inf)
    zero = jnp.zeros((), BF16)

    def fold(s):
        out = s[:, :LANES]
        for cidx in range(1, tk // LANES):
            out = jnp.maximum(out, s[:, cidx * LANES:(cidx + 1) * LANES])
        return out

    def k_tile(u, j):
        return k_ref[u, pl.ds(pl.multiple_of(j * tk, tk), tk), :]

    def v_tile(u, j):
        return v_ref[u, pl.ds(pl.multiple_of(j * tk, tk), tk), :]

    def both(x):
        return jnp.concatenate([x, x], axis=0)

    lane = _iota((tq, LANES), 1)
    chains = [(u, mp) for u in heads for mp in range(2)]
    map_lanes = [lane < MAP_DIM, lane >= MAP_DIM]
    q = [q_ref[u] for u in heads]
    qm = [jnp.where(map_lanes[mp], q[u], zero) for u, mp in chains]
    b0 = [bias_ref[u, 0] for u in heads]
    b1 = [bias_ref[u, 1] + prev_off for u in heads]

    s_diag = [_dot_nt(qm[c], k_tile(u, qi)) for c, (u, _) in enumerate(chains)]
    s_prev = [_dot_nt(qm[c], k_tile(u, prev)) for c, (u, _) in enumerate(chains)]
    for c, (u, _) in enumerate(chains):
        mv_scr[c] = jnp.maximum(fold(s_diag[c] + b0[u]), fold(s_prev[c] + b1[u]))

    def max_tile(j):
        ss = [_dot_nt(qm[c], k_tile(u, j)) for c, (u, _) in enumerate(chains)]
        for c in range(len(chains)):
            mv_scr[c] = jnp.maximum(mv_scr[c], fold(ss[c]))

    _loop_pairs(prev, max_tile)
    m = [jnp.max(mv_scr[c], axis=-1, keepdims=True) for c in range(len(chains))]
    m_hi = [a.astype(BF16) for a in m]
    m_lo = [(a - b.astype(F32)).astype(BF16) for a, b in zip(m, m_hi)]
    q2 = [jnp.where(lane == HEAD_DIM, -m_hi[c], jnp.where(lane == HEAD_DIM + 1, -m_lo[c], qm[c]))
          for c in range(len(chains))]

    def pv(j, bias):
        ss = [_dot_nt(q2[c], k_tile(u, j)) for c, (u, _) in enumerate(chains)]
        if bias is not None:
            ss = [ss[c] + bias[u] for c, (u, _) in enumerate(chains)]
        ps = [jnp.exp2(a).astype(BF16) for a in ss]
        return [_dot(ps[c], v_tile(u, j)) for c, (u, _) in enumerate(chains)]

    pv_diag = pv(qi, b0)
    pv_prev = pv(prev, b1)
    for c in range(len(chains)):
        acc_scr[c] = pv_diag[c] + pv_prev[c]

    def acc_tile(j):
        pvs = pv(j, None)
        for c in range(len(chains)):
            acc_scr[c] += pvs[c]

    _loop_pairs(prev, acc_tile)
    lam = _lambda(lam_ref)
    outs = []
    for c in range(len(chains)):
        acc = acc_scr[c]
        outs.append(acc[:, :HEAD_DIM] / acc[:, HEAD_DIM:HEAD_DIM + 1])
    for u in heads:
        od = outs[2 * u] - lam * outs[2 * u + 1]
        o_scr[hp * ATTN_HEADS_PER_STEP + u] = _subln(od, sw_ref[...])

    @pl.when(hp == N_HEADS // ATTN_HEADS_PER_STEP - 1)
    def _store():
        o_ref[...] = jnp.concatenate([o_scr[i] for i in range(N_HEADS)], axis=-1).astype(o_ref.dtype)


def _prompt_attention(qh, kh, vh, bias_tab, lam_vecs, subln_row, tq):
    _, bsz, t, _ = qh.shape
    assert t % tq == 0
    nq = t // tq
    hps = ATTN_HEADS_PER_STEP
    body = functools.partial(_pattn_body, tq=tq)
    return pl.pallas_call(
        body,
        grid=(bsz, nq, N_HEADS // hps),
        in_specs=[
            pl.BlockSpec((hps, None, tq, LANES), lambda b, i, h: (h, b, i, 0)),
            pl.BlockSpec((hps, None, t, LANES), lambda b, i, h: (h, b, 0, 0)),
            pl.BlockSpec((hps, None, t, LANES), lambda b, i, h: (h, b, 0, 0)),
            pl.BlockSpec((hps, 2, tq, tq), lambda b, i, h: (h, 0, 0, 0)),
            pl.BlockSpec((4, MAP_DIM), lambda b, i, h: (0, 0)),
            pl.BlockSpec((1, HEAD_DIM), lambda b, i, h: (0, 0)),
        ],
        out_specs=pl.BlockSpec((None, tq, MIX_HALF), lambda b, i, h: (b, i, 0)),
        out_shape=jax.ShapeDtypeStruct((bsz, t, MIX_HALF), BF16),
        scratch_shapes=[pltpu.VMEM((N_HEADS, tq, HEAD_DIM), F32),
                        pltpu.VMEM((2 * hps, tq, LANES), F32),
                        pltpu.VMEM((2 * hps, tq, LANES), F32)],
        compiler_params=_params(("parallel", "parallel", "arbitrary")),
        name="prompt_attn",
    )(qh, kh, vh, bias_tab, lam_vecs, subln_row)


def _prompt_bias_tables(rel_bias, tq):
    r = jnp.arange(-tq + 1, tq, dtype=jnp.int32)
    f0 = jnp.where((r >= 0)[:, None], _rel_bias_shifted(rel_bias, r), -jnp.inf)
    f1 = _rel_bias_shifted(rel_bias, r + tq)

    def toeplitz(f):
        period = 2 * tq
        g = jnp.concatenate([f[tq - 1::-1], f[:1], f[:tq - 1:-1]], axis=0)
        x = jnp.tile(g, (tq, 1))[:tq * (period - 1)].reshape(tq, period - 1, f.shape[-1])
        return x[:, :tq]

    return jnp.transpose(jnp.stack([toeplitz(f0), toeplitz(f1)]), (3, 0, 1, 2)) * LOG2E


def _sattn_body(pt_ref, q_ref, kn_ref, vn_ref, bnear_ref, bnew_ref, lam_ref, sw_ref,
                ck_hbm, cv_hbm, o_ref, kbuf, vbuf, sem, m_scr, l_scr, acc_scr, *, ch, n_chunks, n_seq):
    b = pl.program_id(0)
    c = pl.program_id(1)
    step = b * n_chunks + c
    slot = step % 2
    nr = q_ref.shape[0]
    t_new = nr // (2 * N_HEADS)

    def copies(bb, cc, sl):
        out = []
        for p in range(ch):
            pg = pt_ref[bb, cc * ch + p]
            out.append(pltpu.make_async_copy(ck_hbm.at[0, pg], kbuf.at[sl, p], sem.at[0, sl]))
            out.append(pltpu.make_async_copy(cv_hbm.at[0, pg], vbuf.at[sl, p], sem.at[1, sl]))
        return out

    @pl.when(step == 0)
    def _prime():
        for cp in copies(0, 0, 0):
            cp.start()

    @pl.when(step + 1 < n_seq * n_chunks)
    def _prefetch():
        nxt = step + 1
        for cp in copies(nxt // n_chunks, nxt % n_chunks, 1 - slot):
            cp.start()

    for cp in copies(b, c, slot):
        cp.wait()

    @pl.when(c == 0)
    def _init():
        m_scr[...] = jnp.full(m_scr.shape, -jnp.inf, F32)
        l_scr[...] = jnp.zeros(l_scr.shape, F32)
        acc_scr[...] = jnp.zeros(acc_scr.shape, F32)

    row = _iota((nr, MIX_HALF), 0)
    col = _iota((nr, MIX_HALF), 1)
    own = ((col // HEAD_DIM) == ((row % (nr // 2)) // t_new)) & (((col % HEAD_DIM) // MAP_DIM) == (row // (nr // 2)))
    own_head = (col // HEAD_DIM) == ((row % (nr // 2)) // t_new)
    qbd = jnp.where(own, q_ref[...], jnp.zeros((), BF16))

    is_last = c == n_chunks - 1
    rows_hd = N_HEADS * HEAD_DIM
    kt = jnp.concatenate([kbuf[slot, p].reshape(rows_hd, PAGE) for p in range(ch)], axis=-1).astype(BF16)
    s = _dot(qbd, kt)
    near = jnp.where(is_last, bnear_ref[...], 0.0)
    if ch > 1:
        s = jnp.concatenate([s[:, :(ch - 1) * PAGE], s[:, (ch - 1) * PAGE:] + near], axis=-1)
    else:
        s = s + near

    m_prev = m_scr[:, 0:1]
    l_prev = l_scr[:, 0:1]
    m_new = jnp.maximum(m_prev, jnp.max(s, axis=-1, keepdims=True))
    alpha = jnp.exp(m_prev - m_new)
    pr = jnp.exp(s - m_new)
    l_new = alpha * l_prev + jnp.sum(pr, axis=-1, keepdims=True)
    vt = jnp.concatenate([vbuf[slot, p].reshape(rows_hd, PAGE) for p in range(ch)], axis=-1).astype(BF16)
    acc = alpha * acc_scr[...] + _dot_nt(pr.astype(BF16), vt)
    m_scr[...] = jnp.broadcast_to(m_new, m_scr.shape)
    l_scr[...] = jnp.broadcast_to(l_new, l_scr.shape)
    acc_scr[...] = acc

    @pl.when(is_last)
    def _fin():
        sn = _dot_nt(qbd, kn_ref[...].astype(BF16)) + bnew_ref[...]
        m2 = jnp.maximum(m_new, jnp.max(sn, axis=-1, keepdims=True))
        a2 = jnp.exp(m_new - m2)
        p2 = jnp.exp(sn - m2)
        l2 = a2 * l_new + jnp.sum(p2, axis=-1, keepdims=True)
        acc2 = a2 * acc + _dot(p2.astype(BF16), vn_ref[...].astype(BF16))
        o_full = jnp.where(own_head, acc2 / l2, 0.0)
        fold = _one_hot_f((_iota((MIX_HALF, HEAD_DIM), 0) % HEAD_DIM) == _iota((MIX_HALF, HEAD_DIM), 1))
        o = _dot_x01(o_full, fold)
        half = nr // 2
        od = o[:half] - _lambda(lam_ref) * o[half:]
        o_ref[...] = _subln(od, sw_ref[...])


def _sample_attention(page_table, q_rep, k_new, v_new, cache_kt, cache_vt, bnear, bnew,
                      lam_vecs, subln_row, ch):
    n_seq, n_pages = page_table.shape
    assert n_pages % ch == 0
    n_chunks = n_pages // ch
    nr = q_rep.shape[1]
    t_pad = k_new.shape[1]
    body = functools.partial(_sattn_body, ch=ch, n_chunks=n_chunks, n_seq=n_seq)
    grid_spec = pltpu.PrefetchScalarGridSpec(
        num_scalar_prefetch=1,
        grid=(n_seq, n_chunks),
        in_specs=[
            pl.BlockSpec((None, nr, MIX_HALF), lambda b, c, pt: (b, 0, 0)),
            pl.BlockSpec((None, t_pad, MIX_HALF), lambda b, c, pt: (b, 0, 0)),
            pl.BlockSpec((None, t_pad, MIX_HALF), lambda b, c, pt: (b, 0, 0)),
            pl.BlockSpec((nr, PAGE), lambda b, c, pt: (0, 0)),
            pl.BlockSpec((nr, t_pad), lambda b, c, pt: (0, 0)),
            pl.BlockSpec((4, MAP_DIM), lambda b, c, pt: (0, 0)),
            pl.BlockSpec((1, HEAD_DIM), lambda b, c, pt: (0, 0)),
            pl.BlockSpec(memory_space=pl.ANY),
            pl.BlockSpec(memory_space=pl.ANY),
        ],
        out_specs=pl.BlockSpec((None, nr // 2, HEAD_DIM), lambda b, c, pt: (b, 0, 0)),
        scratch_shapes=[
            pltpu.VMEM((2, ch, N_HEADS, HEAD_DIM, PAGE), F32),
            pltpu.VMEM((2, ch, N_HEADS, HEAD_DIM, PAGE), F32),
            pltpu.SemaphoreType.DMA((2, 2)),
            pltpu.VMEM((nr, LANES), F32),
            pltpu.VMEM((nr, LANES), F32),
            pltpu.VMEM((nr, MIX_HALF), F32),
        ],
    )
    return pl.pallas_call(
        body,
        grid_spec=grid_spec,
        out_shape=jax.ShapeDtypeStruct((n_seq, nr // 2, HEAD_DIM), F32),
        compiler_params=_params(("arbitrary", "arbitrary")),
        name="sample_attn",
    )(page_table, q_rep, k_new, v_new, bnear, bnew, lam_vecs, subln_row, cache_kt, cache_vt)


def _sample_bias_tables(rel_bias, t_new, t_pad):
    H = N_HEADS
    nr = 2 * H * t_new
    row = jnp.arange(nr, dtype=jnp.int32)
    row_h = (row % (H * t_new)) // t_new
    row_t = row % t_new
    pick = (row_h[:, None] == jnp.arange(H, dtype=jnp.int32)[None, :])

    def table(rel):
        allh = _rel_bias_shifted(rel_bias, rel)
        return jnp.sum(jnp.where(pick[:, None, :], allh, 0.0), axis=-1)

    pos = jnp.arange(PAGE, dtype=jnp.int32)
    bnear = table(PAGE + row_t[:, None] - pos[None, :])
    tn = jnp.arange(t_pad, dtype=jnp.int32)
    reln = row_t[:, None] - tn[None, :]
    ok = (reln >= 0) & (tn[None, :] < t_new)
    bnew = jnp.where(ok, table(reln), -jnp.inf)
    return bnear.astype(F32), bnew.astype(F32)


def _outproj_body(x_ref, gdn_ref, diff_ref, wo_ref, nw_ref, wr_ref, x2_ref, hp_ref, route_ref):
    x2 = (x_ref[...] + _dot(gdn_ref[...], wo_ref[:MIX_HALF, :])
          + _dot(diff_ref[...], wo_ref[MIX_HALF:, :]))
    x2_ref[...] = x2
    h = x2 * lax.rsqrt(jnp.mean(x2 * x2, axis=-1, keepdims=True) + NORM_EPS) * nw_ref[...]
    h1, h2 = _split2(h)
    logits = _dot(h1, wr_ref[0]) + (_dot(h1, wr_ref[1]) + _dot(h2, wr_ref[0]))
    tm = logits.shape[0]
    lane = _iota((tm, LANES), 1)
    neg = -jnp.inf
    big = jnp.int32(10_000)
    gl = jnp.where(lane < N_ROUTE_GROUPS, logits, neg)
    gm = jnp.max(gl, axis=-1, keepdims=True)
    g_idx = jnp.min(jnp.where(gl == gm, lane, big), axis=-1, keepdims=True)
    g_w = 1.0 / jnp.sum(jnp.exp(gl - gm), axis=-1, keepdims=True)
    lo = N_ROUTE_GROUPS + EXPERTS_PER_GROUP * g_idx
    el = jnp.where((lane >= lo) & (lane < lo + EXPERTS_PER_GROUP), logits, neg)
    em = jnp.max(el, axis=-1, keepdims=True)
    i1 = jnp.min(jnp.where(el == em, lane, big), axis=-1, keepdims=True)
    el2 = jnp.where(lane == i1, neg, el)
    em2 = jnp.max(el2, axis=-1, keepdims=True)
    i2 = jnp.min(jnp.where(el2 == em2, lane, big), axis=-1, keepdims=True)
    e2 = jnp.exp(em2 - em)
    w1 = g_w / (1.0 + e2)
    w2 = g_w * e2 / (1.0 + e2)
    route = jnp.where(lane == 0, (i1 - N_ROUTE_GROUPS).astype(F32),
                      jnp.where(lane == 1, (i2 - N_ROUTE_GROUPS).astype(F32),
                                jnp.where(lane == 2, w1, jnp.where(lane == 3, w2, 0.0))))
    route_ref[...] = route
    hb = h.astype(BF16).astype(F32)
    lo_bits = pltpu.bitcast(hb[:, :D_MODEL // 2], jnp.uint32) >> 16
    hi_bits = pltpu.bitcast(hb[:, D_MODEL // 2:], jnp.uint32) & jnp.uint32(0xFFFF0000)
    hp_ref[...] = lo_bits | hi_bits


def _outproj(x2d, gdn2d, diff2d, w_out16, norm_w, wr_pair, tm):
    n = x2d.shape[0]
    assert n % tm == 0
    row = lambda i: (i, 0)
    fixed = lambda i: (0, 0)
    return pl.pallas_call(
        _outproj_body,
        grid=(n // tm,),
        in_specs=[
            pl.BlockSpec((tm, D_MODEL), row),
            pl.BlockSpec((tm, MIX_HALF), row),
            pl.BlockSpec((tm, MIX_HALF), row),
            pl.BlockSpec((D_MODEL, D_MODEL), fixed),
            pl.BlockSpec((1, D_MODEL), fixed),
            pl.BlockSpec((2, D_MODEL, LANES), lambda i: (0, 0, 0)),
        ],
        out_specs=(pl.BlockSpec((tm, D_MODEL), row),
                   pl.BlockSpec((tm, D_MODEL // 2), row),
                   pl.BlockSpec((tm, LANES), row)),
        out_shape=(jax.ShapeDtypeStruct((n, D_MODEL), F32),
                   jax.ShapeDtypeStruct((n, D_MODEL // 2), jnp.uint32),
                   jax.ShapeDtypeStruct((n, LANES), F32)),
        compiler_params=_params(("parallel",)),
        name="outproj_router",
    )(x2d, gdn2d, diff2d, w_out16, norm_w, wr_pair)


def _moe_plan(e_idx, w, tile, max_blocks):
    n = e_idx.shape[0]
    n_tiles = n // tile
    m = 2 * tile
    experts = jnp.arange(N_EXPERTS, dtype=jnp.int32)
    e = e_idx.reshape(n_tiles, m)
    hit = e[..., None] == experts
    onehot = hit.astype(jnp.int32)
    sub = math.gcd(m, MOE_BLOCK)
    tri = (jnp.arange(sub)[:, None] >= jnp.arange(sub)[None, :]).astype(BF16)
    within = jnp.einsum("ij,tcje->tcie", tri, hit.astype(BF16).reshape(n_tiles, m // sub, sub, N_EXPERTS),
                        preferred_element_type=F32)
    chunk_tot = within[:, :, -1, :]
    chunk_off = jnp.cumsum(chunk_tot, axis=1) - chunk_tot
    csum = (within + chunk_off[:, :, None, :]).reshape(n_tiles, m, N_EXPERTS).astype(jnp.int32)
    counts = csum[:, -1, :]
    nb_e = (counts + MOE_BLOCK - 1) // MOE_BLOCK
    pad_ends = jnp.cumsum(nb_e, axis=1)
    pad_starts = pad_ends - nb_e
    nblk = pad_ends[:, -1]
    dest = jnp.sum(onehot * (pad_starts[:, None, :] * MOE_BLOCK + csum - 1), axis=-1)
    blk = jnp.arange(max_blocks, dtype=jnp.int32)
    block_e = jnp.sum((pad_ends[:, None, :] <= blk[None, :, None]).astype(jnp.int32), axis=-1)
    block_e = jnp.minimum(block_e, N_EXPERTS - 1)
    oh_b = (block_e[..., None] == experts).astype(jnp.int32)
    cnt_b = jnp.sum(oh_b * counts[:, None, :], axis=-1)
    ps_b = jnp.sum(oh_b * pad_starts[:, None, :], axis=-1)
    live = blk[None, :] < nblk[:, None]
    nvalid = jnp.where(live, jnp.clip(cnt_b - (blk[None, :] - ps_b) * MOE_BLOCK, 0, MOE_BLOCK), 0)
    last_e = jnp.max(jnp.where(counts > 0, experts[None, :], 0), axis=-1, keepdims=True)
    block_e = jnp.where(live, block_e, last_e)
    return (block_e.reshape(-1).astype(jnp.int32), nblk.astype(jnp.int32), nvalid.reshape(-1).astype(jnp.int32),
            dest.astype(jnp.int32).reshape(n_tiles, 1, m), w.astype(F32).reshape(n_tiles, 1, m))


MOE_TOKEN_TILE = 4096
_SCATTER_BATCH = 4
_INVERT_BATCH = 8
_NORM_ROWS = 256


def _moe_body(be_ref, nblk_ref, nvalid_ref, hp_ref, dest_ref, wa_ref, wg_ref, wu_ref, wd_ref,
              nf_ref, x2_hbm, y_ref, asg_s, xg_scr, yb_scr, sem, *, max_blocks):
    t = pl.program_id(0)
    b = pl.program_id(1)
    m = dest_ref.shape[1]
    tile = y_ref.shape[0]

    @pl.when(b == 0)
    def _start_tile():
        residual = pltpu.make_async_copy(x2_hbm.at[pl.ds(pl.multiple_of(t * tile, tile), tile)], y_ref, sem.at[0])
        residual.start()
        xg_scr[...] = jnp.zeros(xg_scr.shape, xg_scr.dtype)

        def invert(ab, carry):
            a0 = ab * _INVERT_BATCH
            dests = [dest_ref[0, a0 + u] for u in range(_INVERT_BATCH)]
            for u in range(_INVERT_BATCH):
                asg_s[dests[u]] = a0 + u
            return carry

        lax.fori_loop(0, m // _INVERT_BATCH, invert, 0)
        residual.wait()

    @pl.when(b < nblk_ref[t])
    def _block():
        base = b * MOE_BLOCK
        nv = nvalid_ref[t * max_blocks + b]

        def gather_group(ig, carry):
            i0 = ig * SUBLANES
            toks = [asg_s[base + jnp.minimum(i0 + u, nv - 1)] >> 1 for u in range(SUBLANES)]
            rows = [hp_ref[pl.ds(toks[u], 1), :] for u in range(SUBLANES)]
            for u in range(SUBLANES):
                xg_scr[pl.ds(i0 + u, 1), :] = rows[u]
            return carry

        lax.fori_loop(0, (nv + SUBLANES - 1) // SUBLANES, gather_group, 0)
        xp = xg_scr[...]
        half = D_MODEL // 2
        x_lo = pltpu.bitcast(xp << 16, F32).astype(BF16)
        x_hi = pltpu.bitcast(xp & jnp.uint32(0xFFFF0000), F32).astype(BF16)
        gt = _dot(x_lo, wg_ref[:half, :]) + _dot(x_hi, wg_ref[half:, :])
        up = _dot(x_lo, wu_ref[:half, :]) + _dot(x_hi, wu_ref[half:, :])
        act = (jax.nn.silu(gt) * up).astype(BF16)
        yb_scr[...] = _dot(act, wd_ref[...])

        def scatter_batch(ib, carry):
            i0 = ib * _SCATTER_BATCH
            asgs = [asg_s[base + i0 + u] for u in range(_SCATTER_BATCH)]
            toks = [a >> 1 for a in asgs]
            vals = [y_ref[pl.ds(toks[u], 1), :] + wa_ref[0, asgs[u]] * yb_scr[pl.ds(i0 + u, 1), :]
                    for u in range(_SCATTER_BATCH)]
            for u in range(_SCATTER_BATCH):
                y_ref[pl.ds(toks[u], 1), :] = vals[u]
            return carry

        nfull = nv // _SCATTER_BATCH
        lax.fori_loop(0, nfull, scatter_batch, 0)

        def scatter_one(i, carry):
            a = asg_s[base + i]
            tok = a >> 1
            y_ref[pl.ds(tok, 1), :] = y_ref[pl.ds(tok, 1), :] + wa_ref[0, a] * yb_scr[pl.ds(i, 1), :]
            return carry

        lax.fori_loop(nfull * _SCATTER_BATCH, nv, scatter_one, 0)

    @pl.when(b == max_blocks - 1)
    def _final_norm():
        rows = min(tile, _NORM_ROWS)

        def norm_rows(i, carry):
            sl = pl.ds(pl.multiple_of(i * rows, rows), rows)
            x = y_ref[sl, :]
            y_ref[sl, :] = x * lax.rsqrt(jnp.mean(x * x, axis=-1, keepdims=True) + NORM_EPS) * nf_ref[...]
            return carry

        lax.fori_loop(0, tile // rows, norm_rows, 0)


def _moe(x2, hp, plan, wg16, wu16, wd16, norm_final_row, tile, max_blocks):
    n = hp.shape[0]
    n_tiles = n // tile
    block_e, nblk, nvalid, dest, wa = plan
    m = 2 * tile
    body = functools.partial(_moe_body, max_blocks=max_blocks)
    half = D_MODEL // 2
    wmap = lambda t, b, be, nb, nv: (be[t * max_blocks + b], 0, 0)
    tmap = lambda t, b, be, nb, nv: (t, 0, 0)
    grid_spec = pltpu.PrefetchScalarGridSpec(
        num_scalar_prefetch=3,
        grid=(n_tiles, max_blocks),
        in_specs=[
            pl.BlockSpec((tile, half), lambda t, b, be, nb, nv: (t, 0), pipeline_mode=pl.Buffered(1)),
            pl.BlockSpec((None, 1, m), tmap, memory_space=pltpu.SMEM),
            pl.BlockSpec((None, 1, m), tmap, memory_space=pltpu.SMEM),
            pl.BlockSpec((None, D_MODEL, EXPERT_FF), wmap),
            pl.BlockSpec((None, D_MODEL, EXPERT_FF), wmap),
            pl.BlockSpec((None, EXPERT_FF, D_MODEL), wmap),
            pl.BlockSpec((1, D_MODEL), lambda t, b, be, nb, nv: (0, 0)),
            pl.BlockSpec(memory_space=pl.ANY),
        ],
        out_specs=pl.BlockSpec((tile, D_MODEL), lambda t, b, be, nb, nv: (t, 0), pipeline_mode=pl.Buffered(1)),
        scratch_shapes=[pltpu.SMEM((max_blocks * MOE_BLOCK,), jnp.int32),
                        pltpu.VMEM((MOE_BLOCK, half), jnp.uint32),
                        pltpu.VMEM((MOE_BLOCK, D_MODEL), F32),
                        pltpu.SemaphoreType.DMA((1,))],
    )
    return pl.pallas_call(
        body,
        grid_spec=grid_spec,
        out_shape=jax.ShapeDtypeStruct((n, D_MODEL), F32),
        compiler_params=_params(("arbitrary", "arbitrary")),
        name="moe",
    )(block_e, nblk, nvalid, hp, dest, wa, wg16, wu16, wd16, norm_final_row, x2)


def _pick_tile(n, pref):
    t = min(n, pref)
    while n % t:
        t //= 2
    return t


def _ffn(x2d, gdn2d, diff2d, w_out16, norm_ffn_row, wr_pair, wg16, wu16, wd16, norm_final_row):
    n = x2d.shape[0]
    tm = _pick_tile(n, 512)
    x2, hp, route = _outproj(x2d, gdn2d, diff2d, w_out16, norm_ffn_row, wr_pair, tm)
    e_idx = route[:, 0:2].astype(jnp.int32)
    w = route[:, 2:4]
    tile = _pick_tile(n, MOE_TOKEN_TILE)
    max_blocks = 2 * tile // MOE_BLOCK + N_EXPERTS
    plan = _moe_plan(e_idx, w, tile, max_blocks)
    return _moe(x2, hp, plan, wg16, wu16, wd16, norm_final_row, tile, max_blocks)


def kernel(x_prompt, x_sample, cache_k, cache_v, state_conv, state_gdn, page_table, norm_mix, w_in, conv_w, gdn_a_log, gdn_dt_bias, gdn_norm, lambda_q1, lambda_k1, lambda_q2, lambda_k2, diff_subln, rel_bias, w_out, norm_ffn, router_group, router_expert, w_gate, w_up, w_down, norm_final):
    bsz, seq, _ = x_prompt.shape
    dbsz, dseq, _ = x_sample.shape
    H = N_HEADS
    l = 0

    wl = w_in[l]
    n_ab = 2 * H
    w_packed = jnp.concatenate(
        [wl[:, :_C_GATE[1]], wl[:, _C_GATE[1]:_C_GATE[1] + n_ab],
         jnp.zeros((D_MODEL, LANES - n_ab), wl.dtype), wl[:, _C_GATE[1] + n_ab:]], axis=1).astype(BF16)
    norm_mix_row = norm_mix[l].reshape(1, D_MODEL)
    pad_h = jnp.zeros((LANES - H,), F32)
    alog_row = jnp.concatenate([gdn_a_log[l].astype(F32), pad_h]).reshape(1, LANES)
    dtb_row = jnp.concatenate([gdn_dt_bias[l].astype(F32), pad_h]).reshape(1, LANES)
    gdn_norm_row = jnp.tile(gdn_norm[l].astype(F32), HEADS_PER_GROUP).reshape(1, GROUP_W)
    lam_vecs = jnp.stack([lambda_q1[l], lambda_k1[l], lambda_q2[l], lambda_k2[l]]).astype(F32)
    subln_row = diff_subln[l].astype(F32).reshape(1, HEAD_DIM)
    w_out16 = w_out[l].astype(BF16)
    norm_ffn_row = norm_ffn[l].reshape(1, D_MODEL)
    wr = jnp.concatenate([router_group[l], router_expert[l].reshape(D_MODEL, N_EXPERTS),
                          jnp.zeros((D_MODEL, LANES - N_ROUTE_GROUPS - N_EXPERTS), F32)], axis=1)
    wr1 = wr.astype(BF16)
    wr_pair = jnp.stack([wr1, (wr - wr1.astype(F32)).astype(BF16)])
    wg16 = w_gate[l].astype(BF16)
    wu16 = w_up[l].astype(BF16)
    wd16 = w_down[l].astype(BF16)
    norm_final_row = norm_final.reshape(1, D_MODEL)

    n_p = bsz * seq
    xp2d = x_prompt.reshape(n_p, D_MODEL)
    wkv_t = jnp.stack([wl[:, _C_GATE[1] + n_ab + MIX_HALF:_C_GATE[1] + n_ab + 2 * MIX_HALF].T,
                       wl[:, _C_GATE[1] + n_ab + 2 * MIX_HALF:].T]).astype(BF16)
    zqkv, zgate, zab, kf_t, vf_t, _, qh, kh, vh = _inproj(xp2d, norm_mix_row, w_packed, wkv_t,
                                                          _pick_tile(seq, 256), seq_len=seq)
    conv_zero = jnp.zeros((bsz, SUBLANES, CONV_DIM), F32)
    gdn_p, s_p = _gdn(zqkv.reshape(bsz, seq, CONV_DIM), zgate.reshape(bsz, seq, MIX_HALF),
                      zab.reshape(bsz, seq, LANES), conv_w[l], alog_row, dtb_row, gdn_norm_row,
                      conv_zero, None, seq, GDN_CHUNK)
    tq = 256
    bias_tab = _prompt_bias_tables(rel_bias, tq)
    hm = lambda a: a.reshape(H, bsz, seq, a.shape[-1])
    diff_p = _prompt_attention(hm(qh), hm(kh), hm(vh), bias_tab, lam_vecs, subln_row, tq)
    y_prompt = _ffn(xp2d, gdn_p.reshape(n_p, MIX_HALF), diff_p.reshape(n_p, MIX_HALF), w_out16,
                    norm_ffn_row, wr_pair, wg16, wu16, wd16, norm_final_row).reshape(bsz, seq, D_MODEL)
    tok_major = lambda a: jnp.transpose(a.reshape(bsz, H, HEAD_DIM, seq), (0, 3, 1, 2))[None]
    new_k_prompt = tok_major(kf_t)
    new_v_prompt = tok_major(vf_t)
    new_conv_prompt = zqkv.reshape(bsz, seq, CONV_DIM)[:, seq - (CONV_W - 1):, :][None]
    new_gdn_prompt = s_p[None]

    n_s = dbsz * dseq
    xs2d = x_sample.reshape(n_s, D_MODEL)
    zqkv_s, zgate_s, zab_s, kf_s, vf_s, qf_s, _, _, _ = _inproj(xs2d, norm_mix_row, w_packed, wkv_t,
                                                                _pick_tile(n_s, 256))
    C = GDN_CHUNK_SAMPLE
    assert dseq <= C
    padt = lambda a: jnp.pad(a.reshape(dbsz, dseq, -1), ((0, 0), (0, C - dseq), (0, 0)))
    conv_init = jnp.pad(state_conv[l].astype(F32), ((0, 0), (SUBLANES - (CONV_W - 1), 0), (0, 0)))
    gdn_s, s_s = _gdn(padt(zqkv_s), padt(zgate_s), padt(zab_s), conv_w[l], alog_row, dtb_row,
                      gdn_norm_row, conv_init, state_gdn[l].astype(F32), dseq, C)
    gdn_s = gdn_s[:, :dseq, :].reshape(n_s, MIX_HALF)
    t_pad = BF16_ROWS
    q_rep = jnp.broadcast_to(qf_s.reshape(dbsz, 1, 1, dseq, MIX_HALF),
                             (dbsz, 2, H, dseq, MIX_HALF)).reshape(dbsz, 2 * H * dseq, MIX_HALF)
    newpad = lambda a: jnp.pad(a.reshape(dbsz, dseq, MIX_HALF), ((0, 0), (0, t_pad - dseq), (0, 0)))
    bnear, bnew = _sample_bias_tables(rel_bias, dseq, t_pad)
    cache_kt = jnp.transpose(cache_k, (0, 1, 3, 4, 2))
    cache_vt = jnp.transpose(cache_v, (0, 1, 3, 4, 2))
    n_pages = page_table.shape[1]
    ch = 16 if n_pages % 16 == 0 else 1
    diff_s = _sample_attention(page_table, q_rep, newpad(kf_s), newpad(vf_s), cache_kt, cache_vt,
                               bnear, bnew, lam_vecs, subln_row, ch)
    diff_s = jnp.transpose(diff_s.reshape(dbsz, H, dseq, HEAD_DIM), (0, 2, 1, 3)).reshape(n_s, MIX_HALF)
    y_sample = _ffn(xs2d, gdn_s, diff_s.astype(BF16), w_out16, norm_ffn_row, wr_pair, wg16, wu16, wd16,
                    norm_final_row).reshape(dbsz, dseq, D_MODEL)
    new_k_sample = kf_s.reshape(1, dbsz, dseq, H, HEAD_DIM)
    new_v_sample = vf_s.reshape(1, dbsz, dseq, H, HEAD_DIM)
    zq3 = zqkv_s.reshape(dbsz, dseq, CONV_DIM)
    new_conv_sample = jnp.concatenate([state_conv[l].astype(F32), zq3], axis=1)[:, -(CONV_W - 1):, :][None]
    new_gdn_sample = s_s[None]

    return (y_prompt, y_sample, new_k_prompt, new_v_prompt, new_k_sample, new_v_sample,
            new_conv_prompt, new_conv_sample, new_gdn_prompt, new_gdn_sample)
```

```python
import functools
import math

import jax
import jax.numpy as jnp
from jax import lax
from jax.experimental import pallas as pl
from jax.experimental.pallas import tpu as pltpu

F32 = jnp.float32
BF16 = jnp.bfloat16

D_MODEL = 1024
N_HEADS = 8
HEAD_DIM = 64
MAP_DIM = 32
MIX_HALF = N_HEADS * HEAD_DIM
CONV_W = 4
CONV_DIM = 3 * MIX_HALF
GDN_CHUNK = 64
GDN_CHUNK_SAMPLE = 16
HEADS_PER_GROUP = 4
GROUP_W = HEADS_PER_GROUP * HEAD_DIM
N_GROUPS_GDN = N_HEADS // HEADS_PER_GROUP
PAGE = 128
N_BUCKETS = 32
MAX_DISTANCE = 128
N_ROUTE_GROUPS = 4
EXPERTS_PER_GROUP = 8
N_EXPERTS = N_ROUTE_GROUPS * EXPERTS_PER_GROUP
EXPERT_FF = D_MODEL // 2
MOE_BLOCK = 128
NORM_EPS = 1e-6
SUBLN_EPS = 1e-5
LAMBDA_INIT = 0.8 - 0.6 * math.exp(-0.3 * 0)
LOG2E = math.log2(math.e)

LANES = 128
SUBLANES = 8
BF16_ROWS = 16
VMEM_LIMIT = 48 * 1024 * 1024

_C_QKV = (0, CONV_DIM)
_C_GATE = (_C_QKV[1], _C_QKV[1] + MIX_HALF)
_C_AB = (_C_GATE[1], _C_GATE[1] + LANES)
_C_DQ = (_C_AB[1], _C_AB[1] + MIX_HALF)
_C_DK = (_C_DQ[1], _C_DQ[1] + MIX_HALF)
_C_DV = (_C_DK[1], _C_DK[1] + MIX_HALF)
IN_COLS_PACKED = _C_DV[1]


def _params(sem, vmem=VMEM_LIMIT):
    return pltpu.CompilerParams(dimension_semantics=sem, vmem_limit_bytes=vmem)


def _dot(a, b):
    return jnp.dot(a, b, preferred_element_type=F32)


def _dot_nt(a, b):
    return lax.dot_general(a, b, (((1,), (1,)), ((), ())), preferred_element_type=F32)


def _dot_tn(a, b):
    return lax.dot_general(a, b, (((0,), (0,)), ((), ())), preferred_element_type=F32)


def _split2(x):
    x1 = x.astype(BF16)
    x2 = (x - x1.astype(F32)).astype(BF16)
    return x1, x2


def _split3(x):
    x1 = x.astype(BF16)
    r = x - x1.astype(F32)
    x2 = r.astype(BF16)
    x3 = (r - x2.astype(F32)).astype(BF16)
    return x1, x2, x3


def _dot_x01(x, m01):
    x1, x2, x3 = _split3(x)
    return _dot(x1, m01) + (_dot(x2, m01) + _dot(x3, m01))


def _dot_01x(m01, x):
    x1, x2, x3 = _split3(x)
    return _dot(m01, x1) + (_dot(m01, x2) + _dot(m01, x3))


def _iota(shape, dim):
    return lax.broadcasted_iota(jnp.int32, shape, dim)


def _one_hot_f(cond):
    return jnp.where(cond, 1.0, 0.0).astype(BF16)


def _inproj_body(x_ref, nw_ref, w_ref, wkvt_ref, zqkv_ref, zgate_ref, zab_ref, kf_ref, vf_ref,
                 qf_ref, qh_ref, kh_ref, vh_ref, *, kv_token_minor):
    x = x_ref[...]
    h = x * lax.rsqrt(jnp.mean(x * x, axis=-1, keepdims=True) + NORM_EPS) * nw_ref[...]
    hb = h.astype(BF16)

    def proj(cols):
        return _dot(hb, w_ref[:, cols[0]:cols[1]])

    zqkv_ref[...] = proj(_C_QKV)
    zgate_ref[...] = proj(_C_GATE)
    zab_ref[...] = proj(_C_AB)
    zq = proj(_C_DQ) * (MAP_DIM ** -0.5)
    zk = proj(_C_DK)
    zv = proj(_C_DV)
    if kv_token_minor:
        kf_ref[...] = _dot_nt(wkvt_ref[0], hb)
        vf_ref[...] = _dot_nt(wkvt_ref[1], hb)
    else:
        kf_ref[...] = zk
        vf_ref[...] = zv
    qf_ref[...] = zq.astype(BF16)
    tm = x.shape[0]
    zq2 = zq * LOG2E
    zeros = jnp.zeros((tm, HEAD_DIM), BF16)
    ones = jnp.ones((tm, HEAD_DIM), BF16)
    two_ones = jnp.where(_iota((tm, HEAD_DIM), 1) < 2, 1.0, 0.0).astype(BF16)
    for h_i in range(N_HEADS):
        sl = slice(HEAD_DIM * h_i, HEAD_DIM * (h_i + 1))
        qh_ref[h_i] = jnp.concatenate([zq2[:, sl].astype(BF16), zeros], axis=-1)
        kh_ref[h_i] = jnp.concatenate([zk[:, sl].astype(BF16), two_ones], axis=-1)
        vh_ref[h_i] = jnp.concatenate([zv[:, sl].astype(BF16), ones], axis=-1)


def _inproj(x2d, norm_w, w_packed, wkv_t, tm, seq_len=None):
    n = x2d.shape[0]
    assert n % tm == 0
    row = lambda i: (i, 0)
    fixed = lambda i: (0, 0)
    hm = lambda i: (0, i, 0)
    kv_token_minor = seq_len is not None
    if kv_token_minor:
        assert seq_len % tm == 0 and n % seq_len == 0
        tiles = seq_len // tm
        kv_shape = jax.ShapeDtypeStruct((n // seq_len, MIX_HALF, seq_len), F32)
        kv_spec = pl.BlockSpec((None, MIX_HALF, tm), lambda i: (i // tiles, 0, i % tiles))
    else:
        kv_shape = jax.ShapeDtypeStruct((n, MIX_HALF), F32)
        kv_spec = pl.BlockSpec((tm, MIX_HALF), row)
    out_shape = (
        jax.ShapeDtypeStruct((n, CONV_DIM), F32),
        jax.ShapeDtypeStruct((n, MIX_HALF), F32),
        jax.ShapeDtypeStruct((n, LANES), F32),
        kv_shape,
        kv_shape,
        jax.ShapeDtypeStruct((n, MIX_HALF), BF16),
        jax.ShapeDtypeStruct((N_HEADS, n, LANES), BF16),
        jax.ShapeDtypeStruct((N_HEADS, n, LANES), BF16),
        jax.ShapeDtypeStruct((N_HEADS, n, LANES), BF16),
    )
    out_specs = (
        pl.BlockSpec((tm, CONV_DIM), row),
        pl.BlockSpec((tm, MIX_HALF), row),
        pl.BlockSpec((tm, LANES), row),
        kv_spec,
        kv_spec,
        pl.BlockSpec((tm, MIX_HALF), row),
        pl.BlockSpec((N_HEADS, tm, LANES), hm),
        pl.BlockSpec((N_HEADS, tm, LANES), hm),
        pl.BlockSpec((N_HEADS, tm, LANES), hm),
    )
    return pl.pallas_call(
        functools.partial(_inproj_body, kv_token_minor=kv_token_minor),
        grid=(n // tm,),
        in_specs=[pl.BlockSpec((tm, D_MODEL), row),
                  pl.BlockSpec((1, D_MODEL), fixed),
                  pl.BlockSpec((D_MODEL, IN_COLS_PACKED), fixed),
                  pl.BlockSpec((2, MIX_HALF, D_MODEL), lambda i: (0, 0, 0))],
        out_specs=out_specs,
        out_shape=out_shape,
        compiler_params=_params(("parallel",)),
        name="inproj",
    )(x2d, norm_w, w_packed, wkv_t)


def _gdn_body(z_ref, zg_ref, zab_ref, cw_ref, alog_ref, dtb_ref, nw_ref, ci_ref, s0_ref,
              o_ref, sfin_ref, s_scr, prev_scr, *, t_valid, zero_init, nb, chunk):
    C = chunk
    W = GROUP_W
    WC = HEADS_PER_GROUP * C
    c = pl.program_id(1)
    n_chunks = pl.num_programs(1)

    rw = _iota((W, W), 0)
    cw = _iota((W, W), 1)
    blk = (rw // HEAD_DIM) == (cw // HEAD_DIM)
    ones_blk = _one_hot_f(blk)
    mask_kd = (_iota((WC, W), 0) // C) == (_iota((WC, W), 1) // HEAD_DIM)
    mask_cc = (_iota((WC, WC), 0) // C) == (_iota((WC, WC), 1) // C)
    ri = _iota((C, WC), 0)
    cj = _iota((C, WC), 1) % C
    incl = ri >= cj
    strict = ri > cj
    eye_all = ri == cj
    eye_f = jnp.where(eye_all, 1.0, 0.0)
    zero16 = jnp.zeros((), BF16)

    @pl.when(c == 0)
    def _init():
        prev_scr[...] = ci_ref[...]
        for sg in range(nb * N_GROUPS_GDN):
            bb, g = divmod(sg, N_GROUPS_GDN)
            if zero_init:
                s_scr[sg] = jnp.zeros((W, W), F32)
            else:
                s0 = s0_ref[bb, g * HEADS_PER_GROUP:(g + 1) * HEADS_PER_GROUP].reshape(W, HEAD_DIM)
                et = _one_hot_f(_iota((HEAD_DIM, W), 0) == (_iota((HEAD_DIM, W), 1) % HEAD_DIM))
                s_scr[sg] = jnp.where(blk, _dot_x01(s0, et), 0.0)

    tri = _one_hot_f(_iota((C, C), 0) >= _iota((C, C), 1))

    def blocksum(v):
        return _dot(v.astype(BF16), ones_blk)

    def bd(x16):
        return jnp.where(mask_kd, jnp.concatenate([x16] * HEADS_PER_GROUP, axis=0), zero16)

    def bd_cc(x16):
        return jnp.where(mask_cc, jnp.concatenate([x16] * HEADS_PER_GROUP, axis=0), zero16)

    seqs = range(nb)
    chains = [(bb, g) for bb in seqs for g in range(N_GROUPS_GDN)]

    def each(fn, *lists):
        return [fn(*args) for args in zip(*lists)]

    w = cw_ref[...]
    xc = [jnp.concatenate([prev_scr[bb], z_ref[bb]], axis=0) for bb in seqs]
    y = each(lambda a: (a * w[3:4] + pltpu.roll(a, 1, 0) * w[2:3]
                        + pltpu.roll(a, 2, 0) * w[1:2] + pltpu.roll(a, 3, 0) * w[0:1]), xc)
    for bb in seqs:
        prev_scr[bb] = xc[bb][C:C + SUBLANES]
    qkv = each(lambda a: jax.nn.silu(a[SUBLANES:SUBLANES + C]), y)

    valid = (_iota((C, LANES), 0) + c * C) < t_valid
    zab = [zab_ref[bb] for bb in seqs]
    neg_a = -jnp.exp(alog_ref[...])
    gfull = each(lambda a: jnp.where(valid, neg_a * jax.nn.softplus(a + dtb_ref[...]), 0.0), zab)
    bfull = each(lambda a: jnp.where(valid, jax.nn.sigmoid(a), 0.0), zab)
    gcum_s = each(lambda a: _dot_01x(tri, a), gfull)

    lane_r = _iota((LANES, W), 0)
    head_c = _iota((LANES, W), 1) // HEAD_DIM
    e_g = [_one_hot_f(lane_r == head_c + g * HEADS_PER_GROUP) for g in range(N_GROUPS_GDN)]
    e_b = [_one_hot_f(lane_r == head_c + g * HEADS_PER_GROUP + N_HEADS) for g in range(N_GROUPS_GDN)]

    def cols(part, g):
        return slice(part * MIX_HALF + g * W, part * MIX_HALF + (g + 1) * W)

    q = [qkv[bb][:, cols(0, g)] for bb, g in chains]
    k = [qkv[bb][:, cols(1, g)] for bb, g in chains]
    v = [qkv[bb][:, cols(2, g)] for bb, g in chains]
    q = each(lambda a: a * lax.rsqrt(blocksum(a * a) + 1e-6) * (HEAD_DIM ** -0.5), q)
    k = each(lambda a: a * lax.rsqrt(blocksum(a * a) + 1e-6), k)
    gexp = [_dot_x01(gcum_s[bb], e_g[g]) for bb, g in chains]
    bexp = [_dot_x01(bfull[bb], e_b[g]) for bb, g in chains]
    if WC == W:
        gexp_c = gexp
    else:
        lane_rc = _iota((LANES, WC), 0)
        head_cc = _iota((LANES, WC), 1) // C
        e_gc = [_one_hot_f(lane_rc == head_cc + g * HEADS_PER_GROUP) for g in range(N_GROUPS_GDN)]
        gexp_c = [_dot_x01(gcum_s[bb], e_gc[g]) for bb, g in chains]
    grow = each(lambda a: jnp.sum(jnp.where(eye_all, a, 0.0), axis=0, keepdims=True), gexp_c)
    decay = each(lambda a, r: jnp.where(incl, jnp.exp(jnp.where(incl, a - r, 0.0)), 0.0), gexp_c, grow)

    kb = each(lambda a, b: a * b, k, bexp)
    kkqk = each(lambda a, b, kk: _dot_nt(jnp.concatenate([a, b], axis=0).astype(BF16), bd(kk.astype(BF16))),
                kb, q, k)
    lmat = each(lambda a, d: jnp.where(strict, a[:C] * d, 0.0), kkqk, decay)
    qkd = each(lambda a, d: a[C:] * d, kkqk, decay)

    tinv = each(lambda a: eye_f - a, lmat)
    p16 = each(lambda a: a.astype(BF16), lmat)
    bp = each(bd_cc, p16)
    for _ in range(int(math.log2(C)) - 1):
        p16 = each(lambda a, b: _dot(a, b).astype(BF16), p16, bp)
        bp = each(bd_cc, p16)
        tinv = each(lambda t, b: t + _dot(t.astype(BF16), b), tinv, bp)

    eg = each(jnp.exp, gexp)
    vb = each(lambda a, b: a * b, v, bexp)
    kbg = each(lambda a, b: a * b, kb, eg)
    corr = each(lambda t, a, b: _dot((t - eye_f).astype(BF16),
                                     jnp.concatenate([bd(a.astype(BF16)), bd(b.astype(BF16))], axis=1)),
                tinv, vb, kbg)
    u = each(lambda a, cr: a + cr[:, :W], vb, corr)
    wm = each(lambda a, cr: a + cr[:, W:], kbg, corr)
    s = [s_scr[bb * N_GROUPS_GDN + g] for bb, g in chains]
    s16 = each(lambda a: a.astype(BF16), s)
    v_new = each(lambda a, b, st: a - _dot(b.astype(BF16), st), u, wm, s16)
    o = each(lambda a, e, st, d, vn: _dot((a * e).astype(BF16), st) + _dot(d.astype(BF16), bd(vn.astype(BF16))),
             q, eg, s16, qkd, v_new)
    glast = each(lambda a: a[C - 1:C, :], gexp)
    kd = each(lambda a, gl, ge: a * jnp.exp(gl - ge), k, glast, gexp)
    upd = each(lambda a, vn: _dot_tn(a.astype(BF16), vn.astype(BF16)), kd, v_new)
    s_new = each(lambda st, gl, up: st * jnp.exp(gl) + jnp.where(blk, up, 0.0), s, glast, upd)
    for (bb, g), st in zip(chains, s_new):
        s_scr[bb * N_GROUPS_GDN + g] = st

    on = each(lambda a: a * lax.rsqrt(blocksum(a * a) * (1.0 / HEAD_DIM) + NORM_EPS) * nw_ref[...], o)
    for (bb, g), a in zip(chains, on):
        o_ref[bb, :, cols(0, g)] = (a * jax.nn.silu(zg_ref[bb, :, cols(0, g)])).astype(o_ref.dtype)

    @pl.when(c == n_chunks - 1)
    def _fin():
        ett = _one_hot_f((_iota((W, HEAD_DIM), 0) % HEAD_DIM) == _iota((W, HEAD_DIM), 1))
        for sg in range(nb * N_GROUPS_GDN):
            bb, g = divmod(sg, N_GROUPS_GDN)
            sout = _dot_x01(s_scr[sg], ett)
            sfin_ref[bb, g * HEADS_PER_GROUP:(g + 1) * HEADS_PER_GROUP] = sout.reshape(
                HEADS_PER_GROUP, HEAD_DIM, HEAD_DIM)


def _gdn(zqkv, zgate, zab, conv_w, alog_row, dtb_row, norm_row, conv_init, s0, t_valid, C):
    bsz, t, _ = zqkv.shape
    assert t % C == 0 and C % BF16_ROWS == 0
    n_chunks = t // C
    nb = 4 if bsz % 4 == 0 else (2 if bsz % 2 == 0 else 1)
    zero_init = s0 is None
    if zero_init:
        s0 = jnp.zeros((nb, N_HEADS, HEAD_DIM, HEAD_DIM), F32)
        s0_map = lambda b, c: (0, 0, 0, 0)
    else:
        s0_map = lambda b, c: (b, 0, 0, 0)
    chunk = lambda b, c: (b, c, 0)
    fixed = lambda b, c: (0, 0)
    in_specs = [
        pl.BlockSpec((nb, C, CONV_DIM), chunk),
        pl.BlockSpec((nb, C, MIX_HALF), chunk),
        pl.BlockSpec((nb, C, LANES), chunk),
        pl.BlockSpec((CONV_W, CONV_DIM), fixed),
        pl.BlockSpec((1, LANES), fixed),
        pl.BlockSpec((1, LANES), fixed),
        pl.BlockSpec((1, GROUP_W), fixed),
        pl.BlockSpec((nb, SUBLANES, CONV_DIM), lambda b, c: (b, 0, 0)),
        pl.BlockSpec((nb, N_HEADS, HEAD_DIM, HEAD_DIM), s0_map),
    ]
    out_specs = (
        pl.BlockSpec((nb, C, MIX_HALF), chunk),
        pl.BlockSpec((nb, N_HEADS, HEAD_DIM, HEAD_DIM), lambda b, c: (b, 0, 0, 0)),
    )
    out_shape = (
        jax.ShapeDtypeStruct((bsz, t, MIX_HALF), BF16),
        jax.ShapeDtypeStruct((bsz, N_HEADS, HEAD_DIM, HEAD_DIM), F32),
    )
    body = functools.partial(_gdn_body, t_valid=t_valid, zero_init=zero_init, nb=nb, chunk=C)
    return pl.pallas_call(
        body,
        grid=(bsz // nb, n_chunks),
        in_specs=in_specs,
        out_specs=out_specs,
        out_shape=out_shape,
        scratch_shapes=[pltpu.VMEM((nb * N_GROUPS_GDN, GROUP_W, GROUP_W), F32),
                        pltpu.VMEM((nb, SUBLANES, CONV_DIM), F32)],
        compiler_params=_params(("parallel", "arbitrary")),
        name="gdn",
    )(zqkv, zgate, zab, conv_w, alog_row, dtb_row, norm_row, conv_init, s0)


def _t5_bucket(rel):
    n = jnp.maximum(rel, 0)
    max_exact = N_BUCKETS // 2
    large = max_exact + (jnp.log(jnp.maximum(n, 1).astype(F32) / max_exact)
                         / math.log(MAX_DISTANCE / max_exact) * (N_BUCKETS - max_exact)).astype(jnp.int32)
    large = jnp.minimum(large, N_BUCKETS - 1)
    return jnp.where(n < max_exact, n, large)


def _rel_bias_shifted(rel_bias, rel):
    rb = rel_bias.astype(F32)
    sel = _t5_bucket(rel)[..., None] == jnp.arange(N_BUCKETS, dtype=jnp.int32)
    picked = jnp.sum(jnp.where(sel[..., None], rb, 0.0), axis=-2)
    return picked - rb[N_BUCKETS - 1]


def _lambda(lam_ref):
    lam = lam_ref[...]
    a = jnp.exp(jnp.sum(lam[0:1] * lam[1:2], axis=-1, keepdims=True))
    b = jnp.exp(jnp.sum(lam[2:3] * lam[3:4], axis=-1, keepdims=True))
    return a - b + LAMBDA_INIT


def _subln(o, w_row):
    ms = jnp.mean(o * o, axis=-1, keepdims=True)
    return o * lax.rsqrt(ms + SUBLN_EPS) * w_row * (1.0 - LAMBDA_INIT)


def _loop_pairs(n, body):
    def two(i, carry):
        body(2 * i)
        body(2 * i + 1)
        return carry

    def one(j, carry):
        body(j)
        return carry

    lax.fori_loop(0, n // 2, two, 0)
    lax.fori_loop((n // 2) * 2, n, one, 0)


ATTN_HEADS_PER_STEP = 8


def _pattn_body(q_ref, k_ref, v_ref, bias_ref, lam_ref, sw_ref, o_ref, o_scr, mv_scr, acc_scr, *, tq):
    qi = pl.program_id(1)
    hp = pl.program_id(2)
    tk = tq
    heads = range(ATTN_HEADS_PER_STEP)
    prev = jnp.maximum(qi - 1, 0)
    prev_off = jnp.where(qi > 0, 0.0, -jnp.inf)
    zero = jnp.zeros((), BF16)

    def fold(s):
        out = s[:, :LANES]
        for cidx in range(1, tk // LANES):
            out = jnp.maximum(out, s[:, cidx * LANES:(cidx + 1) * LANES])
        return out

    def k_tile(u, j):
        return k_ref[u, pl.ds(pl.multiple_of(j * tk, tk), tk), :]

    def v_tile(u, j):
        return v_ref[u, pl.ds(pl.multiple_of(j * tk, tk), tk), :]

    def both(x):
        return jnp.concatenate([x, x], axis=0)

    lane = _iota((tq, LANES), 1)
    chains = [(u, mp) for u in heads for mp in range(2)]
    map_lanes = [lane < MAP_DIM, lane >= MAP_DIM]
    q = [q_ref[u] for u in heads]
    qm = [jnp.where(map_lanes[mp], q[u], zero) for u, mp in chains]
    b0 = [bias_ref[u, 0] for u in heads]
    b1 = [bias_ref[u, 1] + prev_off for u in heads]

    s_diag = [_dot_nt(qm[c], k_tile(u, qi)) for c, (u, _) in enumerate(chains)]
    s_prev = [_dot_nt(qm[c], k_tile(u, prev)) for c, (u, _) in enumerate(chains)]
    for c, (u, _) in enumerate(chains):
        mv_scr[c] = jnp.maximum(fold(s_diag[c] + b0[u]), fold(s_prev[c] + b1[u]))

    def max_tile(j):
        ss = [_dot_nt(qm[c], k_tile(u, j)) for c, (u, _) in enumerate(chains)]
        for c in range(len(chains)):
            mv_scr[c] = jnp.maximum(mv_scr[c], fold(ss[c]))

    _loop_pairs(prev, max_tile)
    m = [jnp.max(mv_scr[c], axis=-1, keepdims=True) for c in range(len(chains))]
    m_hi = [a.astype(BF16) for a in m]
    m_lo = [(a - b.astype(F32)).astype(BF16) for a, b in zip(m, m_hi)]
    q2 = [jnp.where(lane == HEAD_DIM, -m_hi[c], jnp.where(lane == HEAD_DIM + 1, -m_lo[c], qm[c]))
          for c in range(len(chains))]

    def pv(j, bias):
        ss = [_dot_nt(q2[c], k_tile(u, j)) for c, (u, _) in enumerate(chains)]
        if bias is not None:
            ss = [ss[c] + bias[u] for c, (u, _) in enumerate(chains)]
        ps = [jnp.exp2(a).astype(BF16) for a in ss]
        return [_dot(ps[c], v_tile(u, j)) for c, (u, _) in enumerate(chains)]

    pv_diag = pv(qi, b0)
    pv_prev = pv(prev, b1)
    for c in range(len(chains)):
        acc_scr[c] = pv_diag[c] + pv_prev[c]

    def acc_tile(j):
        pvs = pv(j, None)
        for c in range(len(chains)):
            acc_scr[c] += pvs[c]

    _loop_pairs(prev, acc_tile)
    lam = _lambda(lam_ref)
    outs = []
    for c in range(len(chains)):
        acc = acc_scr[c]
        outs.append(acc[:, :HEAD_DIM] / acc[:, HEAD_DIM:HEAD_DIM + 1])
    for u in heads:
        od = outs[2 * u] - lam * outs[2 * u + 1]
        o_scr[hp * ATTN_HEADS_PER_STEP + u] = _subln(od, sw_ref[...])

    @pl.when(hp == N_HEADS // ATTN_HEADS_PER_STEP - 1)
    def _store():
        o_ref[...] = jnp.concatenate([o_scr[i] for i in range(N_HEADS)], axis=-1).astype(o_ref.dtype)


def _prompt_attention(qh, kh, vh, bias_tab, lam_vecs, subln_row, tq):
    _, bsz, t, _ = qh.shape
    assert t % tq == 0
    nq = t // tq
    hps = ATTN_HEADS_PER_STEP
    body = functools.partial(_pattn_body, tq=tq)
    return pl.pallas_call(
        body,
        grid=(bsz, nq, N_HEADS // hps),
        in_specs=[
            pl.BlockSpec((hps, None, tq, LANES), lambda b, i, h: (h, b, i, 0)),
            pl.BlockSpec((hps, None, t, LANES), lambda b, i, h: (h, b, 0, 0)),
            pl.BlockSpec((hps, None, t, LANES), lambda b, i, h: (h, b, 0, 0)),
            pl.BlockSpec((hps, 2, tq, tq), lambda b, i, h: (h, 0, 0, 0)),
            pl.BlockSpec((4, MAP_DIM), lambda b, i, h: (0, 0)),
            pl.BlockSpec((1, HEAD_DIM), lambda b, i, h: (0, 0)),
        ],
        out_specs=pl.BlockSpec((None, tq, MIX_HALF), lambda b, i, h: (b, i, 0)),
        out_shape=jax.ShapeDtypeStruct((bsz, t, MIX_HALF), BF16),
        scratch_shapes=[pltpu.VMEM((N_HEADS, tq, HEAD_DIM), F32),
                        pltpu.VMEM((2 * hps, tq, LANES), F32),
                        pltpu.VMEM((2 * hps, tq, LANES), F32)],
        compiler_params=_params(("parallel", "parallel", "arbitrary")),
        name="prompt_attn",
    )(qh, kh, vh, bias_tab, lam_vecs, subln_row)


def _prompt_bias_tables(rel_bias, tq):
    r = jnp.arange(-tq + 1, tq, dtype=jnp.int32)
    f0 = jnp.where((r >= 0)[:, None], _rel_bias_shifted(rel_bias, r), -jnp.inf)
    f1 = _rel_bias_shifted(rel_bias, r + tq)

    def toeplitz(f):
        period = 2 * tq
        g = jnp.concatenate([f[tq - 1::-1], f[:1], f[:tq - 1:-1]], axis=0)
        x = jnp.tile(g, (tq, 1))[:tq * (period - 1)].reshape(tq, period - 1, f.shape[-1])
        return x[:, :tq]

    return jnp.transpose(jnp.stack([toeplitz(f0), toeplitz(f1)]), (3, 0, 1, 2)) * LOG2E


SAMPLE_PAGE_SLOTS = 3


def _sattn_body(pt_ref, q_ref, kn_ref, vn_ref, bnear_ref, bnew_ref, lam_ref, sw_ref,
                ck_hbm, cv_hbm, o_ref, kbuf, vbuf, sem, m_scr, l_scr, acc_scr, *, ch, n_chunks, n_seq):
    b = pl.program_id(0)
    c = pl.program_id(1)
    step = b * n_chunks + c
    n_slots = kbuf.shape[0]
    total = n_seq * n_chunks
    slot = step % n_slots
    nr = q_ref.shape[0]
    t_new = nr // (2 * N_HEADS)

    def copies(bb, cc, sl):
        out = []
        for p in range(ch):
            pg = pt_ref[bb, cc * ch + p]
            out.append(pltpu.make_async_copy(ck_hbm.at[0, pg], kbuf.at[sl, p], sem.at[0, sl]))
            out.append(pltpu.make_async_copy(cv_hbm.at[0, pg], vbuf.at[sl, p], sem.at[1, sl]))
        return out

    @pl.when(step == 0)
    def _prime():
        for s0 in range(min(n_slots - 1, total)):
            for cp in copies(s0 // n_chunks, s0 % n_chunks, s0):
                cp.start()

    @pl.when(step + (n_slots - 1) < total)
    def _prefetch():
        nxt = step + (n_slots - 1)
        for cp in copies(nxt // n_chunks, nxt % n_chunks, nxt % n_slots):
            cp.start()

    for cp in copies(b, c, slot):
        cp.wait()

    @pl.when(c == 0)
    def _init():
        m_scr[...] = jnp.full(m_scr.shape, -jnp.inf, F32)
        l_scr[...] = jnp.zeros(l_scr.shape, F32)
        acc_scr[...] = jnp.zeros(acc_scr.shape, F32)

    row = _iota((nr, MIX_HALF), 0)
    col = _iota((nr, MIX_HALF), 1)
    own = ((col // HEAD_DIM) == ((row % (nr // 2)) // t_new)) & (((col % HEAD_DIM) // MAP_DIM) == (row // (nr // 2)))
    own_head = (col // HEAD_DIM) == ((row % (nr // 2)) // t_new)
    qbd = jnp.where(own, q_ref[...], jnp.zeros((), BF16))

    is_last = c == n_chunks - 1
    rows_hd = N_HEADS * HEAD_DIM
    kt = jnp.concatenate([kbuf[slot, p].reshape(rows_hd, PAGE) for p in range(ch)], axis=-1).astype(BF16)
    s = _dot(qbd, kt)
    near = jnp.where(is_last, bnear_ref[...], 0.0)
    if ch > 1:
        s = jnp.concatenate([s[:, :(ch - 1) * PAGE], s[:, (ch - 1) * PAGE:] + near], axis=-1)
    else:
        s = s + near

    m_prev = m_scr[:, 0:1]
    l_prev = l_scr[:, 0:1]
    m_new = jnp.maximum(m_prev, jnp.max(s, axis=-1, keepdims=True))
    alpha = jnp.exp(m_prev - m_new)
    pr = jnp.exp(s - m_new)
    l_new = alpha * l_prev + jnp.sum(pr, axis=-1, keepdims=True)
    vt = jnp.concatenate([vbuf[slot, p].reshape(rows_hd, PAGE) for p in range(ch)], axis=-1).astype(BF16)
    acc = alpha * acc_scr[...] + _dot_nt(pr.astype(BF16), vt)
    m_scr[...] = jnp.broadcast_to(m_new, m_scr.shape)
    l_scr[...] = jnp.broadcast_to(l_new, l_scr.shape)
    acc_scr[...] = acc

    @pl.when(is_last)
    def _fin():
        sn = _dot_nt(qbd, kn_ref[...].astype(BF16)) + bnew_ref[...]
        m2 = jnp.maximum(m_new, jnp.max(sn, axis=-1, keepdims=True))
        a2 = jnp.exp(m_new - m2)
        p2 = jnp.exp(sn - m2)
        l2 = a2 * l_new + jnp.sum(p2, axis=-1, keepdims=True)
        acc2 = a2 * acc + _dot(p2.astype(BF16), vn_ref[...].astype(BF16))
        o_full = jnp.where(own_head, acc2 / l2, 0.0)
        fold = _one_hot_f((_iota((MIX_HALF, HEAD_DIM), 0) % HEAD_DIM) == _iota((MIX_HALF, HEAD_DIM), 1))
        o = _dot_x01(o_full, fold)
        half = nr // 2
        od = o[:half] - _lambda(lam_ref) * o[half:]
        o_ref[...] = _subln(od, sw_ref[...])


def _sample_attention(page_table, q_rep, k_new, v_new, cache_kt, cache_vt, bnear, bnew,
                      lam_vecs, subln_row, ch):
    n_seq, n_pages = page_table.shape
    assert n_pages % ch == 0
    n_chunks = n_pages // ch
    nr = q_rep.shape[1]
    t_pad = k_new.shape[1]
    body = functools.partial(_sattn_body, ch=ch, n_chunks=n_chunks, n_seq=n_seq)
    grid_spec = pltpu.PrefetchScalarGridSpec(
        num_scalar_prefetch=1,
        grid=(n_seq, n_chunks),
        in_specs=[
            pl.BlockSpec((None, nr, MIX_HALF), lambda b, c, pt: (b, 0, 0)),
            pl.BlockSpec((None, t_pad, MIX_HALF), lambda b, c, pt: (b, 0, 0)),
            pl.BlockSpec((None, t_pad, MIX_HALF), lambda b, c, pt: (b, 0, 0)),
            pl.BlockSpec((nr, PAGE), lambda b, c, pt: (0, 0)),
            pl.BlockSpec((nr, t_pad), lambda b, c, pt: (0, 0)),
            pl.BlockSpec((4, MAP_DIM), lambda b, c, pt: (0, 0)),
            pl.BlockSpec((1, HEAD_DIM), lambda b, c, pt: (0, 0)),
            pl.BlockSpec(memory_space=pl.ANY),
            pl.BlockSpec(memory_space=pl.ANY),
        ],
        out_specs=pl.BlockSpec((None, nr // 2, HEAD_DIM), lambda b, c, pt: (b, 0, 0)),
        scratch_shapes=[
            pltpu.VMEM((SAMPLE_PAGE_SLOTS, ch, N_HEADS, HEAD_DIM, PAGE), F32),
            pltpu.VMEM((SAMPLE_PAGE_SLOTS, ch, N_HEADS, HEAD_DIM, PAGE), F32),
            pltpu.SemaphoreType.DMA((2, SAMPLE_PAGE_SLOTS)),
            pltpu.VMEM((nr, LANES), F32),
            pltpu.VMEM((nr, LANES), F32),
            pltpu.VMEM((nr, MIX_HALF), F32),
        ],
    )
    return pl.pallas_call(
        body,
        grid_spec=grid_spec,
        out_shape=jax.ShapeDtypeStruct((n_seq, nr // 2, HEAD_DIM), F32),
        compiler_params=_params(("arbitrary", "arbitrary")),
        name="sample_attn",
    )(page_table, q_rep, k_new, v_new, bnear, bnew, lam_vecs, subln_row, cache_kt, cache_vt)


def _sample_bias_tables(rel_bias, t_new, t_pad):
    H = N_HEADS
    nr = 2 * H * t_new
    row = jnp.arange(nr, dtype=jnp.int32)
    row_h = (row % (H * t_new)) // t_new
    row_t = row % t_new
    pick = (row_h[:, None] == jnp.arange(H, dtype=jnp.int32)[None, :])

    def table(rel):
        allh = _rel_bias_shifted(rel_bias, rel)
        return jnp.sum(jnp.where(pick[:, None, :], allh, 0.0), axis=-1)

    pos = jnp.arange(PAGE, dtype=jnp.int32)
    bnear = table(PAGE + row_t[:, None] - pos[None, :])
    tn = jnp.arange(t_pad, dtype=jnp.int32)
    reln = row_t[:, None] - tn[None, :]
    ok = (reln >= 0) & (tn[None, :] < t_new)
    bnew = jnp.where(ok, table(reln), -jnp.inf)
    return bnear.astype(F32), bnew.astype(F32)


def _outproj_body(x_ref, gdn_ref, diff_ref, wo_ref, nw_ref, wr_ref, x2_ref, hp_ref, route_ref):
    x2 = (x_ref[...] + _dot(gdn_ref[...], wo_ref[:MIX_HALF, :])
          + _dot(diff_ref[...], wo_ref[MIX_HALF:, :]))
    x2_ref[...] = x2
    h = x2 * lax.rsqrt(jnp.mean(x2 * x2, axis=-1, keepdims=True) + NORM_EPS) * nw_ref[...]
    h1, h2 = _split2(h)
    logits = _dot(h1, wr_ref[0]) + (_dot(h1, wr_ref[1]) + _dot(h2, wr_ref[0]))
    tm = logits.shape[0]
    lane = _iota((tm, LANES), 1)
    neg = -jnp.inf
    big = jnp.int32(10_000)
    gl = jnp.where(lane < N_ROUTE_GROUPS, logits, neg)
    gm = jnp.max(gl, axis=-1, keepdims=True)
    g_idx = jnp.min(jnp.where(gl == gm, lane, big), axis=-1, keepdims=True)
    g_w = 1.0 / jnp.sum(jnp.exp(gl - gm), axis=-1, keepdims=True)
    lo = N_ROUTE_GROUPS + EXPERTS_PER_GROUP * g_idx
    el = jnp.where((lane >= lo) & (lane < lo + EXPERTS_PER_GROUP), logits, neg)
    em = jnp.max(el, axis=-1, keepdims=True)
    i1 = jnp.min(jnp.where(el == em, lane, big), axis=-1, keepdims=True)
    el2 = jnp.where(lane == i1, neg, el)
    em2 = jnp.max(el2, axis=-1, keepdims=True)
    i2 = jnp.min(jnp.where(el2 == em2, lane, big), axis=-1, keepdims=True)
    e2 = jnp.exp(em2 - em)
    w1 = g_w / (1.0 + e2)
    w2 = g_w * e2 / (1.0 + e2)
    route = jnp.where(lane == 0, (i1 - N_ROUTE_GROUPS).astype(F32),
                      jnp.where(lane == 1, (i2 - N_ROUTE_GROUPS).astype(F32),
                                jnp.where(lane == 2, w1, jnp.where(lane == 3, w2, 0.0))))
    route_ref[...] = route
    hb = h.astype(BF16).astype(F32)
    lo_bits = pltpu.bitcast(hb[:, :D_MODEL // 2], jnp.uint32) >> 16
    hi_bits = pltpu.bitcast(hb[:, D_MODEL // 2:], jnp.uint32) & jnp.uint32(0xFFFF0000)
    hp_ref[...] = lo_bits | hi_bits


def _outproj(x2d, gdn2d, diff2d, w_out16, norm_w, wr_pair, tm):
    n = x2d.shape[0]
    assert n % tm == 0
    row = lambda i: (i, 0)
    fixed = lambda i: (0, 0)
    return pl.pallas_call(
        _outproj_body,
        grid=(n // tm,),
        in_specs=[
            pl.BlockSpec((tm, D_MODEL), row),
            pl.BlockSpec((tm, MIX_HALF), row),
            pl.BlockSpec((tm, MIX_HALF), row),
            pl.BlockSpec((D_MODEL, D_MODEL), fixed),
            pl.BlockSpec((1, D_MODEL), fixed),
            pl.BlockSpec((2, D_MODEL, LANES), lambda i: (0, 0, 0)),
        ],
        out_specs=(pl.BlockSpec((tm, D_MODEL), row),
                   pl.BlockSpec((tm, D_MODEL // 2), row),
                   pl.BlockSpec((tm, LANES), row)),
        out_shape=(jax.ShapeDtypeStruct((n, D_MODEL), F32),
                   jax.ShapeDtypeStruct((n, D_MODEL // 2), jnp.uint32),
                   jax.ShapeDtypeStruct((n, LANES), F32)),
        compiler_params=_params(("parallel",)),
        name="outproj_router",
    )(x2d, gdn2d, diff2d, w_out16, norm_w, wr_pair)


def _moe_plan(e_idx, w, tile, max_blocks):
    n = e_idx.shape[0]
    n_tiles = n // tile
    m = 2 * tile
    experts = jnp.arange(N_EXPERTS, dtype=jnp.int32)
    e = e_idx.reshape(n_tiles, m)
    hit = e[..., None] == experts
    onehot = hit.astype(jnp.int32)
    sub = math.gcd(m, MOE_BLOCK)
    tri = (jnp.arange(sub)[:, None] >= jnp.arange(sub)[None, :]).astype(BF16)
    within = jnp.einsum("ij,tcje->tcie", tri, hit.astype(BF16).reshape(n_tiles, m // sub, sub, N_EXPERTS),
                        preferred_element_type=F32)
    chunk_tot = within[:, :, -1, :]
    chunk_off = jnp.cumsum(chunk_tot, axis=1) - chunk_tot
    csum = (within + chunk_off[:, :, None, :]).reshape(n_tiles, m, N_EXPERTS).astype(jnp.int32)
    counts = csum[:, -1, :]
    nb_e = (counts + MOE_BLOCK - 1) // MOE_BLOCK
    pad_ends = jnp.cumsum(nb_e, axis=1)
    pad_starts = pad_ends - nb_e
    nblk = pad_ends[:, -1]
    dest = jnp.sum(onehot * (pad_starts[:, None, :] * MOE_BLOCK + csum - 1), axis=-1)
    blk = jnp.arange(max_blocks, dtype=jnp.int32)
    block_e = jnp.sum((pad_ends[:, None, :] <= blk[None, :, None]).astype(jnp.int32), axis=-1)
    block_e = jnp.minimum(block_e, N_EXPERTS - 1)
    oh_b = (block_e[..., None] == experts).astype(jnp.int32)
    cnt_b = jnp.sum(oh_b * counts[:, None, :], axis=-1)
    ps_b = jnp.sum(oh_b * pad_starts[:, None, :], axis=-1)
    live = blk[None, :] < nblk[:, None]
    nvalid = jnp.where(live, jnp.clip(cnt_b - (blk[None, :] - ps_b) * MOE_BLOCK, 0, MOE_BLOCK), 0)
    last_e = jnp.max(jnp.where(counts > 0, experts[None, :], 0), axis=-1, keepdims=True)
    block_e = jnp.where(live, block_e, last_e)
    return (block_e.reshape(-1).astype(jnp.int32), nblk.astype(jnp.int32), nvalid.reshape(-1).astype(jnp.int32),
            dest.astype(jnp.int32).reshape(n_tiles, 1, m), w.astype(F32).reshape(n_tiles, 1, m))


MOE_TOKEN_TILE = 4096
_SCATTER_BATCH = 4
_INVERT_BATCH = 8
_NORM_ROWS = 256


def _moe_body(be_ref, nblk_ref, nvalid_ref, hp_ref, dest_ref, wa_ref, wg_ref, wu_ref, wd_ref,
              nf_ref, x2_hbm, y_ref, asg_s, xg_scr, yb_scr, sem, *, max_blocks):
    t = pl.program_id(0)
    b = pl.program_id(1)
    m = dest_ref.shape[1]
    tile = y_ref.shape[0]

    @pl.when(b == 0)
    def _start_tile():
        residual = pltpu.make_async_copy(x2_hbm.at[pl.ds(pl.multiple_of(t * tile, tile), tile)], y_ref, sem.at[0])
        residual.start()
        xg_scr[...] = jnp.zeros(xg_scr.shape, xg_scr.dtype)

        def invert(ab, carry):
            a0 = ab * _INVERT_BATCH
            dests = [dest_ref[0, a0 + u] for u in range(_INVERT_BATCH)]
            for u in range(_INVERT_BATCH):
                asg_s[dests[u]] = a0 + u
            return carry

        lax.fori_loop(0, m // _INVERT_BATCH, invert, 0)
        residual.wait()

    @pl.when(b < nblk_ref[t])
    def _block():
        base = b * MOE_BLOCK
        nv = nvalid_ref[t * max_blocks + b]

        def gather_group(ig, carry):
            i0 = ig * SUBLANES
            toks = [asg_s[base + jnp.minimum(i0 + u, nv - 1)] >> 1 for u in range(SUBLANES)]
            rows = [hp_ref[pl.ds(toks[u], 1), :] for u in range(SUBLANES)]
            for u in range(SUBLANES):
                xg_scr[pl.ds(i0 + u, 1), :] = rows[u]
            return carry

        lax.fori_loop(0, (nv + SUBLANES - 1) // SUBLANES, gather_group, 0)
        xp = xg_scr[...]
        half = D_MODEL // 2
        x_lo = pltpu.bitcast(xp << 16, F32).astype(BF16)
        x_hi = pltpu.bitcast(xp & jnp.uint32(0xFFFF0000), F32).astype(BF16)
        gt = _dot(x_lo, wg_ref[:half, :]) + _dot(x_hi, wg_ref[half:, :])
        up = _dot(x_lo, wu_ref[:half, :]) + _dot(x_hi, wu_ref[half:, :])
        act = (jax.nn.silu(gt) * up).astype(BF16)
        yb_scr[...] = _dot(act, wd_ref[...])

        def scatter_batch(ib, carry):
            i0 = ib * _SCATTER_BATCH
            asgs = [asg_s[base + i0 + u] for u in range(_SCATTER_BATCH)]
            toks = [a >> 1 for a in asgs]
            vals = [y_ref[pl.ds(toks[u], 1), :] + wa_ref[0, asgs[u]] * yb_scr[pl.ds(i0 + u, 1), :]
                    for u in range(_SCATTER_BATCH)]
            for u in range(_SCATTER_BATCH):
                y_ref[pl.ds(toks[u], 1), :] = vals[u]
            return carry

        nfull = nv // _SCATTER_BATCH
        lax.fori_loop(0, nfull, scatter_batch, 0)

        def scatter_one(i, carry):
            a = asg_s[base + i]
            tok = a >> 1
            y_ref[pl.ds(tok, 1), :] = y_ref[pl.ds(tok, 1), :] + wa_ref[0, a] * yb_scr[pl.ds(i, 1), :]
            return carry

        lax.fori_loop(nfull * _SCATTER_BATCH, nv, scatter_one, 0)

    @pl.when(b == max_blocks - 1)
    def _final_norm():
        rows = min(tile, _NORM_ROWS)

        def norm_rows(i, carry):
            sl = pl.ds(pl.multiple_of(i * rows, rows), rows)
            x = y_ref[sl, :]
            y_ref[sl, :] = x * lax.rsqrt(jnp.mean(x * x, axis=-1, keepdims=True) + NORM_EPS) * nf_ref[...]
            return carry

        lax.fori_loop(0, tile // rows, norm_rows, 0)


def _moe(x2, hp, plan, wg16, wu16, wd16, norm_final_row, tile, max_blocks):
    n = hp.shape[0]
    n_tiles = n // tile
    block_e, nblk, nvalid, dest, wa = plan
    m = 2 * tile
    body = functools.partial(_moe_body, max_blocks=max_blocks)
    half = D_MODEL // 2
    wmap = lambda t, b, be, nb, nv: (be[t * max_blocks + b], 0, 0)
    tmap = lambda t, b, be, nb, nv: (t, 0, 0)
    grid_spec = pltpu.PrefetchScalarGridSpec(
        num_scalar_prefetch=3,
        grid=(n_tiles, max_blocks),
        in_specs=[
            pl.BlockSpec((tile, half), lambda t, b, be, nb, nv: (t, 0), pipeline_mode=pl.Buffered(1)),
            pl.BlockSpec((None, 1, m), tmap, memory_space=pltpu.SMEM),
            pl.BlockSpec((None, 1, m), tmap, memory_space=pltpu.SMEM),
            pl.BlockSpec((None, D_MODEL, EXPERT_FF), wmap),
            pl.BlockSpec((None, D_MODEL, EXPERT_FF), wmap),
            pl.BlockSpec((None, EXPERT_FF, D_MODEL), wmap),
            pl.BlockSpec((1, D_MODEL), lambda t, b, be, nb, nv: (0, 0)),
            pl.BlockSpec(memory_space=pl.ANY),
        ],
        out_specs=pl.BlockSpec((tile, D_MODEL), lambda t, b, be, nb, nv: (t, 0), pipeline_mode=pl.Buffered(1)),
        scratch_shapes=[pltpu.SMEM((max_blocks * MOE_BLOCK,), jnp.int32),
                        pltpu.VMEM((MOE_BLOCK, half), jnp.uint32),
                        pltpu.VMEM((MOE_BLOCK, D_MODEL), F32),
                        pltpu.SemaphoreType.DMA((1,))],
    )
    return pl.pallas_call(
        body,
        grid_spec=grid_spec,
        out_shape=jax.ShapeDtypeStruct((n, D_MODEL), F32),
        compiler_params=_params(("arbitrary", "arbitrary")),
        name="moe",
    )(block_e, nblk, nvalid, hp, dest, wa, wg16, wu16, wd16, norm_final_row, x2)


def _pick_tile(n, pref):
    t = min(n, pref)
    while n % t:
        t //= 2
    return t


def _ffn(x2d, gdn2d, diff2d, w_out16, norm_ffn_row, wr_pair, wg16, wu16, wd16, norm_final_row):
    n = x2d.shape[0]
    tm = _pick_tile(n, 512)
    x2, hp, route = _outproj(x2d, gdn2d, diff2d, w_out16, norm_ffn_row, wr_pair, tm)
    e_idx = route[:, 0:2].astype(jnp.int32)
    w = route[:, 2:4]
    tile = _pick_tile(n, MOE_TOKEN_TILE)
    max_blocks = 2 * tile // MOE_BLOCK + N_EXPERTS
    plan = _moe_plan(e_idx, w, tile, max_blocks)
    return _moe(x2, hp, plan, wg16, wu16, wd16, norm_final_row, tile, max_blocks)


def kernel(x_prompt, x_sample, cache_k, cache_v, state_conv, state_gdn, page_table, norm_mix, w_in, conv_w, gdn_a_log, gdn_dt_bias, gdn_norm, lambda_q1, lambda_k1, lambda_q2, lambda_k2, diff_subln, rel_bias, w_out, norm_ffn, router_group, router_expert, w_gate, w_up, w_down, norm_final):
    bsz, seq, _ = x_prompt.shape
    dbsz, dseq, _ = x_sample.shape
    H = N_HEADS
    l = 0

    wl = w_in[l]
    n_ab = 2 * H
    w_packed = jnp.concatenate(
        [wl[:, :_C_GATE[1]], wl[:, _C_GATE[1]:_C_GATE[1] + n_ab],
         jnp.zeros((D_MODEL, LANES - n_ab), wl.dtype), wl[:, _C_GATE[1] + n_ab:]], axis=1).astype(BF16)
    norm_mix_row = norm_mix[l].reshape(1, D_MODEL)
    pad_h = jnp.zeros((LANES - H,), F32)
    alog_row = jnp.concatenate([gdn_a_log[l].astype(F32), pad_h]).reshape(1, LANES)
    dtb_row = jnp.concatenate([gdn_dt_bias[l].astype(F32), pad_h]).reshape(1, LANES)
    gdn_norm_row = jnp.tile(gdn_norm[l].astype(F32), HEADS_PER_GROUP).reshape(1, GROUP_W)
    lam_vecs = jnp.stack([lambda_q1[l], lambda_k1[l], lambda_q2[l], lambda_k2[l]]).astype(F32)
    subln_row = diff_subln[l].astype(F32).reshape(1, HEAD_DIM)
    w_out16 = w_out[l].astype(BF16)
    norm_ffn_row = norm_ffn[l].reshape(1, D_MODEL)
    wr = jnp.concatenate([router_group[l], router_expert[l].reshape(D_MODEL, N_EXPERTS),
                          jnp.zeros((D_MODEL, LANES - N_ROUTE_GROUPS - N_EXPERTS), F32)], axis=1)
    wr1 = wr.astype(BF16)
    wr_pair = jnp.stack([wr1, (wr - wr1.astype(F32)).astype(BF16)])
    wg16 = w_gate[l].astype(BF16)
    wu16 = w_up[l].astype(BF16)
    wd16 = w_down[l].astype(BF16)
    norm_final_row = norm_final.reshape(1, D_MODEL)

    n_p = bsz * seq
    xp2d = x_prompt.reshape(n_p, D_MODEL)
    wkv_t = jnp.stack([wl[:, _C_GATE[1] + n_ab + MIX_HALF:_C_GATE[1] + n_ab + 2 * MIX_HALF].T,
                       wl[:, _C_GATE[1] + n_ab + 2 * MIX_HALF:].T]).astype(BF16)
    zqkv, zgate, zab, kf_t, vf_t, _, qh, kh, vh = _inproj(xp2d, norm_mix_row, w_packed, wkv_t,
                                                          _pick_tile(seq, 256), seq_len=seq)
    conv_zero = jnp.zeros((bsz, SUBLANES, CONV_DIM), F32)
    gdn_p, s_p = _gdn(zqkv.reshape(bsz, seq, CONV_DIM), zgate.reshape(bsz, seq, MIX_HALF),
                      zab.reshape(bsz, seq, LANES), conv_w[l], alog_row, dtb_row, gdn_norm_row,
                      conv_zero, None, seq, GDN_CHUNK)
    tq = 256
    bias_tab = _prompt_bias_tables(rel_bias, tq)
    hm = lambda a: a.reshape(H, bsz, seq, a.shape[-1])
    diff_p = _prompt_attention(hm(qh), hm(kh), hm(vh), bias_tab, lam_vecs, subln_row, tq)
    y_prompt = _ffn(xp2d, gdn_p.reshape(n_p, MIX_HALF), diff_p.reshape(n_p, MIX_HALF), w_out16,
                    norm_ffn_row, wr_pair, wg16, wu16, wd16, norm_final_row).reshape(bsz, seq, D_MODEL)
    tok_major = lambda a: jnp.transpose(a.reshape(bsz, H, HEAD_DIM, seq), (0, 3, 1, 2))[None]
    new_k_prompt = tok_major(kf_t)
    new_v_prompt = tok_major(vf_t)
    new_conv_prompt = zqkv.reshape(bsz, seq, CONV_DIM)[:, seq - (CONV_W - 1):, :][None]
    new_gdn_prompt = s_p[None]

    n_s = dbsz * dseq
    xs2d = x_sample.reshape(n_s, D_MODEL)
    zqkv_s, zgate_s, zab_s, kf_s, vf_s, qf_s, _, _, _ = _inproj(xs2d, norm_mix_row, w_packed, wkv_t,
                                                                _pick_tile(n_s, 256))
    C = GDN_CHUNK_SAMPLE
    assert dseq <= C
    padt = lambda a: jnp.pad(a.reshape(dbsz, dseq, -1), ((0, 0), (0, C - dseq), (0, 0)))
    conv_init = jnp.pad(state_conv[l].astype(F32), ((0, 0), (SUBLANES - (CONV_W - 1), 0), (0, 0)))
    gdn_s, s_s = _gdn(padt(zqkv_s), padt(zgate_s), padt(zab_s), conv_w[l], alog_row, dtb_row,
                      gdn_norm_row, conv_init, state_gdn[l].astype(F32), dseq, C)
    gdn_s = gdn_s[:, :dseq, :].reshape(n_s, MIX_HALF)
    t_pad = BF16_ROWS
    q_rep = jnp.broadcast_to(qf_s.reshape(dbsz, 1, 1, dseq, MIX_HALF),
                             (dbsz, 2, H, dseq, MIX_HALF)).reshape(dbsz, 2 * H * dseq, MIX_HALF)
    newpad = lambda a: jnp.pad(a.reshape(dbsz, dseq, MIX_HALF), ((0, 0), (0, t_pad - dseq), (0, 0)))
    bnear, bnew = _sample_bias_tables(rel_bias, dseq, t_pad)
    cache_kt = jnp.transpose(cache_k, (0, 1, 3, 4, 2))
    cache_vt = jnp.transpose(cache_v, (0, 1, 3, 4, 2))
    n_pages = page_table.shape[1]
    ch = 16 if n_pages % 16 == 0 else 1
    diff_s = _sample_attention(page_table, q_rep, newpad(kf_s), newpad(vf_s), cache_kt, cache_vt,
                               bnear, bnew, lam_vecs, subln_row, ch)
    diff_s = jnp.transpose(diff_s.reshape(dbsz, H, dseq, HEAD_DIM), (0, 2, 1, 3)).reshape(n_s, MIX_HALF)
    y_sample = _ffn(xs2d, gdn_s, diff_s.astype(BF16), w_out16, norm_ffn_row, wr_pair, wg16, wu16, wd16,
                    norm_final_row).reshape(dbsz, dseq, D_MODEL)
    new_k_sample = kf_s.reshape(1, dbsz, dseq, H, HEAD_DIM)
    new_v_sample = vf_s.reshape(1, dbsz, dseq, H, HEAD_DIM)
    zq3 = zqkv_s.reshape(dbsz, dseq, CONV_DIM)
    new_conv_sample = jnp.concatenate([state_conv[l].astype(F32), zq3], axis=1)[:, -(CONV_W - 1):, :][None]
    new_gdn_sample = s_s[None]

    return (y_prompt, y_sample, new_k_prompt, new_v_prompt, new_k_sample, new_v_sample,
            new_conv_prompt, new_conv_sample, new_gdn_prompt, new_gdn_sample)
```
